```python
import jax, jax.numpy as jnp
from jax import lax
import numpy as np

D_MODEL = 1024
BATCH = 8
SEQ = 2048
DEPTH = 1
DEC_BATCH = 128
DEC_SEQ = 1
PAST_LEN = 16384
PAGE_SIZE = 128

RET_HEADS = 4
RET_DK = 256
RET_DV = 512
RET_QK = RET_HEADS * RET_DK
RET_V = RET_HEADS * RET_DV
RET_CHUNK = 128
ROPE_BASE = 10000.0
RW_HEAD = 64
RW_HEADS = D_MODEL // RW_HEAD
RW_C = RW_HEADS * RW_HEAD
RW_DECAY_LORA = 64
RW_AAA_LORA = 64
RW_GN_EPS = 1e-5 * RW_HEAD
N_SHIFT = 3 * RW_C + RW_DECAY_LORA + RW_AAA_LORA
N_COLS = 2 * RET_QK + 2 * RET_V + N_SHIFT + RW_C + 2 * D_MODEL
PLE_DIM = 256
NORM_EPS = 1e-6

kernel_name = 'retention_rwkv7_gated_hybrid_step'


def _rmsnorm(x, g):
    xf = x.astype(jnp.float32)
    y = xf * lax.rsqrt(jnp.mean(xf * xf, axis=-1, keepdims=True) + NORM_EPS)
    return (y * g.astype(jnp.float32)).astype(x.dtype)


def _rope(x, pos):
    half = x.shape[-1] // 2
    inv = ROPE_BASE ** (-jnp.arange(half, dtype=jnp.float32) / half)
    ang = pos[:, None] * inv[None, :]
    cos = jnp.cos(ang)[None, :, None, :]
    sin = jnp.sin(ang)[None, :, None, :]
    x1, x2 = x[..., :half], x[..., half:]
    return jnp.concatenate([x1 * cos - x2 * sin, x2 * cos + x1 * sin], axis=-1)


def _retention(q, k, v, s0, log_g):
    B, H, T, _ = q.shape
    C = RET_CHUNK if T % RET_CHUNK == 0 else T
    NC = T // C
    idx = jnp.arange(C, dtype=jnp.float32)
    rel = idx[:, None] - idx[None, :]
    lg = log_g[:, None, None]
    dmask = jnp.where(rel[None] >= 0, jnp.exp(jnp.maximum(rel, 0.0)[None] * lg), 0.0)
    q_dec = jnp.exp((idx + 1.0)[None, :] * log_g[:, None])[:, :, None]
    k_dec = jnp.exp((C - 1.0 - idx)[None, :] * log_g[:, None])[:, :, None]
    chunk_dec = jnp.exp(C * log_g)[:, None, None]

    def chunk(s, qkv):
        qc, kc, vc = qkv
        inner = jnp.einsum('bhid,bhjd->bhij', qc, kc) * dmask
        o = jnp.einsum('bhij,bhjv->bhiv', inner, vc) + jnp.einsum('bhid,bhdv->bhiv', qc * q_dec, s)
        s = chunk_dec * s + jnp.einsum('bhjd,bhjv->bhdv', kc * k_dec, vc)
        return s, o

    split = lambda t: jnp.moveaxis(t.reshape(B, H, NC, C, t.shape[-1]), 2, 0)
    s, o = lax.scan(chunk, s0, (split(q), split(k), split(v)))
    o = jnp.moveaxis(o, 0, 2).reshape(B, H, T, v.shape[-1])
    return o, s


def _retention_branch(q, k, v, g_a, pos0, s0):
    f32 = jnp.float32
    B, T = q.shape[:2]
    pos = pos0 + jnp.arange(T, dtype=f32)
    q = _rope(q.reshape(B, T, RET_HEADS, RET_DK).astype(f32), pos)
    k = _rope(k.reshape(B, T, RET_HEADS, RET_DK).astype(f32), pos) * (RET_DK ** -0.5)
    v = v.reshape(B, T, RET_HEADS, RET_DV).astype(f32)
    tr = lambda t: jnp.transpose(t, (0, 2, 1, 3))
    log_g = jnp.log(1.0 - jnp.exp2(-5.0 - jnp.arange(RET_HEADS, dtype=f32)))
    o, s = _retention(tr(q), tr(k), tr(v), s0.astype(f32), log_g)
    o = o * lax.rsqrt(jnp.mean(o * o, axis=-1, keepdims=True) + NORM_EPS)
    o = tr(o).reshape(B, T, RET_V)
    return o * jax.nn.silu(g_a.astype(f32)), s


def _rwkv_scan(r, w, k, v, kk, a, s0):
    def step(s, inp):
        r_t, w_t, k_t, v_t, kk_t, a_t = inp
        sk = jnp.einsum('bhvk,bhk->bhv', s, kk_t)
        s = s * w_t[:, :, None, :] - sk[..., None] * (a_t * kk_t)[:, :, None, :] + v_t[..., None] * k_t[:, :, None, :]
        return s, jnp.einsum('bhvk,bhk->bhv', s, r_t)

    xs = tuple(jnp.moveaxis(t, 1, 0) for t in (r, w, k, v, kk, a))
    s, o = lax.scan(step, s0, xs)
    return jnp.moveaxis(o, 0, 1), s


def _rwkv_branch(sh, sh_prev, g_b, mu, w0, w2, a0, a2, k_k, k_a, r_k, ln_w, ln_b, s0):
    f32 = jnp.float32
    cur = sh.astype(f32)
    z = cur + (sh_prev.astype(f32) - cur) * mu.astype(f32)
    r, k, v, wl, al = jnp.split(z, [RW_C, 2 * RW_C, 3 * RW_C, 3 * RW_C + RW_DECAY_LORA], axis=-1)
    w = -jax.nn.softplus(-(w0.astype(f32) + jnp.tanh(wl) @ w2.astype(f32))) - 0.5
    decay = jnp.exp(-jnp.exp(w))
    a = jax.nn.sigmoid(a0.astype(f32) + al @ a2.astype(f32))
    B, T = r.shape[:2]
    hs = lambda t: t.reshape(B, T, RW_HEADS, RW_HEAD)
    kk = hs(k * k_k.astype(f32))
    kk = kk / jnp.maximum(jnp.sqrt(jnp.sum(kk * kk, axis=-1, keepdims=True)), 1e-12)
    k = k * (1.0 + (a - 1.0) * k_a.astype(f32))
    r, k, v, decay, a = hs(r), hs(k), hs(v), hs(decay), hs(a)
    o, s = _rwkv_scan(r, decay, k, v, kk, a, s0.astype(f32))
    mean = jnp.mean(o, axis=-1, keepdims=True)
    var = jnp.mean(jnp.square(o - mean), axis=-1, keepdims=True)
    o = ((o - mean) * lax.rsqrt(var + RW_GN_EPS)).reshape(B, T, RW_C) * ln_w.astype(f32) + ln_b.astype(f32)
    bonus = jnp.sum(r * k * r_k.astype(f32), axis=-1, keepdims=True) * v
    o = o + bonus.reshape(B, T, RW_C)
    return o * jax.nn.silu(g_b.astype(f32)), s


def _layer(x, h_prev, s_ret, s_rw, p, pos0, norm_g, w_in, rw_mu, rw_w0, rw_w2, rw_a0, rw_a2, rw_k_k, rw_k_a, rw_r_k, rw_ln_w, rw_ln_b, w_down_a, w_down_b, w_out, w_ple, ple_norm_g, w_ple_gate):
    h = _rmsnorm(x, norm_g)
    hf = jnp.concatenate([h_prev[:, None, :].astype(h.dtype), h], axis=1)
    p_full = jnp.einsum('btd,dn->btn', hf, w_in)
    P = p_full[:, 1:]
    o1 = RET_QK
    o2 = 2 * RET_QK
    o3 = o2 + RET_V
    o4 = o3 + RET_V
    o5 = o4 + N_SHIFT
    o6 = o5 + RW_C
    o7 = o6 + D_MODEL
    sh_prev = p_full[:, :-1, o4:o5]
    q, k, v, g_a, sh, g_b, m_a, m_b = jnp.split(P, [o1, o2, o3, o4, o5, o6, o7], axis=-1)
    y_a, s_ret_new = _retention_branch(q, k, v, g_a, pos0, s_ret)
    y_b, s_rw_new = _rwkv_branch(sh, sh_prev, g_b, rw_mu, rw_w0, rw_w2, rw_a0, rw_a2, rw_k_k, rw_k_a, rw_r_k, rw_ln_w, rw_ln_b, s_rw)
    merged = jax.nn.sigmoid(m_a.astype(jnp.float32)) * (y_a @ w_down_a) + jax.nn.sigmoid(m_b.astype(jnp.float32)) * (y_b @ w_down_b)
    x = x + (merged @ w_out).astype(x.dtype)
    gate = jax.nn.sigmoid(_rmsnorm(x, ple_norm_g) @ w_ple_gate)
    x = x + ((p @ w_ple) * gate).astype(x.dtype)
    return x, h[:, -1], s_ret_new.astype(s_ret.dtype), s_rw_new.astype(s_rw.dtype)


def setup_inputs(seed: int = 0) -> dict:
    key = jax.random.key(seed)
    ks = jax.random.split(key, 32)
    f32 = jnp.float32
    nrm = lambda k, shape, s: jax.random.normal(k, shape, f32) * s
    L = DEPTH
    return {
        'x_prompt': nrm(ks[0], (BATCH, SEQ, D_MODEL), 1.0),
        'x_sample': nrm(ks[1], (DEC_BATCH, DEC_SEQ, D_MODEL), 1.0),
        'state_ret': nrm(ks[2], (L, DEC_BATCH, RET_HEADS, RET_DK, RET_DV), 1.0),
        'state_rwkv': nrm(ks[3], (L, DEC_BATCH, RW_HEADS, RW_HEAD, RW_HEAD), 1.0),
        'state_shift': nrm(ks[4], (L, DEC_BATCH, D_MODEL), 1.0),
        'p_prompt': nrm(ks[5], (L, BATCH, SEQ, PLE_DIM), 1.0),
        'p_sample': nrm(ks[6], (L, DEC_BATCH, DEC_SEQ, PLE_DIM), 1.0),
        'norm_g': 1.0 + nrm(ks[7], (L, D_MODEL), 0.02),
        'w_in': nrm(ks[8], (L, D_MODEL, N_COLS), D_MODEL ** -0.5),
        'rw_mu': jax.random.uniform(ks[9], (L, N_SHIFT), f32),
        'rw_w0': jax.random.uniform(ks[10], (L, RW_C), f32, -6.0, -0.5),
        'rw_w2': nrm(ks[11], (L, RW_DECAY_LORA, RW_C), 0.1),
        'rw_a0': nrm(ks[12], (L, RW_C), 0.1),
        'rw_a2': nrm(ks[13], (L, RW_AAA_LORA, RW_C), RW_AAA_LORA ** -0.5),
        'rw_k_k': 0.85 + nrm(ks[14], (L, RW_C), 0.05),
        'rw_k_a': 1.0 + nrm(ks[15], (L, RW_C), 0.05),
        'rw_r_k': nrm(ks[16], (L, RW_HEADS, RW_HEAD), 0.1),
        'rw_ln_w': 1.0 + nrm(ks[17], (L, RW_C), 0.02),
        'rw_ln_b': nrm(ks[18], (L, RW_C), 0.02),
        'w_down_a': nrm(ks[19], (L, RET_V, D_MODEL), RET_V ** -0.5),
        'w_down_b': nrm(ks[20], (L, RW_C, D_MODEL), RW_C ** -0.5),
        'w_out': nrm(ks[21], (L, D_MODEL, D_MODEL), D_MODEL ** -0.5),
        'w_ple': nrm(ks[22], (L, PLE_DIM, D_MODEL), PLE_DIM ** -0.5),
        'ple_norm_g': 1.0 + nrm(ks[23], (L, D_MODEL), 0.02),
        'w_ple_gate': nrm(ks[24], (L, D_MODEL, D_MODEL), D_MODEL ** -0.5),
        'final_norm_g': 1.0 + nrm(ks[25], (D_MODEL,), 0.02),
    }


def reference(x_prompt, x_sample, state_ret, state_rwkv, state_shift, p_prompt, p_sample, norm_g, w_in, rw_mu, rw_w0, rw_w2, rw_a0, rw_a2, rw_k_k, rw_k_a, rw_r_k, rw_ln_w, rw_ln_b, w_down_a, w_down_b, w_out, w_ple, ple_norm_g, w_ple_gate, final_norm_g):
    xp, xs = x_prompt, x_sample
    pr_ret, pr_rw, pr_sh, sa_ret, sa_rw, sa_sh = [], [], [], [], [], []
    for i in range(DEPTH):
        lw = (norm_g[i], w_in[i], rw_mu[i], rw_w0[i], rw_w2[i], rw_a0[i], rw_a2[i], rw_k_k[i], rw_k_a[i], rw_r_k[i], rw_ln_w[i], rw_ln_b[i], w_down_a[i], w_down_b[i], w_out[i], w_ple[i], ple_norm_g[i], w_ple_gate[i])
        h0 = jnp.zeros((BATCH, D_MODEL), state_shift.dtype)
        r0 = jnp.zeros((BATCH, RET_HEADS, RET_DK, RET_DV), state_ret.dtype)
        w0s = jnp.zeros((BATCH, RW_HEADS, RW_HEAD, RW_HEAD), state_rwkv.dtype)
        xp, hp, rp, wp = _layer(xp, h0, r0, w0s, p_prompt[i], 0, *lw)
        xs, hs_, rs, ws = _layer(xs, state_shift[i], state_ret[i], state_rwkv[i], p_sample[i], PAST_LEN, *lw)
        pr_ret.append(rp)
        pr_rw.append(wp)
        pr_sh.append(hp.astype(state_shift.dtype))
        sa_ret.append(rs)
        sa_rw.append(ws)
        sa_sh.append(hs_.astype(state_shift.dtype))
    y_prompt = _rmsnorm(xp, final_norm_g)
    y_sample = _rmsnorm(xs, final_norm_g)
    return (y_prompt, y_sample, jnp.stack(pr_ret), jnp.stack(pr_rw), jnp.stack(pr_sh), jnp.stack(sa_ret), jnp.stack(sa_rw), jnp.stack(sa_sh))
```

```python
import functools
import math

import jax
import jax.numpy as jnp
from jax import lax
from jax.experimental import pallas as pl
from jax.experimental.pallas import tpu as pltpu

F32 = jnp.float32
BF16 = jnp.bfloat16

D_MODEL = 1024
RET_HEADS = 4
RET_DK = 256
RET_DV = 512
RET_QK = RET_HEADS * RET_DK
RET_V = RET_HEADS * RET_DV
RET_CHUNK = 128
ROPE_BASE = 10000.0
RW_HEAD = 64
RW_HEADS = D_MODEL // RW_HEAD
RW_C = RW_HEADS * RW_HEAD
RW_LORA = 64
RW_GN_EPS = 1e-5 * RW_HEAD
RW_CHUNK = 64
PLE_DIM = 256
NORM_EPS = 1e-6
PAST_LEN = 16384

LANES = 128
MXU_DIM = 256
HEADS_PER_GROUP = MXU_DIM // RW_HEAD
N_GROUPS = RW_C // MXU_DIM
HEAD_SHIFT = RW_HEAD.bit_length() - 1
VMEM_LIMIT_BYTES = 56 * 1024 * 1024


def _params(*sem):
    return pltpu.CompilerParams(dimension_semantics=sem, vmem_limit_bytes=VMEM_LIMIT_BYTES)


def _dot(a, b):
    return jnp.dot(a.astype(BF16), b.astype(BF16), preferred_element_type=F32)


def _dot_nt(a, b):
    return lax.dot_general(a.astype(BF16), b.astype(BF16), (((1,), (1,)), ((), ())),
                           preferred_element_type=F32)


def _dot_tn(a, b):
    return lax.dot_general(a.astype(BF16), b.astype(BF16), (((0,), (0,)), ((), ())),
                           preferred_element_type=F32)


def _sigmoid(x):
    return 1.0 / (1.0 + jnp.exp(-x))


def _silu(x):
    return x * _sigmoid(x)


def _rms(x, g):
    return x * lax.rsqrt(jnp.mean(x * x, axis=-1, keepdims=True) + NORM_EPS) * g


def _rmsnorm_kernel(x_ref, g_ref, o_ref):
    o_ref[...] = _rms(x_ref[...], g_ref[...]).astype(o_ref.dtype)


def _rmsnorm(x, g, out_dtype, tm):
    m, d = x.shape
    return pl.pallas_call(
        _rmsnorm_kernel,
        grid=(m // tm,),
        in_specs=[pl.BlockSpec((tm, d), lambda i: (i, 0)), pl.BlockSpec((1, d), lambda i: (0, 0))],
        out_specs=pl.BlockSpec((tm, d), lambda i: (i, 0)),
        out_shape=jax.ShapeDtypeStruct((m, d), out_dtype),
        compiler_params=_params("arbitrary"),
        name="rmsnorm",
    )(x, g.reshape(1, d))


def _matmul_kernel(x_ref, w_ref, o_ref):
    o_ref[...] = jnp.dot(x_ref[...], w_ref[...], preferred_element_type=F32).astype(o_ref.dtype)


def _matmul(x, w, tm, out_dtype=F32, name="matmul"):
    m, k = x.shape
    n = w.shape[1]
    return pl.pallas_call(
        _matmul_kernel,
        grid=(m // tm,),
        in_specs=[pl.BlockSpec((tm, k), lambda i: (i, 0)), pl.BlockSpec((k, n), lambda i: (0, 0))],
        out_specs=pl.BlockSpec((tm, n), lambda i: (i, 0)),
        out_shape=jax.ShapeDtypeStruct((m, n), out_dtype),
        compiler_params=_params("arbitrary"),
        name=name,
    )(x, w)


def _rope(x, cos, sin):
    half = x.shape[-1] // 2
    x1, x2 = x[:, :half], x[:, half:]
    return jnp.concatenate([x1 * cos - x2 * sin, x2 * cos + x1 * sin], axis=-1)


def _rope_tables(pos):
    half = RET_DK // 2
    inv = ROPE_BASE ** (-jnp.arange(half, dtype=F32) / half)
    ang = pos[:, None] * inv[None, :]
    return jnp.cos(ang), jnp.sin(ang)


def _ret_chunk_kernel(dec_ref, q_ref, k_ref, v_ref, ga_ref, cos_ref, sin_ref, y_ref, s_ref):
    c = pl.program_id(1)
    C = RET_CHUNK

    @pl.when(c == 0)
    def _():
        s_ref[...] = jnp.zeros_like(s_ref)

    ti = lax.broadcasted_iota(jnp.int32, (C, C), 0)
    tj = lax.broadcasted_iota(jnp.int32, (C, C), 1)
    rel = (ti - tj).astype(F32)
    idx = lax.broadcasted_iota(jnp.int32, (C, 1), 0).astype(F32)
    cos = cos_ref[...]
    sin = sin_ref[...]
    for h in range(RET_HEADS):
        lg = dec_ref[2 * h]
        chunk_dec = dec_ref[2 * h + 1]
        dmask = jnp.where(rel >= 0, jnp.exp(jnp.maximum(rel, 0.0) * lg), 0.0)
        q_dec = jnp.exp((idx + 1.0) * lg)
        k_dec = jnp.exp((C - 1.0 - idx) * lg)
        q = _rope(q_ref[:, h * RET_DK:(h + 1) * RET_DK], cos, sin)
        k = _rope(k_ref[:, h * RET_DK:(h + 1) * RET_DK], cos, sin) * (RET_DK ** -0.5)
        v = v_ref[:, h * RET_DV:(h + 1) * RET_DV]
        s = s_ref[h]
        inner = _dot_nt(q, k) * dmask
        o = _dot(inner, v) + _dot(q * q_dec, s)
        s_ref[h] = chunk_dec * s + _dot_tn(k * k_dec, v)
        o = o * lax.rsqrt(jnp.mean(o * o, axis=-1, keepdims=True) + NORM_EPS)
        y_ref[:, h * RET_DV:(h + 1) * RET_DV] = (
            o * _silu(ga_ref[:, h * RET_DV:(h + 1) * RET_DV])).astype(y_ref.dtype)


def _ret_decay_table():
    log_g = jnp.log(1.0 - jnp.exp2(-5.0 - jnp.arange(RET_HEADS, dtype=F32)))
    return log_g


def _retention_prompt(qk, v, ga, batch, seq):
    C = RET_CHUNK
    nc = seq // C
    log_g = _ret_decay_table()
    dec = jnp.stack([log_g, jnp.exp(C * log_g)], axis=1).reshape(-1)
    half = RET_DK // 2
    cos, sin = _rope_tables(jnp.arange(seq, dtype=F32))
    row = lambda b, c: (b * nc + c, 0)
    return pl.pallas_call(
        _ret_chunk_kernel,
        grid=(batch, nc),
        in_specs=[
            pl.BlockSpec(memory_space=pltpu.SMEM),
            pl.BlockSpec((C, RET_QK), lambda b, c: (b * nc + c, 0)),
            pl.BlockSpec((C, RET_QK), lambda b, c: (b * nc + c, 1)),
            pl.BlockSpec((C, RET_V), row),
            pl.BlockSpec((C, RET_V), row),
            pl.BlockSpec((C, half), lambda b, c: (c, 0)),
            pl.BlockSpec((C, half), lambda b, c: (c, 0)),
        ],
        out_specs=[
            pl.BlockSpec((C, RET_V), row),
            pl.BlockSpec((None, RET_HEADS, RET_DK, RET_DV), lambda b, c: (b, 0, 0, 0)),
        ],
        out_shape=[
            jax.ShapeDtypeStruct((batch * seq, RET_V), BF16),
            jax.ShapeDtypeStruct((batch, RET_HEADS, RET_DK, RET_DV), F32),
        ],
        compiler_params=_params("arbitrary", "arbitrary"),
        name="retention_chunk",
    )(dec, qk, qk, v, ga, cos, sin)


def _ret_step_kernel(dec_ref, q_ref, k_ref, v_ref, ga_ref, cos_ref, sin_ref, s_ref, y_ref, so_ref):
    row0 = lax.broadcasted_iota(jnp.int32, (8, 1), 0)
    cos = cos_ref[...]
    sin = sin_ref[...]
    for h in range(RET_HEADS):
        g = dec_ref[h]
        q = _rope(q_ref[:, h * RET_DK:(h + 1) * RET_DK], cos, sin)
        k = _rope(k_ref[:, h * RET_DK:(h + 1) * RET_DK], cos, sin) * (RET_DK ** -0.5)
        v = v_ref[:, h * RET_DV:(h + 1) * RET_DV]
        s = s_ref[h]
        qk = jnp.sum(q * k, axis=-1, keepdims=True)
        q8 = jnp.broadcast_to(q, (8, RET_DK))
        o = qk * v + g * _dot(q8, s)[0:1, :]
        k_hi = k.astype(BF16).astype(F32)
        k_lo = k - k_hi
        v_hi = v.astype(BF16).astype(F32)
        v_lo = v - v_hi
        k8 = jnp.where(row0 < 2, k_hi, jnp.where(row0 == 2, k_lo, 0.0))
        v8 = jnp.where((row0 == 0) | (row0 == 2), v_hi, jnp.where(row0 == 1, v_lo, 0.0))
        so_ref[h] = g * s + _dot_tn(k8, v8)
        o = o * lax.rsqrt(jnp.mean(o * o, axis=-1, keepdims=True) + NORM_EPS)
        y_ref[:, h * RET_DV:(h + 1) * RET_DV] = (
            o * _silu(ga_ref[:, h * RET_DV:(h + 1) * RET_DV])).astype(y_ref.dtype)


def _retention_sample(qk, v, ga, state):
    nb = qk.shape[0]
    g = jnp.exp(_ret_decay_table())
    cos, sin = _rope_tables(PAST_LEN + jnp.arange(1, dtype=F32))
    q, k = qk[:, :RET_QK], qk[:, RET_QK:]
    tab = pl.BlockSpec((1, RET_DK // 2), lambda b: (0, 0))
    r3 = lambda t: t.reshape(nb, 1, t.shape[-1])
    vec = lambda n: pl.BlockSpec((None, 1, n), lambda b: (b, 0, 0))
    st = pl.BlockSpec((None, RET_HEADS, RET_DK, RET_DV), lambda b: (b, 0, 0, 0))
    y, s_new = pl.pallas_call(
        _ret_step_kernel,
        grid=(nb,),
        in_specs=[pl.BlockSpec(memory_space=pltpu.SMEM), vec(RET_QK), vec(RET_QK), vec(RET_V), vec(RET_V),
                  tab, tab, st],
        out_specs=[vec(RET_V), st],
        out_shape=[jax.ShapeDtypeStruct((nb, 1, RET_V), BF16),
                   jax.ShapeDtypeStruct(state.shape, F32)],
        compiler_params=_params("arbitrary"),
        name="retention_step",
    )(g, r3(q), r3(k), r3(v), r3(ga), cos, sin, state)
    return y.reshape(nb, RET_V), s_new


def _group_sum(x, ones_bd):
    hi = x.astype(BF16)
    lo = (x - hi.astype(F32)).astype(BF16)
    return (jnp.dot(hi, ones_bd, preferred_element_type=F32)
            + jnp.dot(lo, ones_bd, preferred_element_type=F32))


def _head_sum(x, ones_bd):
    return jnp.concatenate(
        [_group_sum(x[:, g * MXU_DIM:(g + 1) * MXU_DIM], ones_bd) for g in range(N_GROUPS)], axis=-1)


def _rwkv_prep(rkv, rkv_prev, wa, wa_prev, mu_rkv, mu_wa, w0, w2p, a0, a2p, k_k, k_a, ones_bd):
    z = rkv + (rkv_prev - rkv) * mu_rkv
    zwa = wa + (wa_prev - wa) * mu_wa
    r = z[:, :RW_C]
    k = z[:, RW_C:2 * RW_C]
    v = z[:, 2 * RW_C:]
    wpre = w0 + _dot(jnp.tanh(zwa), w2p)
    log_decay = -math.exp(-0.5) * _sigmoid(wpre)
    a = _sigmoid(a0 + _dot(zwa, a2p))
    kk = k * k_k
    nrm = jnp.maximum(jnp.sqrt(_head_sum(kk * kk, ones_bd)), 1e-12)
    kk = kk / nrm
    k = k * (1.0 + (a - 1.0) * k_a)
    return r, log_decay, k, v, kk, a


def _rwkv_post(o, r, k, v, gb, r_k, ln_w, ln_b, ones_bd):
    mean = _head_sum(o, ones_bd) * (1.0 / RW_HEAD)
    d = o - mean
    var = _head_sum(d * d, ones_bd) * (1.0 / RW_HEAD)
    on = d * lax.rsqrt(var + RW_GN_EPS) * ln_w + ln_b
    bonus = _head_sum(r * k * r_k, ones_bd) * v
    return (on + bonus) * _silu(gb)


def _block_diag(x, head_of_lane):
    return jnp.concatenate(
        [jnp.where(head_of_lane == j, x, 0.0) for j in range(HEADS_PER_GROUP)], axis=0).astype(BF16)


def _rwkv_chunk_kernel(rkv_ref, wa_ref, gb_ref, mu_rkv_ref, mu_wa_ref, w0_ref, w2_ref, a0_ref, a2_ref,
                       kk_ref, ka_ref, rk_ref, lnw_ref, lnb_ref, ones_ref,
                       y_ref, so_ref, carry_rkv, carry_wa, state):
    c = pl.program_id(1)
    nc = pl.num_programs(1)
    C = RW_CHUNK

    @pl.when(c == 0)
    def _():
        carry_rkv[...] = jnp.zeros_like(carry_rkv)
        carry_wa[...] = jnp.zeros_like(carry_wa)
        state[...] = jnp.zeros_like(state)

    ones_bd = ones_ref[...]
    rkv = rkv_ref[...]
    wa = wa_ref[...]
    row = lax.broadcasted_iota(jnp.int32, (C, 1), 0)
    rkv_prev = jnp.where(row == 0, carry_rkv[...], pltpu.roll(rkv, 1, 0))
    wa_prev = jnp.where(row == 0, carry_wa[...], pltpu.roll(wa, 1, 0))
    carry_rkv[...] = rkv[C - 1:C, :]
    carry_wa[...] = wa[C - 1:C, :]

    r, lw, k, v, kk, a = _rwkv_prep(rkv, rkv_prev, wa, wa_prev, mu_rkv_ref[...], mu_wa_ref[...],
                                    w0_ref[...], w2_ref[...], a0_ref[...], a2_ref[...],
                                    kk_ref[...], ka_ref[...], ones_bd)

    ti = lax.broadcasted_iota(jnp.int32, (C, C), 0)
    tj = lax.broadcasted_iota(jnp.int32, (C, C), 1)
    tri = (ti >= tj).astype(BF16)
    lw_hi = lw.astype(BF16)
    lw_r = lw - lw_hi.astype(F32)
    lw_mid = lw_r.astype(BF16)
    lw_lo = (lw_r - lw_mid.astype(F32)).astype(BF16)
    cl = (jnp.dot(tri, lw_hi, preferred_element_type=F32)
          + jnp.dot(tri, lw_mid, preferred_element_type=F32)
          + jnp.dot(tri, lw_lo, preferred_element_type=F32))
    cl_last = cl[C - 1:C, :]
    e_incl = jnp.exp(cl)
    e_excl = jnp.exp(cl - lw)
    e_neg = jnp.exp(-cl)
    e_rem = jnp.exp(cl_last - cl)
    g_chunk = jnp.exp(cl_last)
    beta = a * kk
    ag = -kk * e_excl
    rg = r * e_incl
    bg = beta * e_neg
    kg = k * e_neg
    bg_c = beta * e_rem
    kg_c = k * e_rem

    lane = lax.broadcasted_iota(jnp.int32, (1, MXU_DIM), 1)
    head_of_lane = lane >> HEAD_SHIFT
    t_col = lax.broadcasted_iota(jnp.int32, (C, MXU_DIM), 0)
    i_lane = lax.broadcasted_iota(jnp.int32, (C, MXU_DIM), 1) & (RW_HEAD - 1)
    strict = t_col > i_lane
    incl = t_col >= i_lane
    eye = (t_col == i_lane).astype(F32)
    vrow_head = lax.broadcasted_iota(jnp.int32, (MXU_DIM, MXU_DIM), 0) >> HEAD_SHIFT
    klane_head = lax.broadcasted_iota(jnp.int32, (MXU_DIM, MXU_DIM), 1) >> HEAD_SHIFT
    same_head = vrow_head == klane_head

    outs = []
    for g in range(N_GROUPS):
        sl = slice(g * MXU_DIM, (g + 1) * MXU_DIM)
        s_bd = state[g]
        lhs = jnp.concatenate([ag[:, sl], rg[:, sl]], axis=0)
        ab = _dot_nt(lhs, _block_diag(bg[:, sl], head_of_lane))
        ak = _dot_nt(lhs, _block_diag(kg[:, sl], head_of_lane))
        n_ab = jnp.where(strict, ab[:C], 0.0)
        a_ak = jnp.where(strict, ak[:C], 0.0)
        a_rb = jnp.where(incl, ab[C:], 0.0)
        a_rk = jnp.where(incl, ak[C:], 0.0)
        t_inv = eye + n_ab
        n_pow = n_ab
        for _ in range(int(math.log2(C)) - 1):
            n_bd = _block_diag(n_pow, head_of_lane)
            n_pow = _dot(n_pow, n_bd)
            t_inv = t_inv + _dot(t_inv, _block_diag(n_pow, head_of_lane))
        v_g = v[:, sl]
        v_bd = _block_diag(v_g, head_of_lane)
        b_m = _dot_nt(ag[:, sl], s_bd) + _dot(a_ak, v_bd)
        p = _dot(t_inv, _block_diag(b_m, head_of_lane))
        o = _dot_nt(rg[:, sl], s_bd) + _dot(a_rb, _block_diag(p, head_of_lane)) + _dot(a_rk, v_bd)
        upd = _dot_tn(p, bg_c[:, sl]) + _dot_tn(v_g, kg_c[:, sl])
        state[g] = s_bd * g_chunk[:, sl] + jnp.where(same_head, upd, 0.0)
        outs.append(o)
    o = jnp.concatenate(outs, axis=-1)
    y = _rwkv_post(o, r, k, v, gb_ref[...], rk_ref[...], lnw_ref[...], lnb_ref[...], ones_bd)
    y_ref[...] = y.astype(y_ref.dtype)

    @pl.when(c == nc - 1)
    def _():
        for g in range(N_GROUPS):
            for j in range(HEADS_PER_GROUP):
                blk = slice(j * RW_HEAD, (j + 1) * RW_HEAD)
                so_ref[g * HEADS_PER_GROUP + j] = state[g, blk, blk]


def _rwkv_consts(p):
    row = lambda t: t.reshape(1, -1).astype(F32)
    zeros = jnp.zeros((RW_LORA, RW_C), F32)
    w2p = jnp.concatenate([p["rw_w2"], zeros], axis=0).astype(BF16)
    a2p = jnp.concatenate([zeros, p["rw_a2"]], axis=0).astype(BF16)
    hl = jnp.arange(MXU_DIM) // RW_HEAD
    ones_bd = (hl[:, None] == hl[None, :]).astype(BF16)
    mu = p["rw_mu"]
    return dict(mu_rkv=row(mu[:3 * RW_C]), mu_wa=row(mu[3 * RW_C:]), w0=row(p["rw_w0"]), w2p=w2p,
                a0=row(p["rw_a0"]), a2p=a2p, k_k=row(p["rw_k_k"]), k_a=row(p["rw_k_a"]),
                r_k=row(p["rw_r_k"]), ln_w=row(p["rw_ln_w"]), ln_b=row(p["rw_ln_b"]), ones_bd=ones_bd)


def _const_spec(arr, ngrid):
    zeros = (0,) * arr.ndim
    if ngrid == 1:
        return pl.BlockSpec(arr.shape, lambda i: zeros)
    return pl.BlockSpec(arr.shape, lambda i, j: zeros)


def _rwkv_prompt(rkv, wa, gb, cs, batch, seq):
    C = RW_CHUNK
    nc = seq // C
    row = lambda b, c: (b * nc + c, 0)
    consts = [cs[n] for n in ("mu_rkv", "mu_wa", "w0", "w2p", "a0", "a2p", "k_k", "k_a", "r_k",
                              "ln_w", "ln_b", "ones_bd")]
    return pl.pallas_call(
        _rwkv_chunk_kernel,
        grid=(batch, nc),
        in_specs=[pl.BlockSpec((C, 3 * RW_C), row), pl.BlockSpec((C, 2 * RW_LORA), row),
                  pl.BlockSpec((C, RW_C), row)] + [_const_spec(a, 2) for a in consts],
        out_specs=[pl.BlockSpec((C, RW_C), row),
                   pl.BlockSpec((None, RW_HEADS, RW_HEAD, RW_HEAD), lambda b, c: (b, 0, 0, 0))],
        out_shape=[jax.ShapeDtypeStruct((batch * seq, RW_C), BF16),
                   jax.ShapeDtypeStruct((batch, RW_HEADS, RW_HEAD, RW_HEAD), F32)],
        scratch_shapes=[pltpu.VMEM((1, 3 * RW_C), F32), pltpu.VMEM((1, 2 * RW_LORA), F32),
                        pltpu.VMEM((N_GROUPS, MXU_DIM, MXU_DIM), F32)],
        compiler_params=_params("arbitrary", "arbitrary"),
        name="rwkv_chunk",
    )(rkv, wa, gb, *consts)


def _rwkv_prep_kernel(rkv_ref, rkvp_ref, wa_ref, wap_ref, mu_rkv_ref, mu_wa_ref, w0_ref, w2_ref,
                      a0_ref, a2_ref, kk_ref, ka_ref, ones_ref,
                      r_ref, w_ref, k_ref, v_ref, kko_ref, b_ref):
    r, lw, k, v, kk, a = _rwkv_prep(rkv_ref[...], rkvp_ref[...], wa_ref[...], wap_ref[...],
                                    mu_rkv_ref[...], mu_wa_ref[...], w0_ref[...], w2_ref[...],
                                    a0_ref[...], a2_ref[...], kk_ref[...], ka_ref[...], ones_ref[...])
    r_ref[...] = r
    w_ref[...] = jnp.exp(lw)
    k_ref[...] = k
    v_ref[...] = v
    kko_ref[...] = kk
    b_ref[...] = a * kk


def _rwkv_step_kernel(r_ref, w_ref, k_ref, v_ref, kk_ref, b_ref, s_ref, o_ref, so_ref):
    s = s_ref[...]
    vi = lax.broadcasted_iota(jnp.int32, (RW_HEAD, RW_HEAD), 0)
    ki = lax.broadcasted_iota(jnp.int32, (RW_HEAD, RW_HEAD), 1)
    eye = (vi == ki).astype(F32)
    v_col = jnp.sum(v_ref[...] * eye, axis=-1, keepdims=True)
    sk = jnp.sum(s * kk_ref[...], axis=-1, keepdims=True)
    s_new = s * w_ref[...] - sk * b_ref[...] + v_col * k_ref[...]
    so_ref[...] = s_new
    o_col = jnp.sum(s_new * r_ref[...], axis=-1, keepdims=True)
    o_ref[...] = jnp.sum(o_col * eye, axis=-2, keepdims=True)


def _rwkv_post_kernel(o_ref, r_ref, k_ref, v_ref, gb_ref, rk_ref, lnw_ref, lnb_ref, ones_ref, y_ref):
    y = _rwkv_post(o_ref[...], r_ref[...], k_ref[...], v_ref[...], gb_ref[...], rk_ref[...],
                   lnw_ref[...], lnb_ref[...], ones_ref[...])
    y_ref[...] = y.astype(y_ref.dtype)


def _rwkv_sample(rkv, rkv_prev, wa, wa_prev, gb, cs, state):
    nb = rkv.shape[0]
    full = lambda a: pl.BlockSpec(a.shape, lambda i: (0,) * a.ndim)
    prep_in = [rkv, rkv_prev, wa, wa_prev] + [cs[n] for n in (
        "mu_rkv", "mu_wa", "w0", "w2p", "a0", "a2p", "k_k", "k_a", "ones_bd")]
    vec_shape = jax.ShapeDtypeStruct((nb, RW_C), F32)
    r, w, k, v, kk, b = pl.pallas_call(
        _rwkv_prep_kernel,
        grid=(1,),
        in_specs=[full(a) for a in prep_in],
        out_specs=[pl.BlockSpec((nb, RW_C), lambda i: (0, 0))] * 6,
        out_shape=[vec_shape] * 6,
        compiler_params=_params("arbitrary"),
        name="rwkv_prep",
    )(*prep_in)
    heads = lambda t: t.reshape(nb, RW_HEADS, 1, RW_HEAD)
    vec = pl.BlockSpec((None, RW_HEADS, 1, RW_HEAD), lambda i: (i, 0, 0, 0))
    st = pl.BlockSpec((None, RW_HEADS, RW_HEAD, RW_HEAD), lambda i: (i, 0, 0, 0))
    o, s_new = pl.pallas_call(
        _rwkv_step_kernel,
        grid=(nb,),
        in_specs=[vec] * 6 + [st],
        out_specs=[vec, st],
        out_shape=[jax.ShapeDtypeStruct((nb, RW_HEADS, 1, RW_HEAD), F32),
                   jax.ShapeDtypeStruct(state.shape, F32)],
        compiler_params=_params("arbitrary"),
        name="rwkv_step",
    )(heads(r), heads(w), heads(k), heads(v), heads(kk), heads(b), state)
    post_in = [o.reshape(nb, RW_C), r, k, v, gb, cs["r_k"], cs["ln_w"], cs["ln_b"], cs["ones_bd"]]
    y = pl.pallas_call(
        _rwkv_post_kernel,
        grid=(1,),
        in_specs=[full(a) for a in post_in],
        out_specs=pl.BlockSpec((nb, RW_C), lambda i: (0, 0)),
        out_shape=jax.ShapeDtypeStruct((nb, RW_C), BF16),
        compiler_params=_params("arbitrary"),
        name="rwkv_post",
    )(*post_in)
    return y, s_new


def _tail_kernel(ya_ref, yb_ref, m_ref, x_ref, p_ref, wda_ref, wdb_ref, wout_ref, wple_ref, wgate_ref,
                 pg_ref, fg_ref, y_ref):
    m = m_ref[...]
    merged = (_sigmoid(m[:, :D_MODEL]) * jnp.dot(ya_ref[...], wda_ref[...], preferred_element_type=F32)
              + _sigmoid(m[:, D_MODEL:]) * jnp.dot(yb_ref[...], wdb_ref[...], preferred_element_type=F32))
    x = x_ref[...] + _dot(merged, wout_ref[...])
    gate = _sigmoid(_dot(_rms(x, pg_ref[...]), wgate_ref[...]))
    x = x + _dot(p_ref[...], wple_ref[...]) * gate
    y_ref[...] = _rms(x, fg_ref[...])


def _tail(ya, yb, m, x, p, w, tm):
    rows = x.shape[0]
    tile = lambda n: pl.BlockSpec((tm, n), lambda i: (i, 0))
    consts = [w["wda"], w["wdb"], w["wout"], w["wple"], w["wgate"], w["ple_g"], w["final_g"]]
    return pl.pallas_call(
        _tail_kernel,
        grid=(rows // tm,),
        in_specs=[tile(RET_V), tile(RW_C), tile(2 * D_MODEL), tile(D_MODEL), tile(PLE_DIM)]
        + [_const_spec(a, 1) for a in consts],
        out_specs=tile(D_MODEL),
        out_shape=jax.ShapeDtypeStruct((rows, D_MODEL), F32),
        compiler_params=_params("arbitrary"),
        name="tail",
    )(ya, yb, m, x, p, *consts)


def _layer_weights(p):
    w_in = p["w_in"].astype(BF16)
    o2 = 2 * RET_QK
    o3 = o2 + RET_V
    o4 = o3 + RET_V
    o5 = o4 + 3 * RW_C + 2 * RW_LORA
    o6 = o5 + RW_C
    return dict(
        w_qk=w_in[:, :o2], w_v=w_in[:, o2:o3], w_ga=w_in[:, o3:o4],
        w_rkv=w_in[:, o4:o4 + 3 * RW_C], w_wa=w_in[:, o4 + 3 * RW_C:o5],
        w_gb=w_in[:, o5:o6], w_m=w_in[:, o6:],
        wda=p["w_down_a"].astype(BF16), wdb=p["w_down_b"].astype(BF16), wout=p["w_out"].astype(BF16),
        wple=p["w_ple"].astype(BF16), wgate=p["w_ple_gate"].astype(BF16),
        ple_g=p["ple_norm_g"].reshape(1, -1), final_g=p["final_norm_g"].reshape(1, -1),
    )


def _prompt_path(x, pe, p, w, cs):
    batch, seq, d = x.shape
    rows = batch * seq
    x2 = x.reshape(rows, d)
    h = _rmsnorm(x2, p["norm_g"], BF16, 512)
    shift = _rmsnorm(x[:, -1, :], p["norm_g"], F32, batch)
    proj = lambda name: _matmul(h, w[name], 512, name="in_proj_" + name)
    y_a, s_ret = _retention_prompt(proj("w_qk"), proj("w_v"), proj("w_ga"), batch, seq)
    y_b, s_rw = _rwkv_prompt(proj("w_rkv"), proj("w_wa"), proj("w_gb"), cs, batch, seq)
    y = _tail(y_a, y_b, proj("w_m"), x2, pe.reshape(rows, PLE_DIM), w, 256)
    return y.reshape(batch, seq, d), shift, s_ret, s_rw


def _sample_path(x, h_prev, s_ret, s_rw, pe, p, w, cs):
    nb = x.shape[0]
    x2 = x.reshape(nb, D_MODEL)
    h = _rmsnorm(x2, p["norm_g"], F32, nb)
    hb = h.astype(BF16)
    hcat = jnp.concatenate([hb, h_prev.astype(BF16)], axis=0)
    proj = lambda name: _matmul(hb, w[name], nb, name="s_proj_" + name)
    y_a, s_ret_new = _retention_sample(proj("w_qk"), proj("w_v"), proj("w_ga"), s_ret)
    rkv2 = _matmul(hcat, w["w_rkv"], 2 * nb, name="s_proj_rkv")
    wa2 = _matmul(hcat, w["w_wa"], 2 * nb, name="s_proj_wa")
    y_b, s_rw_new = _rwkv_sample(rkv2[:nb], rkv2[nb:], wa2[:nb], wa2[nb:], proj("w_gb"), cs, s_rw)
    y = _tail(y_a, y_b, proj("w_m"), x2, pe.reshape(nb, PLE_DIM), w, nb)
    return y.reshape(nb, 1, D_MODEL), h, s_ret_new, s_rw_new


def kernel(x_prompt, x_sample, state_ret, state_rwkv, state_shift, p_prompt, p_sample, norm_g, w_in, rw_mu, rw_w0, rw_w2, rw_a0, rw_a2, rw_k_k, rw_k_a, rw_r_k, rw_ln_w, rw_ln_b, w_down_a, w_down_b, w_out, w_ple, ple_norm_g, w_ple_gate, final_norm_g):
    assert norm_g.shape[0] == 1, "single-layer step"
    p = dict(norm_g=norm_g[0], w_in=w_in[0], rw_mu=rw_mu[0], rw_w0=rw_w0[0], rw_w2=rw_w2[0],
             rw_a0=rw_a0[0], rw_a2=rw_a2[0], rw_k_k=rw_k_k[0], rw_k_a=rw_k_a[0], rw_r_k=rw_r_k[0],
             rw_ln_w=rw_ln_w[0], rw_ln_b=rw_ln_b[0], w_down_a=w_down_a[0], w_down_b=w_down_b[0],
             w_out=w_out[0], w_ple=w_ple[0], ple_norm_g=ple_norm_g[0], w_ple_gate=w_ple_gate[0],
             final_norm_g=final_norm_g)
    w = _layer_weights(p)
    cs = _rwkv_consts(p)
    y_p, sh_p, ret_p, rw_p = _prompt_path(x_prompt, p_prompt[0], p, w, cs)
    y_s, sh_s, ret_s, rw_s = _sample_path(x_sample, state_shift[0], state_ret[0], state_rwkv[0],
                                          p_sample[0], p, w, cs)
    return (y_p, y_s, ret_p[None], rw_p[None], sh_p[None], ret_s[None], rw_s[None], sh_s[None])
```

```python
import functools
import math

import jax
import jax.numpy as jnp
from jax import lax
from jax.experimental import pallas as pl
from jax.experimental.pallas import tpu as pltpu

F32 = jnp.float32
BF16 = jnp.bfloat16

D_MODEL = 1024
RET_HEADS = 4
RET_DK = 256
RET_DV = 512
RET_QK = RET_HEADS * RET_DK
RET_V = RET_HEADS * RET_DV
RET_CHUNK = 128
ROPE_BASE = 10000.0
RW_HEAD = 64
RW_HEADS = D_MODEL // RW_HEAD
RW_C = RW_HEADS * RW_HEAD
RW_LORA = 64
RW_GN_EPS = 1e-5 * RW_HEAD
RW_CHUNK = 64
PLE_DIM = 256
NORM_EPS = 1e-6
PAST_LEN = 16384

LANES = 128
MXU_DIM = 256
HEADS_PER_GROUP = MXU_DIM // RW_HEAD
N_GROUPS = RW_C // MXU_DIM
HEAD_SHIFT = RW_HEAD.bit_length() - 1
VMEM_LIMIT_BYTES = 56 * 1024 * 1024
PROMPT_PROJ_ROWS = 256
PROMPT_TAIL_ROWS = 256
RET_STEP_SEQS = 4
RW_STEP_SEQS = 8
RW_SEQS_PER_STEP = 4


def _params(*sem):
    return pltpu.CompilerParams(dimension_semantics=sem, vmem_limit_bytes=VMEM_LIMIT_BYTES)


def _dot(a, b):
    return jnp.dot(a.astype(BF16), b.astype(BF16), preferred_element_type=F32)


def _dot_nt(a, b):
    return lax.dot_general(a.astype(BF16), b.astype(BF16), (((1,), (1,)), ((), ())),
                           preferred_element_type=F32)


def _dot_tn(a, b):
    return lax.dot_general(a.astype(BF16), b.astype(BF16), (((0,), (0,)), ((), ())),
                           preferred_element_type=F32)


def _sigmoid(x):
    return 1.0 / (1.0 + jnp.exp(-x))


def _silu(x):
    return x * _sigmoid(x)


def _rms(x, g):
    return x * lax.rsqrt(jnp.mean(x * x, axis=-1, keepdims=True) + NORM_EPS) * g


def _rmsnorm_kernel(x_ref, g_ref, o_ref):
    o_ref[...] = _rms(x_ref[...], g_ref[...]).astype(o_ref.dtype)


def _rmsnorm(x, g, out_dtype, tm):
    m, d = x.shape
    return pl.pallas_call(
        _rmsnorm_kernel,
        grid=(m // tm,),
        in_specs=[pl.BlockSpec((tm, d), lambda i: (i, 0)), pl.BlockSpec((1, d), lambda i: (0, 0))],
        out_specs=pl.BlockSpec((tm, d), lambda i: (i, 0)),
        out_shape=jax.ShapeDtypeStruct((m, d), out_dtype),
        compiler_params=_params("arbitrary"),
        name="rmsnorm",
    )(x, g.reshape(1, d))


PROJ_WIDTHS = (2 * RET_QK, RET_V, RET_V, 3 * RW_C + 2 * RW_LORA, RW_C, 2 * D_MODEL)


def _in_proj_kernel(x_ref, g_ref, w_ref, *out_refs, normalize):
    x = x_ref[...]
    h = (_rms(x, g_ref[...]) if normalize else x).astype(BF16)
    off = 0
    for o_ref, n in zip(out_refs, PROJ_WIDTHS):
        o_ref[...] = jnp.dot(h, w_ref[:, off:off + n], preferred_element_type=F32).astype(o_ref.dtype)
        off += n


def _in_proj(x, g, w_in, normalize, out_dtype, tm):
    m, d = x.shape
    n_all = w_in.shape[1]
    return pl.pallas_call(
        functools.partial(_in_proj_kernel, normalize=normalize),
        grid=(m // tm,),
        in_specs=[pl.BlockSpec((tm, d), lambda i: (i, 0)), pl.BlockSpec((1, d), lambda i: (0, 0)),
                  pl.BlockSpec((d, n_all), lambda i: (0, 0), pipeline_mode=pl.Buffered(1))],
        out_specs=[pl.BlockSpec((tm, n), lambda i: (i, 0)) for n in PROJ_WIDTHS],
        out_shape=[jax.ShapeDtypeStruct((m, n), out_dtype) for n in PROJ_WIDTHS],
        compiler_params=_params("arbitrary"),
        name="in_proj",
    )(x, g.reshape(1, d), w_in)


def _rope(x, cos, sin):
    half = x.shape[-1] // 2
    x1, x2 = x[:, :half], x[:, half:]
    return jnp.concatenate([x1 * cos - x2 * sin, x2 * cos + x1 * sin], axis=-1)


def _rope_tables(pos):
    half = RET_DK // 2
    inv = ROPE_BASE ** (-jnp.arange(half, dtype=F32) / half)
    ang = pos[:, None] * inv[None, :]
    return jnp.cos(ang), jnp.sin(ang)


def _ret_chunk_kernel(dec_ref, q_ref, k_ref, v_ref, ga_ref, cos_ref, sin_ref, y_ref, s_ref):
    c = pl.program_id(1)
    C = RET_CHUNK

    @pl.when(c == 0)
    def _():
        s_ref[...] = jnp.zeros_like(s_ref)

    ti = lax.broadcasted_iota(jnp.int32, (C, C), 0)
    tj = lax.broadcasted_iota(jnp.int32, (C, C), 1)
    rel = (ti - tj).astype(F32)
    idx = lax.broadcasted_iota(jnp.int32, (C, 1), 0).astype(F32)
    cos = cos_ref[...]
    sin = sin_ref[...]
    for h in range(RET_HEADS):
        lg = dec_ref[2 * h]
        chunk_dec = dec_ref[2 * h + 1]
        dmask = jnp.where(rel >= 0, jnp.exp(jnp.maximum(rel, 0.0) * lg), 0.0)
        q_dec = jnp.exp((idx + 1.0) * lg)
        k_dec = jnp.exp((C - 1.0 - idx) * lg)
        q = _rope(q_ref[:, h * RET_DK:(h + 1) * RET_DK].astype(F32), cos, sin)
        k = _rope(k_ref[:, h * RET_DK:(h + 1) * RET_DK].astype(F32), cos, sin) * (RET_DK ** -0.5)
        v = v_ref[:, h * RET_DV:(h + 1) * RET_DV]
        s = s_ref[h]
        inner = _dot_nt(q, k) * dmask
        o = _dot(inner, v) + _dot(q * q_dec, s)
        s_ref[h] = chunk_dec * s + _dot_tn(k * k_dec, v)
        o = o * lax.rsqrt(jnp.mean(o * o, axis=-1, keepdims=True) + NORM_EPS)
        y_ref[:, h * RET_DV:(h + 1) * RET_DV] = (
            o * _silu(ga_ref[:, h * RET_DV:(h + 1) * RET_DV].astype(F32))).astype(y_ref.dtype)


def _ret_decay_table():
    log_g = jnp.log(1.0 - jnp.exp2(-5.0 - jnp.arange(RET_HEADS, dtype=F32)))
    return log_g


def _retention_prompt(qk, v, ga, batch, seq):
    C = RET_CHUNK
    nc = seq // C
    log_g = _ret_decay_table()
    dec = jnp.stack([log_g, jnp.exp(C * log_g)], axis=1).reshape(-1)
    half = RET_DK // 2
    cos, sin = _rope_tables(jnp.arange(seq, dtype=F32))
    row = lambda b, c: (b * nc + c, 0)
    return pl.pallas_call(
        _ret_chunk_kernel,
        grid=(batch, nc),
        in_specs=[
            pl.BlockSpec(memory_space=pltpu.SMEM),
            pl.BlockSpec((C, RET_QK), lambda b, c: (b * nc + c, 0)),
            pl.BlockSpec((C, RET_QK), lambda b, c: (b * nc + c, 1)),
            pl.BlockSpec((C, RET_V), row),
            pl.BlockSpec((C, RET_V), row),
            pl.BlockSpec((C, half), lambda b, c: (c, 0)),
            pl.BlockSpec((C, half), lambda b, c: (c, 0)),
        ],
        out_specs=[
            pl.BlockSpec((C, RET_V), row),
            pl.BlockSpec((None, RET_HEADS, RET_DK, RET_DV), lambda b, c: (b, 0, 0, 0)),
        ],
        out_shape=[
            jax.ShapeDtypeStruct((batch * seq, RET_V), BF16),
            jax.ShapeDtypeStruct((batch, RET_HEADS, RET_DK, RET_DV), F32),
        ],
        compiler_params=_params("arbitrary", "arbitrary"),
        name="retention_chunk",
    )(dec, qk, qk, v, ga, cos, sin)


def _ret_step_kernel(dec_ref, q_ref, k_ref, v_ref, ga_ref, cos_ref, sin_ref, s_ref, y_ref, so_ref):
    row0 = lax.broadcasted_iota(jnp.int32, (8, 1), 0)
    cos = cos_ref[...]
    sin = sin_ref[...]
    for b, h in [(b, h) for b in range(q_ref.shape[0]) for h in range(RET_HEADS)]:
        g = dec_ref[h]
        q = _rope(q_ref[b, :, h * RET_DK:(h + 1) * RET_DK], cos, sin)
        k = _rope(k_ref[b, :, h * RET_DK:(h + 1) * RET_DK], cos, sin) * (RET_DK ** -0.5)
        v = v_ref[b, :, h * RET_DV:(h + 1) * RET_DV]
        s = s_ref[b, h]
        qk = jnp.sum(q * k, axis=-1, keepdims=True)
        q8 = jnp.broadcast_to(q, (8, RET_DK))
        o = qk * v + g * _dot(q8, s)[0:1, :]
        k_hi = k.astype(BF16).astype(F32)
        k_lo = k - k_hi
        v_hi = v.astype(BF16).astype(F32)
        v_lo = v - v_hi
        k8 = jnp.where(row0 < 2, k_hi, jnp.where(row0 == 2, k_lo, 0.0))
        v8 = jnp.where((row0 == 0) | (row0 == 2), v_hi, jnp.where(row0 == 1, v_lo, 0.0))
        so_ref[b, h] = g * s + _dot_tn(k8, v8)
        o = o * lax.rsqrt(jnp.mean(o * o, axis=-1, keepdims=True) + NORM_EPS)
        y_ref[b, :, h * RET_DV:(h + 1) * RET_DV] = (
            o * _silu(ga_ref[b, :, h * RET_DV:(h + 1) * RET_DV])).astype(y_ref.dtype)


def _retention_sample(qk, v, ga, state):
    nb = qk.shape[0]
    g = jnp.exp(_ret_decay_table())
    cos, sin = _rope_tables(PAST_LEN + jnp.arange(1, dtype=F32))
    q, k = qk[:, :RET_QK], qk[:, RET_QK:]
    tab = pl.BlockSpec((1, RET_DK // 2), lambda b: (0, 0))
    nq = RET_STEP_SEQS
    r3 = lambda t: t.reshape(nb, 1, t.shape[-1])
    vec = lambda n: pl.BlockSpec((nq, 1, n), lambda b: (b, 0, 0))
    st = pl.BlockSpec((nq, RET_HEADS, RET_DK, RET_DV), lambda b: (b, 0, 0, 0))
    y, s_new = pl.pallas_call(
        _ret_step_kernel,
        grid=(nb // nq,),
        in_specs=[pl.BlockSpec(memory_space=pltpu.SMEM), vec(RET_QK), vec(RET_QK), vec(RET_V), vec(RET_V),
                  tab, tab, st],
        out_specs=[vec(RET_V), st],
        out_shape=[jax.ShapeDtypeStruct((nb, 1, RET_V), BF16),
                   jax.ShapeDtypeStruct(state.shape, F32)],
        compiler_params=_params("arbitrary"),
        name="retention_step",
    )(g, r3(q), r3(k), r3(v), r3(ga), cos, sin, state)
    return y.reshape(nb, RET_V), s_new


def _head_sums(xs, ones_bd):
    rows = xs[0].shape[0]
    stack = jnp.concatenate(
        [x[:, g * MXU_DIM:(g + 1) * MXU_DIM] for x in xs for g in range(N_GROUPS)], axis=0)
    s = jnp.dot(stack.astype(BF16), ones_bd, preferred_element_type=F32)
    return [jnp.concatenate([s[(i * N_GROUPS + g) * rows:(i * N_GROUPS + g + 1) * rows]
                             for g in range(N_GROUPS)], axis=-1) for i in range(len(xs))]


def _rwkv_prep(rkv, rkv_prev, wa, wa_prev, mu_rkv, mu_wa, w0, w2p, a0, a2p, k_k, k_a, ones_bd):
    z = rkv + (rkv_prev - rkv) * mu_rkv
    zwa = wa + (wa_prev - wa) * mu_wa
    r = z[:, :RW_C]
    k = z[:, RW_C:2 * RW_C]
    v = z[:, 2 * RW_C:]
    wpre = w0 + _dot(jnp.tanh(zwa), w2p)
    log_decay = -math.exp(-0.5) * _sigmoid(wpre)
    a = _sigmoid(a0 + _dot(zwa, a2p))
    kk = k * k_k
    nrm = jnp.maximum(jnp.sqrt(_head_sums([kk * kk], ones_bd)[0]), 1e-12)
    kk = kk / nrm
    k = k * (1.0 + (a - 1.0) * k_a)
    return r, log_decay, k, v, kk, a


def _rwkv_post(o, r, k, v, gb, r_k, ln_w, ln_b, ones_bd):
    o_sum, rk_sum = _head_sums([o, r * k * r_k], ones_bd)
    d = o - o_sum * (1.0 / RW_HEAD)
    var = _head_sums([d * d], ones_bd)[0] * (1.0 / RW_HEAD)
    on = d * lax.rsqrt(var + RW_GN_EPS) * ln_w + ln_b
    return (on + rk_sum * v) * _silu(gb)


def _block_diag(x, head_of_lane):
    xb = x.astype(BF16)
    zero = jnp.zeros_like(xb)
    return jnp.concatenate(
        [jnp.where(head_of_lane == j, xb, zero) for j in range(HEADS_PER_GROUP)], axis=0)


def _rwkv_chunk_kernel(sh_ref, gb_ref, mu_rkv_ref, mu_wa_ref, w0_ref, w2_ref, a0_ref, a2_ref,
                       kk_ref, ka_ref, rk_ref, lnw_ref, lnb_ref, ones_ref,
                       y_ref, so_ref, carry, state):
    c = pl.program_id(1)
    nc = pl.num_programs(1)
    C = RW_CHUNK
    n_seq = sh_ref.shape[0]

    @pl.when(c == 0)
    def _():
        carry[...] = jnp.zeros_like(carry)
        state[...] = jnp.zeros_like(state)

    ones_bd = ones_ref[...]
    row = lax.broadcasted_iota(jnp.int32, (C, 1), 0)
    ti = lax.broadcasted_iota(jnp.int32, (C, C), 0)
    tj = lax.broadcasted_iota(jnp.int32, (C, C), 1)
    tri = (ti >= tj).astype(BF16)

    tok = []
    for q in range(n_seq):
        sh = sh_ref[q].astype(F32)
        sh_prev = jnp.where(row == 0, carry[q], pltpu.roll(sh, 1, 0))
        carry[q] = sh[C - 1:C, :]
        rkv, wa = sh[:, :3 * RW_C], sh[:, 3 * RW_C:]
        rkv_prev, wa_prev = sh_prev[:, :3 * RW_C], sh_prev[:, 3 * RW_C:]
        r, lw, k, v, kk, a = _rwkv_prep(rkv, rkv_prev, wa, wa_prev, mu_rkv_ref[...], mu_wa_ref[...],
                                        w0_ref[...], w2_ref[...], a0_ref[...], a2_ref[...],
                                        kk_ref[...], ka_ref[...], ones_bd)
        lw_hi = lw.astype(BF16)
        lw_lo = (lw - lw_hi.astype(F32)).astype(BF16)
        cl = (jnp.dot(tri, lw_hi, preferred_element_type=F32)
              + jnp.dot(tri, lw_lo, preferred_element_type=F32))
        cl_last = cl[C - 1:C, :]
        e_neg = jnp.exp(-cl)
        e_rem = jnp.exp(cl_last - cl)
        beta = a * kk
        tok.append(dict(
            r=r, k=k, v=v,
            ag=-kk * jnp.exp(cl - lw), rg=r * jnp.exp(cl), bg=beta * e_neg, kg=k * e_neg,
            bg_c=beta * e_rem, kg_c=k * e_rem,
            g_chunk=jnp.exp(cl_last)))

    lane = lax.broadcasted_iota(jnp.int32, (1, MXU_DIM), 1)
    head_of_lane = lane >> HEAD_SHIFT
    t_col = lax.broadcasted_iota(jnp.int32, (C, MXU_DIM), 0)
    i_lane = lax.broadcasted_iota(jnp.int32, (C, MXU_DIM), 1) & (RW_HEAD - 1)
    strict = t_col > i_lane
    incl = t_col >= i_lane
    eye = (t_col == i_lane).astype(F32)
    vrow_head = lax.broadcasted_iota(jnp.int32, (MXU_DIM, MXU_DIM), 0) >> HEAD_SHIFT
    klane_head = lax.broadcasted_iota(jnp.int32, (MXU_DIM, MXU_DIM), 1) >> HEAD_SHIFT
    same_head = vrow_head == klane_head

    units = [(q, g) for q in range(n_seq) for g in range(N_GROUPS)]
    U = range(len(units))
    bd = lambda x: _block_diag(x, head_of_lane)
    grp = lambda name, u: tok[units[u][0]][name][:, units[u][1] * MXU_DIM:(units[u][1] + 1) * MXU_DIM]
    s_bd = [state[q, g] for q, g in units]
    lhs = [jnp.concatenate([grp("ag", u), grp("rg", u)], axis=0) for u in U]
    ab = [_dot_nt(lhs[u], bd(grp("bg", u))) for u in U]
    ak = [_dot_nt(lhs[u], bd(grp("kg", u))) for u in U]
    n_pow = [jnp.where(strict, ab[u][:C], 0.0) for u in U]
    a_ak = [jnp.where(strict, ak[u][:C], 0.0) for u in U]
    a_rb = [jnp.where(incl, ab[u][C:], 0.0) for u in U]
    a_rk = [jnp.where(incl, ak[u][C:], 0.0) for u in U]
    v_bd = [bd(grp("v", u)) for u in U]
    sv = [_dot_nt(lhs[u], s_bd[u]) for u in U]
    av = [_dot(jnp.concatenate([a_ak[u], a_rk[u]], axis=0), v_bd[u]) for u in U]
    t_inv = [eye + n_pow[u] for u in U]
    n_pow = [_dot(n_pow[u], bd(n_pow[u])) for u in U]
    for _ in range(int(math.log2(C)) - 2):
        prod = [_dot(jnp.concatenate([n_pow[u], t_inv[u]], axis=0), bd(n_pow[u])) for u in U]
        n_pow = [prod[u][:C] for u in U]
        t_inv = [t_inv[u] + prod[u][C:] for u in U]
    t_inv = [t_inv[u] + _dot(t_inv[u], bd(n_pow[u])) for u in U]
    p = [_dot(t_inv[u], bd(sv[u][:C] + av[u][:C])) for u in U]
    o = [sv[u][C:] + _dot(a_rb[u], bd(p[u])) + av[u][C:] for u in U]
    for u, (q, g) in enumerate(units):
        upd = _dot_tn(jnp.concatenate([p[u], grp("v", u)], axis=0),
                      jnp.concatenate([grp("bg_c", u), grp("kg_c", u)], axis=0))
        state[q, g] = s_bd[u] * grp("g_chunk", u) + jnp.where(same_head, upd, 0.0)
    for q in range(n_seq):
        o_q = jnp.concatenate(o[q * N_GROUPS:(q + 1) * N_GROUPS], axis=-1)
        t = tok[q]
        y = _rwkv_post(o_q, t["r"], t["k"], t["v"], gb_ref[q].astype(F32), rk_ref[...], lnw_ref[...],
                       lnb_ref[...], ones_bd)
        y_ref[q] = y.astype(y_ref.dtype)

    @pl.when(c == nc - 1)
    def _():
        for q in range(n_seq):
            for g in range(N_GROUPS):
                for j in range(HEADS_PER_GROUP):
                    blk = slice(j * RW_HEAD, (j + 1) * RW_HEAD)
                    so_ref[q, g * HEADS_PER_GROUP + j] = state[q, g, blk, blk]


def _rwkv_consts(p):
    row = lambda t: t.reshape(1, -1).astype(F32)
    zeros = jnp.zeros((RW_LORA, RW_C), F32)
    w2p = jnp.concatenate([p["rw_w2"], zeros], axis=0).astype(BF16)
    a2p = jnp.concatenate([zeros, p["rw_a2"]], axis=0).astype(BF16)
    hl = jnp.arange(MXU_DIM) // RW_HEAD
    ones_bd = (hl[:, None] == hl[None, :]).astype(BF16)
    mu = p["rw_mu"]
    return dict(mu_rkv=row(mu[:3 * RW_C]), mu_wa=row(mu[3 * RW_C:]), w0=row(p["rw_w0"]), w2p=w2p,
                a0=row(p["rw_a0"]), a2p=a2p, k_k=row(p["rw_k_k"]), k_a=row(p["rw_k_a"]),
                r_k=row(p["rw_r_k"]), ln_w=row(p["rw_ln_w"]), ln_b=row(p["rw_ln_b"]), ones_bd=ones_bd)


def _const_spec(arr, ngrid):
    zeros = (0,) * arr.ndim
    if ngrid == 1:
        return pl.BlockSpec(arr.shape, lambda i: zeros)
    return pl.BlockSpec(arr.shape, lambda i, j: zeros)


def _rwkv_prompt(sh, gb, cs, batch, seq):
    C = RW_CHUNK
    nq = RW_SEQS_PER_STEP
    n_sh = 3 * RW_C + 2 * RW_LORA
    blk = lambda n: pl.BlockSpec((nq, C, n), lambda b, c: (b, c, 0))
    consts = [cs[n] for n in ("mu_rkv", "mu_wa", "w0", "w2p", "a0", "a2p", "k_k", "k_a", "r_k",
                              "ln_w", "ln_b", "ones_bd")]
    y, s = pl.pallas_call(
        _rwkv_chunk_kernel,
        grid=(batch // nq, seq // C),
        in_specs=[blk(n_sh), blk(RW_C)] + [_const_spec(a, 2) for a in consts],
        out_specs=[blk(RW_C),
                   pl.BlockSpec((nq, RW_HEADS, RW_HEAD, RW_HEAD), lambda b, c: (b, 0, 0, 0))],
        out_shape=[jax.ShapeDtypeStruct((batch, seq, RW_C), BF16),
                   jax.ShapeDtypeStruct((batch, RW_HEADS, RW_HEAD, RW_HEAD), F32)],
        scratch_shapes=[pltpu.VMEM((nq, 1, n_sh), F32),
                        pltpu.VMEM((nq, N_GROUPS, MXU_DIM, MXU_DIM), F32)],
        compiler_params=_params("arbitrary", "arbitrary"),
        name="rwkv_chunk",
    )(sh.reshape(batch, seq, n_sh), gb.reshape(batch, seq, RW_C), *consts)
    return y.reshape(batch * seq, RW_C), s


def _rwkv_prep_kernel(rkv_ref, rkvp_ref, wa_ref, wap_ref, mu_rkv_ref, mu_wa_ref, w0_ref, w2_ref,
                      a0_ref, a2_ref, kk_ref, ka_ref, ones_ref,
                      r_ref, w_ref, k_ref, v_ref, kko_ref, b_ref):
    r, lw, k, v, kk, a = _rwkv_prep(rkv_ref[...], rkvp_ref[...], wa_ref[...], wap_ref[...],
                                    mu_rkv_ref[...], mu_wa_ref[...], w0_ref[...], w2_ref[...],
                                    a0_ref[...], a2_ref[...], kk_ref[...], ka_ref[...], ones_ref[...])
    r_ref[...] = r
    w_ref[...] = jnp.exp(lw)
    k_ref[...] = k
    v_ref[...] = v
    kko_ref[...] = kk
    b_ref[...] = a * kk


def _rwkv_step_kernel(r_ref, w_ref, k_ref, v_ref, kk_ref, b_ref, s_ref, o_ref, so_ref):
    s = s_ref[...]
    vi = lax.broadcasted_iota(jnp.int32, (RW_HEAD, RW_HEAD), 0)
    ki = lax.broadcasted_iota(jnp.int32, (RW_HEAD, RW_HEAD), 1)
    eye = (vi == ki).astype(F32)
    v_col = jnp.sum(v_ref[...] * eye, axis=-1, keepdims=True)
    sk = jnp.sum(s * kk_ref[...], axis=-1, keepdims=True)
    s_new = s * w_ref[...] - sk * b_ref[...] + v_col * k_ref[...]
    so_ref[...] = s_new
    o_col = jnp.sum(s_new * r_ref[...], axis=-1, keepdims=True)
    o_ref[...] = jnp.sum(o_col * eye, axis=-2, keepdims=True)


def _rwkv_post_kernel(o_ref, r_ref, k_ref, v_ref, gb_ref, rk_ref, lnw_ref, lnb_ref, ones_ref, y_ref):
    y = _rwkv_post(o_ref[...], r_ref[...], k_ref[...], v_ref[...], gb_ref[...], rk_ref[...],
                   lnw_ref[...], lnb_ref[...], ones_ref[...])
    y_ref[...] = y.astype(y_ref.dtype)


def _rwkv_sample(rkv, rkv_prev, wa, wa_prev, gb, cs, state):
    nb = rkv.shape[0]
    full = lambda a: pl.BlockSpec(a.shape, lambda i: (0,) * a.ndim)
    prep_in = [rkv, rkv_prev, wa, wa_prev] + [cs[n] for n in (
        "mu_rkv", "mu_wa", "w0", "w2p", "a0", "a2p", "k_k", "k_a", "ones_bd")]
    vec_shape = jax.ShapeDtypeStruct((nb, RW_C), F32)
    r, w, k, v, kk, b = pl.pallas_call(
        _rwkv_prep_kernel,
        grid=(1,),
        in_specs=[full(a) for a in prep_in],
        out_specs=[pl.BlockSpec((nb, RW_C), lambda i: (0, 0))] * 6,
        out_shape=[vec_shape] * 6,
        compiler_params=_params("arbitrary"),
        name="rwkv_prep",
    )(*prep_in)
    nq = RW_STEP_SEQS
    heads = lambda t: t.reshape(nb, RW_HEADS, 1, RW_HEAD)
    vec = pl.BlockSpec((nq, RW_HEADS, 1, RW_HEAD), lambda i: (i, 0, 0, 0))
    st = pl.BlockSpec((nq, RW_HEADS, RW_HEAD, RW_HEAD), lambda i: (i, 0, 0, 0))
    o, s_new = pl.pallas_call(
        _rwkv_step_kernel,
        grid=(nb // nq,),
        in_specs=[vec] * 6 + [st],
        out_specs=[vec, st],
        out_shape=[jax.ShapeDtypeStruct((nb, RW_HEADS, 1, RW_HEAD), F32),
                   jax.ShapeDtypeStruct(state.shape, F32)],
        compiler_params=_params("arbitrary"),
        name="rwkv_step",
    )(heads(r), heads(w), heads(k), heads(v), heads(kk), heads(b), state)
    post_in = [o.reshape(nb, RW_C), r, k, v, gb, cs["r_k"], cs["ln_w"], cs["ln_b"], cs["ones_bd"]]
    y = pl.pallas_call(
        _rwkv_post_kernel,
        grid=(1,),
        in_specs=[full(a) for a in post_in],
        out_specs=pl.BlockSpec((nb, RW_C), lambda i: (0, 0)),
        out_shape=jax.ShapeDtypeStruct((nb, RW_C), BF16),
        compiler_params=_params("arbitrary"),
        name="rwkv_post",
    )(*post_in)
    return y, s_new


def _tail_kernel(ya_ref, yb_ref, m_ref, x_ref, p_ref, wda_ref, wdb_ref, wout_ref, wple_ref, wgate_ref,
                 pg_ref, fg_ref, y_ref):
    m = m_ref[...].astype(F32)
    merged = (_sigmoid(m[:, :D_MODEL]) * jnp.dot(ya_ref[...], wda_ref[...], preferred_element_type=F32)
              + _sigmoid(m[:, D_MODEL:]) * jnp.dot(yb_ref[...], wdb_ref[...], preferred_element_type=F32))
    x = x_ref[...] + _dot(merged, wout_ref[...])
    gate = _sigmoid(_dot(_rms(x, pg_ref[...]), wgate_ref[...]))
    x = x + _dot(p_ref[...], wple_ref[...]) * gate
    y_ref[...] = _rms(x, fg_ref[...])


def _tail(ya, yb, m, x, p, w, tm):
    rows = x.shape[0]
    tile = lambda n: pl.BlockSpec((tm, n), lambda i: (i, 0))
    consts = [w["wda"], w["wdb"], w["wout"], w["wple"], w["wgate"], w["ple_g"], w["final_g"]]
    return pl.pallas_call(
        _tail_kernel,
        grid=(rows // tm,),
        in_specs=[tile(RET_V), tile(RW_C), tile(2 * D_MODEL), tile(D_MODEL), tile(PLE_DIM)]
        + [_const_spec(a, 1) for a in consts],
        out_specs=tile(D_MODEL),
        out_shape=jax.ShapeDtypeStruct((rows, D_MODEL), F32),
        compiler_params=_params("arbitrary"),
        name="tail",
    )(ya, yb, m, x, p, *consts)


def _layer_weights(p):
    return dict(
        w_in=p["w_in"].astype(BF16),
        wda=p["w_down_a"].astype(BF16), wdb=p["w_down_b"].astype(BF16), wout=p["w_out"].astype(BF16),
        wple=p["w_ple"].astype(BF16), wgate=p["w_ple_gate"].astype(BF16),
        ple_g=p["ple_norm_g"].reshape(1, -1), final_g=p["final_norm_g"].reshape(1, -1),
    )


def _prompt_path(x, pe, p, w, cs):
    batch, seq, d = x.shape
    rows = batch * seq
    x2 = x.reshape(rows, d)
    shift = _rmsnorm(x[:, -1, :], p["norm_g"], F32, batch)
    qk, v, ga, sh, gb, m = _in_proj(x2, p["norm_g"], w["w_in"], True, BF16, PROMPT_PROJ_ROWS)
    y_a, s_ret = _retention_prompt(qk, v, ga, batch, seq)
    y_b, s_rw = _rwkv_prompt(sh, gb, cs, batch, seq)
    y = _tail(y_a, y_b, m, x2, pe.reshape(rows, PLE_DIM), w, PROMPT_TAIL_ROWS)
    return y.reshape(batch, seq, d), shift, s_ret, s_rw


def _sample_path(x, h_prev, s_ret, s_rw, pe, p, w, cs):
    nb = x.shape[0]
    x2 = x.reshape(nb, D_MODEL)
    h = _rmsnorm(x2, p["norm_g"], F32, nb)
    hcat = jnp.concatenate([h, h_prev], axis=0)
    qk, v, ga, sh, gb, m = _in_proj(hcat, p["norm_g"], w["w_in"], False, F32, 2 * nb)
    y_a, s_ret_new = _retention_sample(qk[:nb], v[:nb], ga[:nb], s_ret)
    n_rkv = 3 * RW_C
    y_b, s_rw_new = _rwkv_sample(sh[:nb, :n_rkv], sh[nb:, :n_rkv], sh[:nb, n_rkv:], sh[nb:, n_rkv:],
                                 gb[:nb], cs, s_rw)
    y = _tail(y_a, y_b, m[:nb], x2, pe.reshape(nb, PLE_DIM), w, nb)
    return y.reshape(nb, 1, D_MODEL), h, s_ret_new, s_rw_new


def kernel(x_prompt, x_sample, state_ret, state_rwkv, state_shift, p_prompt, p_sample, norm_g, w_in, rw_mu, rw_w0, rw_w2, rw_a0, rw_a2, rw_k_k, rw_k_a, rw_r_k, rw_ln_w, rw_ln_b, w_down_a, w_down_b, w_out, w_ple, ple_norm_g, w_ple_gate, final_norm_g):
    assert norm_g.shape[0] == 1, "single-layer step"
    p = dict(norm_g=norm_g[0], w_in=w_in[0], rw_mu=rw_mu[0], rw_w0=rw_w0[0], rw_w2=rw_w2[0],
             rw_a0=rw_a0[0], rw_a2=rw_a2[0], rw_k_k=rw_k_k[0], rw_k_a=rw_k_a[0], rw_r_k=rw_r_k[0],
             rw_ln_w=rw_ln_w[0], rw_ln_b=rw_ln_b[0], w_down_a=w_down_a[0], w_down_b=w_down_b[0],
             w_out=w_out[0], w_ple=w_ple[0], ple_norm_g=ple_norm_g[0], w_ple_gate=w_ple_gate[0],
             final_norm_g=final_norm_g)
    w = _layer_weights(p)
    cs = _rwkv_consts(p)
    y_p, sh_p, ret_p, rw_p = _prompt_path(x_prompt, p_prompt[0], p, w, cs)
    y_s, sh_s, ret_s, rw_s = _sample_path(x_sample, state_shift[0], state_ret[0], state_rwkv[0],
                                          p_sample[0], p, w, cs)
    return (y_p, y_s, ret_p[None], rw_p[None], sh_p[None], ret_s[None], rw_s[None], sh_s[None])
```

```python
import functools
import math

import jax
import jax.numpy as jnp
from jax import lax
from jax.experimental import pallas as pl
from jax.experimental.pallas import tpu as pltpu

F32 = jnp.float32
BF16 = jnp.bfloat16

D_MODEL = 1024
RET_HEADS = 4
RET_DK = 256
RET_DV = 512
RET_QK = RET_HEADS * RET_DK
RET_V = RET_HEADS * RET_DV
RET_CHUNK = 128
ROPE_BASE = 10000.0
RW_HEAD = 64
RW_HEADS = D_MODEL // RW_HEAD
RW_C = RW_HEADS * RW_HEAD
RW_LORA = 64
RW_GN_EPS = 1e-5 * RW_HEAD
RW_CHUNK = 64
PLE_DIM = 256
NORM_EPS = 1e-6
PAST_LEN = 16384

LANES = 128
MXU_DIM = 256
HEADS_PER_GROUP = MXU_DIM // RW_HEAD
N_GROUPS = RW_C // MXU_DIM
HEAD_SHIFT = RW_HEAD.bit_length() - 1
VMEM_LIMIT_BYTES = 56 * 1024 * 1024
PROMPT_PROJ_ROWS = 256
PROMPT_TAIL_ROWS = 256
RW_SEQS_PER_STEP = 4


def _params(*sem):
    return pltpu.CompilerParams(dimension_semantics=sem, vmem_limit_bytes=VMEM_LIMIT_BYTES)


def _dot(a, b):
    return jnp.dot(a.astype(BF16), b.astype(BF16), preferred_element_type=F32)


def _dot_nt(a, b):
    return lax.dot_general(a.astype(BF16), b.astype(BF16), (((1,), (1,)), ((), ())),
                           preferred_element_type=F32)


def _dot_tn(a, b):
    return lax.dot_general(a.astype(BF16), b.astype(BF16), (((0,), (0,)), ((), ())),
                           preferred_element_type=F32)


def _sigmoid(x):
    return 1.0 / (1.0 + jnp.exp(-x))


def _silu(x):
    return x * _sigmoid(x)


def _rms(x, g):
    return x * lax.rsqrt(jnp.mean(x * x, axis=-1, keepdims=True) + NORM_EPS) * g


def _rmsnorm_kernel(x_ref, g_ref, o_ref):
    o_ref[...] = _rms(x_ref[...], g_ref[...]).astype(o_ref.dtype)


def _rmsnorm(x, g, out_dtype, tm):
    m, d = x.shape
    return pl.pallas_call(
        _rmsnorm_kernel,
        grid=(m // tm,),
        in_specs=[pl.BlockSpec((tm, d), lambda i: (i, 0)), pl.BlockSpec((1, d), lambda i: (0, 0))],
        out_specs=pl.BlockSpec((tm, d), lambda i: (i, 0)),
        out_shape=jax.ShapeDtypeStruct((m, d), out_dtype),
        compiler_params=_params("arbitrary"),
        name="rmsnorm",
    )(x, g.reshape(1, d))


PROJ_WIDTHS = (2 * RET_QK, RET_V, RET_V, 3 * RW_C + 2 * RW_LORA, RW_C, 2 * D_MODEL)


def _in_proj_kernel(x_ref, g_ref, w_ref, *out_refs, normalize):
    x = x_ref[...]
    h = (_rms(x, g_ref[...]) if normalize else x).astype(BF16)
    off = 0
    for o_ref, n in zip(out_refs, PROJ_WIDTHS):
        o_ref[...] = jnp.dot(h, w_ref[:, off:off + n], preferred_element_type=F32).astype(o_ref.dtype)
        off += n


def _in_proj(x, g, w_in, normalize, out_dtype, tm):
    m, d = x.shape
    n_all = w_in.shape[1]
    return pl.pallas_call(
        functools.partial(_in_proj_kernel, normalize=normalize),
        grid=(m // tm,),
        in_specs=[pl.BlockSpec((tm, d), lambda i: (i, 0)), pl.BlockSpec((1, d), lambda i: (0, 0)),
                  pl.BlockSpec((d, n_all), lambda i: (0, 0), pipeline_mode=pl.Buffered(1))],
        out_specs=[pl.BlockSpec((tm, n), lambda i: (i, 0)) for n in PROJ_WIDTHS],
        out_shape=[jax.ShapeDtypeStruct((m, n), out_dtype) for n in PROJ_WIDTHS],
        compiler_params=_params("arbitrary"),
        name="in_proj",
    )(x, g.reshape(1, d), w_in)


def _rope(x, cos, sin):
    half = x.shape[-1] // 2
    x1, x2 = x[:, :half], x[:, half:]
    return jnp.concatenate([x1 * cos - x2 * sin, x2 * cos + x1 * sin], axis=-1)


def _rope_tables(pos):
    half = RET_DK // 2
    inv = ROPE_BASE ** (-jnp.arange(half, dtype=F32) / half)
    ang = pos[:, None] * inv[None, :]
    return jnp.cos(ang), jnp.sin(ang)


def _ret_chunk_kernel(dec_ref, q_ref, k_ref, v_ref, ga_ref, cos_ref, sin_ref, y_ref, s_ref):
    c = pl.program_id(1)
    C = RET_CHUNK

    @pl.when(c == 0)
    def _():
        s_ref[...] = jnp.zeros_like(s_ref)

    ti = lax.broadcasted_iota(jnp.int32, (C, C), 0)
    tj = lax.broadcasted_iota(jnp.int32, (C, C), 1)
    rel = (ti - tj).astype(F32)
    idx = lax.broadcasted_iota(jnp.int32, (C, 1), 0).astype(F32)
    cos = cos_ref[...]
    sin = sin_ref[...]
    for h in range(RET_HEADS):
        lg = dec_ref[2 * h]
        chunk_dec = dec_ref[2 * h + 1]
        dmask = jnp.where(rel >= 0, jnp.exp(jnp.maximum(rel, 0.0) * lg), 0.0)
        q_dec = jnp.exp((idx + 1.0) * lg)
        k_dec = jnp.exp((C - 1.0 - idx) * lg)
        q = _rope(q_ref[:, h * RET_DK:(h + 1) * RET_DK].astype(F32), cos, sin)
        k = _rope(k_ref[:, h * RET_DK:(h + 1) * RET_DK].astype(F32), cos, sin) * (RET_DK ** -0.5)
        v = v_ref[:, h * RET_DV:(h + 1) * RET_DV]
        s = s_ref[h]
        inner = _dot_nt(q, k) * dmask
        o = _dot(inner, v) + _dot(q * q_dec, s)
        s_ref[h] = chunk_dec * s + _dot_tn(k * k_dec, v)
        o = o * lax.rsqrt(jnp.mean(o * o, axis=-1, keepdims=True) + NORM_EPS)
        y_ref[:, h * RET_DV:(h + 1) * RET_DV] = (
            o * _silu(ga_ref[:, h * RET_DV:(h + 1) * RET_DV].astype(F32))).astype(y_ref.dtype)


def _ret_decay_table():
    log_g = jnp.log(1.0 - jnp.exp2(-5.0 - jnp.arange(RET_HEADS, dtype=F32)))
    return log_g


def _retention_prompt(qk, v, ga, batch, seq):
    C = RET_CHUNK
    nc = seq // C
    log_g = _ret_decay_table()
    dec = jnp.stack([log_g, jnp.exp(C * log_g)], axis=1).reshape(-1)
    half = RET_DK // 2
    cos, sin = _rope_tables(jnp.arange(seq, dtype=F32))
    row = lambda b, c: (b * nc + c, 0)
    return pl.pallas_call(
        _ret_chunk_kernel,
        grid=(batch, nc),
        in_specs=[
            pl.BlockSpec(memory_space=pltpu.SMEM),
            pl.BlockSpec((C, RET_QK), lambda b, c: (b * nc + c, 0)),
            pl.BlockSpec((C, RET_QK), lambda b, c: (b * nc + c, 1)),
            pl.BlockSpec((C, RET_V), row),
            pl.BlockSpec((C, RET_V), row),
            pl.BlockSpec((C, half), lambda b, c: (c, 0)),
            pl.BlockSpec((C, half), lambda b, c: (c, 0)),
        ],
        out_specs=[
            pl.BlockSpec((C, RET_V), row),
            pl.BlockSpec((None, RET_HEADS, RET_DK, RET_DV), lambda b, c: (b, 0, 0, 0)),
        ],
        out_shape=[
            jax.ShapeDtypeStruct((batch * seq, RET_V), BF16),
            jax.ShapeDtypeStruct((batch, RET_HEADS, RET_DK, RET_DV), F32),
        ],
        compiler_params=_params("arbitrary", "arbitrary"),
        name="retention_chunk",
    )(dec, qk, qk, v, ga, cos, sin)


def _ret_step(dec_ref, q_ref, k_ref, v_ref, ga_ref, cos_ref, sin_ref, s_ref, y_ref, so_ref):
    row0 = lax.broadcasted_iota(jnp.int32, (8, 1), 0)
    cos = cos_ref[...]
    sin = sin_ref[...]
    for b, h in [(b, h) for b in range(q_ref.shape[0]) for h in range(RET_HEADS)]:
        g = dec_ref[h]
        q = _rope(q_ref[b, :, h * RET_DK:(h + 1) * RET_DK], cos, sin)
        k = _rope(k_ref[b, :, h * RET_DK:(h + 1) * RET_DK], cos, sin) * (RET_DK ** -0.5)
        v = v_ref[b, :, h * RET_DV:(h + 1) * RET_DV]
        s = s_ref[b, h]
        qk = jnp.sum(q * k, axis=-1, keepdims=True)
        q8 = jnp.broadcast_to(q, (8, RET_DK))
        o = qk * v + g * _dot(q8, s)[0:1, :]
        k_hi = k.astype(BF16).astype(F32)
        k_lo = k - k_hi
        v_hi = v.astype(BF16).astype(F32)
        v_lo = v - v_hi
        k8 = jnp.where(row0 < 2, k_hi, jnp.where(row0 == 2, k_lo, 0.0))
        v8 = jnp.where((row0 == 0) | (row0 == 2), v_hi, jnp.where(row0 == 1, v_lo, 0.0))
        so_ref[b, h] = g * s + _dot_tn(k8, v8)
        o = o * lax.rsqrt(jnp.mean(o * o, axis=-1, keepdims=True) + NORM_EPS)
        y_ref[b, :, h * RET_DV:(h + 1) * RET_DV] = (
            o * _silu(ga_ref[b, :, h * RET_DV:(h + 1) * RET_DV])).astype(y_ref.dtype)


def _ret_step_operands(qk, v, ga, state, n_steps, step_index):
    nb = qk.shape[0]
    nq = nb // n_steps
    assert nq * n_steps == nb
    g = jnp.exp(_ret_decay_table())
    cos, sin = _rope_tables(PAST_LEN + jnp.arange(1, dtype=F32))
    r3 = lambda t: t.reshape(nb, 1, t.shape[-1])
    tab = pl.BlockSpec((1, RET_DK // 2), lambda *ids: (0, 0))
    vec = lambda n: pl.BlockSpec((nq, 1, n), lambda *ids: (step_index(*ids), 0, 0))
    st = pl.BlockSpec((nq, RET_HEADS, RET_DK, RET_DV), lambda *ids: (step_index(*ids), 0, 0, 0))
    args = [g, r3(qk[:, :RET_QK]), r3(qk[:, RET_QK:]), r3(v), r3(ga), cos, sin, state]
    in_specs = [pl.BlockSpec(memory_space=pltpu.SMEM), vec(RET_QK), vec(RET_QK), vec(RET_V), vec(RET_V),
                tab, tab, st]
    out_specs = [vec(RET_V), st]
    out_shape = [jax.ShapeDtypeStruct((nb, 1, RET_V), BF16), jax.ShapeDtypeStruct(state.shape, F32)]
    return args, in_specs, out_specs, out_shape


def _head_sums(xs, ones_bd):
    rows = xs[0].shape[0]
    stack = jnp.concatenate(
        [x[:, g * MXU_DIM:(g + 1) * MXU_DIM] for x in xs for g in range(N_GROUPS)], axis=0)
    s = jnp.dot(stack.astype(BF16), ones_bd, preferred_element_type=F32)
    return [jnp.concatenate([s[(i * N_GROUPS + g) * rows:(i * N_GROUPS + g + 1) * rows]
                             for g in range(N_GROUPS)], axis=-1) for i in range(len(xs))]


def _rwkv_prep(rkv, rkv_prev, wa, wa_prev, mu_rkv, mu_wa, w0, w2p, a0, a2p, k_k, k_a, ones_bd):
    z = rkv + (rkv_prev - rkv) * mu_rkv
    zwa = wa + (wa_prev - wa) * mu_wa
    r = z[:, :RW_C]
    k = z[:, RW_C:2 * RW_C]
    v = z[:, 2 * RW_C:]
    wpre = w0 + _dot(jnp.tanh(zwa), w2p)
    log_decay = -math.exp(-0.5) * _sigmoid(wpre)
    a = _sigmoid(a0 + _dot(zwa, a2p))
    kk = k * k_k
    nrm = jnp.maximum(jnp.sqrt(_head_sums([kk * kk], ones_bd)[0]), 1e-12)
    kk = kk / nrm
    k = k * (1.0 + (a - 1.0) * k_a)
    return r, log_decay, k, v, kk, a


def _rwkv_post(o, r, k, v, gb, r_k, ln_w, ln_b, ones_bd):
    o_sum, rk_sum = _head_sums([o, r * k * r_k], ones_bd)
    d = o - o_sum * (1.0 / RW_HEAD)
    var = _head_sums([d * d], ones_bd)[0] * (1.0 / RW_HEAD)
    on = d * lax.rsqrt(var + RW_GN_EPS) * ln_w + ln_b
    return (on + rk_sum * v) * _silu(gb)


def _block_diag(x, head_of_lane):
    xb = x.astype(BF16)
    zero = jnp.zeros_like(xb)
    return jnp.concatenate(
        [jnp.where(head_of_lane == j, xb, zero) for j in range(HEADS_PER_GROUP)], axis=0)


def _rwkv_chunk_kernel(sh_ref, gb_ref, mu_rkv_ref, mu_wa_ref, w0_ref, w2_ref, a0_ref, a2_ref,
                       kk_ref, ka_ref, rk_ref, lnw_ref, lnb_ref, ones_ref,
                       rt_dec_ref, rt_q_ref, rt_k_ref, rt_v_ref, rt_ga_ref, rt_cos_ref, rt_sin_ref, rt_s_ref,
                       y_ref, so_ref, rt_y_ref, rt_so_ref, carry, state):
    _ret_step(rt_dec_ref, rt_q_ref, rt_k_ref, rt_v_ref, rt_ga_ref, rt_cos_ref, rt_sin_ref, rt_s_ref,
              rt_y_ref, rt_so_ref)
    c = pl.program_id(1)
    nc = pl.num_programs(1)
    C = RW_CHUNK
    n_seq = sh_ref.shape[0]

    @pl.when(c == 0)
    def _():
        carry[...] = jnp.zeros_like(carry)
        state[...] = jnp.zeros_like(state)

    ones_bd = ones_ref[...]
    row = lax.broadcasted_iota(jnp.int32, (C, 1), 0)
    ti = lax.broadcasted_iota(jnp.int32, (C, C), 0)
    tj = lax.broadcasted_iota(jnp.int32, (C, C), 1)
    tri = (ti >= tj).astype(BF16)

    tok = []
    for q in range(n_seq):
        sh = sh_ref[q].astype(F32)
        sh_prev = jnp.where(row == 0, carry[q], pltpu.roll(sh, 1, 0))
        carry[q] = sh[C - 1:C, :]
        rkv, wa = sh[:, :3 * RW_C], sh[:, 3 * RW_C:]
        rkv_prev, wa_prev = sh_prev[:, :3 * RW_C], sh_prev[:, 3 * RW_C:]
        r, lw, k, v, kk, a = _rwkv_prep(rkv, rkv_prev, wa, wa_prev, mu_rkv_ref[...], mu_wa_ref[...],
                                        w0_ref[...], w2_ref[...], a0_ref[...], a2_ref[...],
                                        kk_ref[...], ka_ref[...], ones_bd)
        lw_hi = lw.astype(BF16)
        lw_lo = (lw - lw_hi.astype(F32)).astype(BF16)
        cl = (jnp.dot(tri, lw_hi, preferred_element_type=F32)
              + jnp.dot(tri, lw_lo, preferred_element_type=F32))
        cl_last = cl[C - 1:C, :]
        e_neg = jnp.exp(-cl)
        e_rem = jnp.exp(cl_last - cl)
        beta = a * kk
        tok.append(dict(
            r=r, k=k, v=v,
            ag=-kk * jnp.exp(cl - lw), rg=r * jnp.exp(cl), bg=beta * e_neg, kg=k * e_neg,
            bg_c=beta * e_rem, kg_c=k * e_rem,
            g_chunk=jnp.exp(cl_last)))

    lane = lax.broadcasted_iota(jnp.int32, (1, MXU_DIM), 1)
    head_of_lane = lane >> HEAD_SHIFT
    t_col = lax.broadcasted_iota(jnp.int32, (C, MXU_DIM), 0)
    i_lane = lax.broadcasted_iota(jnp.int32, (C, MXU_DIM), 1) & (RW_HEAD - 1)
    strict = t_col > i_lane
    incl = t_col >= i_lane
    eye = (t_col == i_lane).astype(F32)
    vrow_head = lax.broadcasted_iota(jnp.int32, (MXU_DIM, MXU_DIM), 0) >> HEAD_SHIFT
    klane_head = lax.broadcasted_iota(jnp.int32, (MXU_DIM, MXU_DIM), 1) >> HEAD_SHIFT
    same_head = vrow_head == klane_head

    units = [(q, g) for q in range(n_seq) for g in range(N_GROUPS)]
    U = range(len(units))
    bd = lambda x: _block_diag(x, head_of_lane)
    grp = lambda name, u: tok[units[u][0]][name][:, units[u][1] * MXU_DIM:(units[u][1] + 1) * MXU_DIM]
    s_bd = [state[q, g] for q, g in units]
    lhs = [jnp.concatenate([grp("ag", u), grp("rg", u)], axis=0) for u in U]
    ab = [_dot_nt(lhs[u], bd(grp("bg", u))) for u in U]
    ak = [_dot_nt(lhs[u], bd(grp("kg", u))) for u in U]
    n_pow = [jnp.where(strict, ab[u][:C], 0.0) for u in U]
    a_ak = [jnp.where(strict, ak[u][:C], 0.0) for u in U]
    a_rb = [jnp.where(incl, ab[u][C:], 0.0) for u in U]
    a_rk = [jnp.where(incl, ak[u][C:], 0.0) for u in U]
    v_bd = [bd(grp("v", u)) for u in U]
    sv = [_dot_nt(lhs[u], s_bd[u]) for u in U]
    av = [_dot(jnp.concatenate([a_ak[u], a_rk[u]], axis=0), v_bd[u]) for u in U]
    t_inv = [eye + n_pow[u] for u in U]
    n_pow = [_dot(n_pow[u], bd(n_pow[u])) for u in U]
    for _ in range(int(math.log2(C)) - 2):
        prod = [_dot(jnp.concatenate([n_pow[u], t_inv[u]], axis=0), bd(n_pow[u])) for u in U]
        n_pow = [prod[u][:C] for u in U]
        t_inv = [t_inv[u] + prod[u][C:] for u in U]
    t_inv = [t_inv[u] + _dot(t_inv[u], bd(n_pow[u])) for u in U]
    p = [_dot(t_inv[u], bd(sv[u][:C] + av[u][:C])) for u in U]
    o = [sv[u][C:] + _dot(a_rb[u], bd(p[u])) + av[u][C:] for u in U]
    for u, (q, g) in enumerate(units):
        upd = _dot_tn(jnp.concatenate([p[u], grp("v", u)], axis=0),
                      jnp.concatenate([grp("bg_c", u), grp("kg_c", u)], axis=0))
        state[q, g] = s_bd[u] * grp("g_chunk", u) + jnp.where(same_head, upd, 0.0)
    for q in range(n_seq):
        o_q = jnp.concatenate(o[q * N_GROUPS:(q + 1) * N_GROUPS], axis=-1)
        t = tok[q]
        y = _rwkv_post(o_q, t["r"], t["k"], t["v"], gb_ref[q].astype(F32), rk_ref[...], lnw_ref[...],
                       lnb_ref[...], ones_bd)
        y_ref[q] = y.astype(y_ref.dtype)

    @pl.when(c == nc - 1)
    def _():
        for q in range(n_seq):
            for g in range(N_GROUPS):
                for j in range(HEADS_PER_GROUP):
                    blk = slice(j * RW_HEAD, (j + 1) * RW_HEAD)
                    so_ref[q, g * HEADS_PER_GROUP + j] = state[q, g, blk, blk]


def _rwkv_consts(p):
    row = lambda t: t.reshape(1, -1).astype(F32)
    zeros = jnp.zeros((RW_LORA, RW_C), F32)
    w2p = jnp.concatenate([p["rw_w2"], zeros], axis=0).astype(BF16)
    a2p = jnp.concatenate([zeros, p["rw_a2"]], axis=0).astype(BF16)
    hl = jnp.arange(MXU_DIM) // RW_HEAD
    ones_bd = (hl[:, None] == hl[None, :]).astype(BF16)
    mu = p["rw_mu"]
    return dict(mu_rkv=row(mu[:3 * RW_C]), mu_wa=row(mu[3 * RW_C:]), w0=row(p["rw_w0"]), w2p=w2p,
                a0=row(p["rw_a0"]), a2p=a2p, k_k=row(p["rw_k_k"]), k_a=row(p["rw_k_a"]),
                r_k=row(p["rw_r_k"]), ln_w=row(p["rw_ln_w"]), ln_b=row(p["rw_ln_b"]), ones_bd=ones_bd)


def _const_spec(arr, ngrid):
    zeros = (0,) * arr.ndim
    if ngrid == 1:
        return pl.BlockSpec(arr.shape, lambda i: zeros)
    return pl.BlockSpec(arr.shape, lambda i, j: zeros)


def _rwkv_prompt(sh, gb, cs, batch, seq, ret_sample):
    C = RW_CHUNK
    nq = RW_SEQS_PER_STEP
    nc = seq // C
    n_sh = 3 * RW_C + 2 * RW_LORA
    blk = lambda n: pl.BlockSpec((nq, C, n), lambda b, c: (b, c, 0))
    consts = [cs[n] for n in ("mu_rkv", "mu_wa", "w0", "w2p", "a0", "a2p", "k_k", "k_a", "r_k",
                              "ln_w", "ln_b", "ones_bd")]
    rt_args, rt_in, rt_out, rt_shape = _ret_step_operands(
        *ret_sample, n_steps=(batch // nq) * nc, step_index=lambda b, c: b * nc + c)
    y, s, rt_y, rt_s = pl.pallas_call(
        _rwkv_chunk_kernel,
        grid=(batch // nq, nc),
        in_specs=[blk(n_sh), blk(RW_C)] + [_const_spec(a, 2) for a in consts] + rt_in,
        out_specs=[blk(RW_C),
                   pl.BlockSpec((nq, RW_HEADS, RW_HEAD, RW_HEAD), lambda b, c: (b, 0, 0, 0))] + rt_out,
        out_shape=[jax.ShapeDtypeStruct((batch, seq, RW_C), BF16),
                   jax.ShapeDtypeStruct((batch, RW_HEADS, RW_HEAD, RW_HEAD), F32)] + rt_shape,
        scratch_shapes=[pltpu.VMEM((nq, 1, n_sh), F32),
                        pltpu.VMEM((nq, N_GROUPS, MXU_DIM, MXU_DIM), F32)],
        compiler_params=_params("arbitrary", "arbitrary"),
        name="rwkv_chunk",
    )(sh.reshape(batch, seq, n_sh), gb.reshape(batch, seq, RW_C), *consts, *rt_args)
    return y.reshape(batch * seq, RW_C), s, rt_y.reshape(rt_y.shape[0], RET_V), rt_s


def _rwkv_step_kernel(r_ref, k_ref, v_ref, wa_ref, gb_ref, s_ref,
                      mu_r_ref, mu_k_ref, mu_v_ref, mu_wa_ref, w0_ref, w2t_ref, a0_ref, a2t_ref,
                      kk_ref, ka_ref, rk_ref, lnw_ref, lnb_ref,
                      y_ref, so_ref, o_scr):
    nb = y_ref.shape[-1]
    lerp = lambda ref, mu: ref[:, :nb] + (ref[:, nb:] - ref[:, :nb]) * mu[...]
    r = lerp(r_ref, mu_r_ref)
    k = lerp(k_ref, mu_k_ref)
    v = lerp(v_ref, mu_v_ref)
    zwa = lerp(wa_ref, mu_wa_ref)
    wpre = w0_ref[...] + _dot(w2t_ref[...], jnp.tanh(zwa[:RW_LORA]))
    decay = jnp.exp(-math.exp(-0.5) * _sigmoid(wpre))
    a = _sigmoid(a0_ref[...] + _dot(a2t_ref[...], zwa[RW_LORA:]))
    kk = k * kk_ref[...]
    kk = kk / jnp.maximum(jnp.sqrt(jnp.sum(kk * kk, axis=0, keepdims=True)), 1e-12)
    k = k * (1.0 + (a - 1.0) * ka_ref[...])
    beta = a * kk
    for i in range(RW_HEAD):
        s = s_ref[i]
        sk = jnp.sum(s * kk, axis=0, keepdims=True)
        s_new = s * decay - sk * beta + v[i:i + 1, :] * k
        so_ref[i] = s_new
        o_scr[i:i + 1, :] = jnp.sum(s_new * r, axis=0, keepdims=True)
    o = o_scr[...]
    d = o - jnp.mean(o, axis=0, keepdims=True)
    var = jnp.mean(d * d, axis=0, keepdims=True)
    on = d * lax.rsqrt(var + RW_GN_EPS) * lnw_ref[...] + lnb_ref[...]
    bonus = jnp.sum(r * k * rk_ref[...], axis=0, keepdims=True) * v
    y_ref[...] = ((on + bonus) * _silu(gb_ref[...])).astype(y_ref.dtype)


def _rwkv_sample(sh2, gb, p, state):
    nb = gb.shape[0]
    sht = sh2.T
    st = jnp.transpose(state, (0, 2, 3, 4, 1))
    col = lambda t: t.reshape(-1, 1).astype(F32)
    mu = col(p["rw_mu"])
    n_head_blocks = RW_C // RW_HEAD
    rows = lambda off: pl.BlockSpec((RW_HEAD, 2 * nb), lambda h: (h + off, 0))
    colblk = lambda off: pl.BlockSpec((RW_HEAD, 1), lambda h: (h + off, 0))
    lora_blk = pl.BlockSpec((2 * RW_LORA, 2 * nb), lambda h: (3 * RW_C // (2 * RW_LORA), 0))
    st_blk = pl.BlockSpec((None, None, RW_HEAD, RW_HEAD, nb), lambda h: (0, h, 0, 0, 0))
    wt_blk = pl.BlockSpec((RW_HEAD, RW_LORA), lambda h: (h, 0))
    yt, so = pl.pallas_call(
        _rwkv_step_kernel,
        grid=(RW_HEADS,),
        in_specs=[rows(0), rows(n_head_blocks), rows(2 * n_head_blocks), lora_blk,
                  pl.BlockSpec((RW_HEAD, nb), lambda h: (h, 0)), st_blk,
                  colblk(0), colblk(n_head_blocks), colblk(2 * n_head_blocks),
                  pl.BlockSpec((2 * RW_LORA, 1), lambda h: (0, 0)),
                  colblk(0), wt_blk, colblk(0), wt_blk] + [colblk(0)] * 5,
        out_specs=[pl.BlockSpec((RW_HEAD, nb), lambda h: (h, 0)), st_blk],
        out_shape=[jax.ShapeDtypeStruct((RW_C, nb), BF16), jax.ShapeDtypeStruct(st.shape, F32)],
        scratch_shapes=[pltpu.VMEM((RW_HEAD, nb), F32)],
        compiler_params=_params("arbitrary"),
        name="rwkv_step",
    )(sht, sht, sht, sht, gb.T, st,
      mu[:3 * RW_C], mu[:3 * RW_C], mu[:3 * RW_C], mu[3 * RW_C:],
      col(p["rw_w0"]), p["rw_w2"].T.astype(BF16), col(p["rw_a0"]), p["rw_a2"].T.astype(BF16),
      col(p["rw_k_k"]), col(p["rw_k_a"]), col(p["rw_r_k"]), col(p["rw_ln_w"]), col(p["rw_ln_b"]))
    return yt.T, jnp.transpose(so, (0, 4, 1, 2, 3))


def _tail_kernel(ya_ref, yb_ref, m_ref, x_ref, p_ref, wda_ref, wdb_ref, wout_ref, wple_ref, wgate_ref,
                 pg_ref, fg_ref, y_ref):
    m = m_ref[...].astype(F32)
    merged = (_sigmoid(m[:, :D_MODEL]) * jnp.dot(ya_ref[...], wda_ref[...], preferred_element_type=F32)
              + _sigmoid(m[:, D_MODEL:]) * jnp.dot(yb_ref[...], wdb_ref[...], preferred_element_type=F32))
    x = x_ref[...] + _dot(merged, wout_ref[...])
    gate = _sigmoid(_dot(_rms(x, pg_ref[...]), wgate_ref[...]))
    x = x + _dot(p_ref[...], wple_ref[...]) * gate
    y_ref[...] = _rms(x, fg_ref[...])


def _tail(ya, yb, m, x, p, w, tm):
    rows = x.shape[0]
    tile = lambda n: pl.BlockSpec((tm, n), lambda i: (i, 0))
    consts = [w["wda"], w["wdb"], w["wout"], w["wple"], w["wgate"], w["ple_g"], w["final_g"]]
    return pl.pallas_call(
        _tail_kernel,
        grid=(rows // tm,),
        in_specs=[tile(RET_V), tile(RW_C), tile(2 * D_MODEL), tile(D_MODEL), tile(PLE_DIM)]
        + [_const_spec(a, 1) for a in consts],
        out_specs=tile(D_MODEL),
        out_shape=jax.ShapeDtypeStruct((rows, D_MODEL), F32),
        compiler_params=_params("arbitrary"),
        name="tail",
    )(ya, yb, m, x, p, *consts)


def _layer_weights(p):
    return dict(
        w_in=p["w_in"].astype(BF16),
        wda=p["w_down_a"].astype(BF16), wdb=p["w_down_b"].astype(BF16), wout=p["w_out"].astype(BF16),
        wple=p["w_ple"].astype(BF16), wgate=p["w_ple_gate"].astype(BF16),
        ple_g=p["ple_norm_g"].reshape(1, -1), final_g=p["final_norm_g"].reshape(1, -1),
    )


def _layer_paths(x_p, pe_p, x_s, h_prev, s_ret, s_rw, pe_s, p, w, cs):
    batch, seq, d = x_p.shape
    rows = batch * seq
    nb = x_s.shape[0]
    xp2 = x_p.reshape(rows, d)
    xs2 = x_s.reshape(nb, d)
    shift_p = _rmsnorm(x_p[:, -1, :], p["norm_g"], F32, batch)
    h_s = _rmsnorm(xs2, p["norm_g"], F32, nb)
    hcat = jnp.concatenate([h_s, h_prev], axis=0)
    qk_s, v_s, ga_s, sh_s, gb_s, m_s = _in_proj(hcat, p["norm_g"], w["w_in"], False, F32, 2 * nb)
    qk, v, ga, sh, gb, m = _in_proj(xp2, p["norm_g"], w["w_in"], True, BF16, PROMPT_PROJ_ROWS)
    ya_p, ret_p = _retention_prompt(qk, v, ga, batch, seq)
    yb_p, rw_p, ya_s, ret_s = _rwkv_prompt(sh, gb, cs, batch, seq,
                                           (qk_s[:nb], v_s[:nb], ga_s[:nb], s_ret))
    y_p = _tail(ya_p, yb_p, m, xp2, pe_p.reshape(rows, PLE_DIM), w, PROMPT_TAIL_ROWS)
    yb_s, rw_s = _rwkv_sample(sh_s, gb_s[:nb], p, s_rw)
    y_s = _tail(ya_s, yb_s, m_s[:nb], xs2, pe_s.reshape(nb, PLE_DIM), w, nb)
    return (y_p.reshape(batch, seq, d), shift_p, ret_p, rw_p,
            y_s.reshape(nb, 1, d), h_s, ret_s, rw_s)


def kernel(x_prompt, x_sample, state_ret, state_rwkv, state_shift, p_prompt, p_sample, norm_g, w_in, rw_mu, rw_w0, rw_w2, rw_a0, rw_a2, rw_k_k, rw_k_a, rw_r_k, rw_ln_w, rw_ln_b, w_down_a, w_down_b, w_out, w_ple, ple_norm_g, w_ple_gate, final_norm_g):
    assert norm_g.shape[0] == 1, "single-layer step"
    p = dict(norm_g=norm_g[0], w_in=w_in[0], rw_mu=rw_mu[0], rw_w0=rw_w0[0], rw_w2=rw_w2[0],
             rw_a0=rw_a0[0], rw_a2=rw_a2[0], rw_k_k=rw_k_k[0], rw_k_a=rw_k_a[0], rw_r_k=rw_r_k[0],
             rw_ln_w=rw_ln_w[0], rw_ln_b=rw_ln_b[0], w_down_a=w_down_a[0], w_down_b=w_down_b[0],
             w_out=w_out[0], w_ple=w_ple[0], ple_norm_g=ple_norm_g[0], w_ple_gate=w_ple_gate[0],
             final_norm_g=final_norm_g)
    w = _layer_weights(p)
    cs = _rwkv_consts(p)
    y_p, sh_p, ret_p, rw_p, y_s, sh_s, ret_s, rw_s = _layer_paths(
        x_prompt, p_prompt[0], x_sample, state_shift[0], state_ret[0], state_rwkv, p_sample[0], p, w, cs)
    return (y_p, y_s, ret_p[None], rw_p[None], sh_p[None], ret_s[None], rw_s, sh_s[None])
```

```python
import functools
import math

import jax
import jax.numpy as jnp
from jax import lax
from jax.experimental import pallas as pl
from jax.experimental.pallas import tpu as pltpu

F32 = jnp.float32
BF16 = jnp.bfloat16

D_MODEL = 1024
RET_HEADS = 4
RET_DK = 256
RET_DV = 512
RET_QK = RET_HEADS * RET_DK
RET_V = RET_HEADS * RET_DV
ROPE_BASE = 10000.0
RW_HEAD = 64
RW_HEADS = D_MODEL // RW_HEAD
RW_C = RW_HEADS * RW_HEAD
RW_LORA = 64
RW_GN_EPS = 1e-5 * RW_HEAD
RW_CHUNK = 64
PLE_DIM = 256
NORM_EPS = 1e-6
PAST_LEN = 16384

LANES = 128
MXU_DIM = 256
HEADS_PER_GROUP = MXU_DIM // RW_HEAD
N_GROUPS = RW_C // MXU_DIM
HEAD_SHIFT = RW_HEAD.bit_length() - 1
VMEM_LIMIT_BYTES = 56 * 1024 * 1024
PROMPT_PROJ_ROWS = 256
PROMPT_TAIL_ROWS = 256
RW_SEQS_PER_STEP = 4


def _params(*sem):
    return pltpu.CompilerParams(dimension_semantics=sem, vmem_limit_bytes=VMEM_LIMIT_BYTES)


def _dot(a, b):
    return jnp.dot(a.astype(BF16), b.astype(BF16), preferred_element_type=F32)


def _dot_nt(a, b):
    return lax.dot_general(a.astype(BF16), b.astype(BF16), (((1,), (1,)), ((), ())),
                           preferred_element_type=F32)


def _dot_tn(a, b):
    return lax.dot_general(a.astype(BF16), b.astype(BF16), (((0,), (0,)), ((), ())),
                           preferred_element_type=F32)


def _sigmoid(x):
    return 0.5 * jnp.tanh(0.5 * x) + 0.5


def _silu(x):
    return x * _sigmoid(x)


def _rms(x, g):
    return x * lax.rsqrt(jnp.mean(x * x, axis=-1, keepdims=True) + NORM_EPS) * g


def _rmsnorm_kernel(x_ref, g_ref, o_ref):
    o_ref[...] = _rms(x_ref[...], g_ref[...]).astype(o_ref.dtype)


def _rmsnorm(x, g, out_dtype, tm):
    m, d = x.shape
    return pl.pallas_call(
        _rmsnorm_kernel,
        grid=(m // tm,),
        in_specs=[pl.BlockSpec((tm, d), lambda i: (i, 0)), pl.BlockSpec((1, d), lambda i: (0, 0))],
        out_specs=pl.BlockSpec((tm, d), lambda i: (i, 0)),
        out_shape=jax.ShapeDtypeStruct((m, d), out_dtype),
        compiler_params=_params("arbitrary"),
        name="rmsnorm",
    )(x, g.reshape(1, d))


PROJ_WIDTHS = (2 * RET_QK, RET_V, RET_V, 3 * RW_C + 2 * RW_LORA, RW_C, 2 * D_MODEL)


def _in_proj_kernel(x_ref, g_ref, w_ref, *out_refs, normalize):
    x = x_ref[...]
    h = (_rms(x, g_ref[...]) if normalize else x).astype(BF16)
    off = 0
    for o_ref, n in zip(out_refs, PROJ_WIDTHS):
        o_ref[...] = jnp.dot(h, w_ref[:, off:off + n], preferred_element_type=F32).astype(o_ref.dtype)
        off += n


def _in_proj(x, g, w_in, normalize, out_dtype, tm):
    m, d = x.shape
    n_all = w_in.shape[1]
    return pl.pallas_call(
        functools.partial(_in_proj_kernel, normalize=normalize),
        grid=(m // tm,),
        in_specs=[pl.BlockSpec((tm, d), lambda i: (i, 0)), pl.BlockSpec((1, d), lambda i: (0, 0)),
                  pl.BlockSpec((d, n_all), lambda i: (0, 0), pipeline_mode=pl.Buffered(1))],
        out_specs=[pl.BlockSpec((tm, n), lambda i: (i, 0)) for n in PROJ_WIDTHS],
        out_shape=[jax.ShapeDtypeStruct((m, n), out_dtype) for n in PROJ_WIDTHS],
        compiler_params=_params("arbitrary"),
        name="in_proj",
    )(x, g.reshape(1, d), w_in)


def _in_proj_ret_kernel(dec_ref, x_ref, g_ref, w_ref, cos_ref, sin_ref,
                        ya_ref, sh_ref, gb_ref, m_ref, s_ref):
    @pl.when(pl.program_id(1) == 0)
    def _():
        s_ref[...] = jnp.zeros_like(s_ref)

    h = _rms(x_ref[...], g_ref[...]).astype(BF16)
    o_qk, o_v, o_ga, o_sh, o_gb, o_m = [sum(PROJ_WIDTHS[:i]) for i in range(len(PROJ_WIDTHS))]
    proj = lambda off, n: jnp.dot(h, w_ref[:, off:off + n], preferred_element_type=F32)
    qk = proj(o_qk, 2 * RET_QK)
    v = proj(o_v, RET_V)
    ga = proj(o_ga, RET_V)
    sh_ref[...] = proj(o_sh, PROJ_WIDTHS[3]).astype(sh_ref.dtype)
    gb_ref[...] = proj(o_gb, PROJ_WIDTHS[4]).astype(gb_ref.dtype)
    m_ref[...] = proj(o_m, PROJ_WIDTHS[5]).astype(m_ref.dtype)
    _ret_chunk(qk[:, :RET_QK], qk[:, RET_QK:], v, ga, cos_ref[...], sin_ref[...], dec_ref, s_ref, ya_ref)


def _in_proj_retention(x, g, w_in, batch, seq, tm):
    m, d = x.shape
    nt = seq // tm
    n_all = w_in.shape[1]
    log_g = _ret_decay_table()
    dec = jnp.stack([log_g, jnp.exp(tm * log_g)], axis=1).reshape(-1)
    half = RET_DK // 2
    cos, sin = _rope_tables(jnp.arange(seq, dtype=F32))
    row = lambda b, t: (b * nt + t, 0)
    widths = (RET_V,) + PROJ_WIDTHS[3:]
    return pl.pallas_call(
        _in_proj_ret_kernel,
        grid=(batch, nt),
        in_specs=[pl.BlockSpec(memory_space=pltpu.SMEM),
                  pl.BlockSpec((tm, d), row), pl.BlockSpec((1, d), lambda b, t: (0, 0)),
                  pl.BlockSpec((d, n_all), lambda b, t: (0, 0), pipeline_mode=pl.Buffered(1)),
                  pl.BlockSpec((tm, half), lambda b, t: (t, 0)),
                  pl.BlockSpec((tm, half), lambda b, t: (t, 0))],
        out_specs=[pl.BlockSpec((tm, n), row) for n in widths]
        + [pl.BlockSpec((None, RET_HEADS, RET_DK, RET_DV), lambda b, t: (b, 0, 0, 0))],
        out_shape=[jax.ShapeDtypeStruct((m, n), BF16) for n in widths]
        + [jax.ShapeDtypeStruct((batch, RET_HEADS, RET_DK, RET_DV), F32)],
        compiler_params=_params("arbitrary", "arbitrary"),
        name="in_proj_retention",
    )(dec, x, g.reshape(1, d), w_in, cos, sin)


def _rope(x, cos, sin):
    half = x.shape[-1] // 2
    x1, x2 = x[:, :half], x[:, half:]
    return jnp.concatenate([x1 * cos - x2 * sin, x2 * cos + x1 * sin], axis=-1)


def _rope_tables(pos):
    half = RET_DK // 2
    inv = ROPE_BASE ** (-jnp.arange(half, dtype=F32) / half)
    ang = pos[:, None] * inv[None, :]
    return jnp.cos(ang), jnp.sin(ang)


def _ret_chunk(q, k, v, ga, cos, sin, dec_ref, s_ref, y_ref):
    C = q.shape[0]
    H = range(RET_HEADS)
    ti = lax.broadcasted_iota(jnp.int32, (C, C), 0)
    tj = lax.broadcasted_iota(jnp.int32, (C, C), 1)
    rel = (ti - tj).astype(F32)
    idx = lax.broadcasted_iota(jnp.int32, (C, 1), 0).astype(F32)
    lg = [dec_ref[2 * h] for h in H]
    qh = [_rope(q[:, h * RET_DK:(h + 1) * RET_DK], cos, sin) for h in H]
    kh = [_rope(k[:, h * RET_DK:(h + 1) * RET_DK], cos, sin) * (RET_DK ** -0.5) for h in H]
    vh = [v[:, h * RET_DV:(h + 1) * RET_DV].astype(BF16) for h in H]
    s = [s_ref[h] for h in H]
    inner = [_dot_nt(qh[h], kh[h]) * jnp.where(rel >= 0, jnp.exp(jnp.maximum(rel, 0.0) * lg[h]), 0.0)
             for h in H]
    o = [_dot(inner[h], vh[h]) + _dot(qh[h] * jnp.exp((idx + 1.0) * lg[h]), s[h]) for h in H]
    for h in H:
        s_ref[h] = dec_ref[2 * h + 1] * s[h] + _dot_tn(kh[h] * jnp.exp((C - 1.0 - idx) * lg[h]), vh[h])
    for h in H:
        on = o[h] * lax.rsqrt(jnp.mean(o[h] * o[h], axis=-1, keepdims=True) + NORM_EPS)
        y_ref[:, h * RET_DV:(h + 1) * RET_DV] = (
            on * _silu(ga[:, h * RET_DV:(h + 1) * RET_DV])).astype(y_ref.dtype)


def _ret_decay_table():
    log_g = jnp.log(1.0 - jnp.exp2(-5.0 - jnp.arange(RET_HEADS, dtype=F32)))
    return log_g


def _ret_step(dec_ref, q_ref, k_ref, v_ref, ga_ref, cos_ref, sin_ref, s_ref, y_ref, so_ref):
    row0 = lax.broadcasted_iota(jnp.int32, (8, 1), 0)
    cos = cos_ref[...]
    sin = sin_ref[...]
    for b, h in [(b, h) for b in range(q_ref.shape[0]) for h in range(RET_HEADS)]:
        g = dec_ref[h]
        q = _rope(q_ref[b, :, h * RET_DK:(h + 1) * RET_DK], cos, sin)
        k = _rope(k_ref[b, :, h * RET_DK:(h + 1) * RET_DK], cos, sin) * (RET_DK ** -0.5)
        v = v_ref[b, :, h * RET_DV:(h + 1) * RET_DV]
        s = s_ref[b, h]
        qk = jnp.sum(q * k, axis=-1, keepdims=True)
        q8 = jnp.broadcast_to(q, (8, RET_DK))
        o = qk * v + g * _dot(q8, s)[0:1, :]
        k_hi = k.astype(BF16).astype(F32)
        k_lo = k - k_hi
        v_hi = v.astype(BF16).astype(F32)
        v_lo = v - v_hi
        k8 = jnp.where(row0 < 2, k_hi, jnp.where(row0 == 2, k_lo, 0.0))
        v8 = jnp.where((row0 == 0) | (row0 == 2), v_hi, jnp.where(row0 == 1, v_lo, 0.0))
        so_ref[b, h] = g * s + _dot_tn(k8, v8)
        o = o * lax.rsqrt(jnp.mean(o * o, axis=-1, keepdims=True) + NORM_EPS)
        y_ref[b, :, h * RET_DV:(h + 1) * RET_DV] = (
            o * _silu(ga_ref[b, :, h * RET_DV:(h + 1) * RET_DV])).astype(y_ref.dtype)


def _ret_step_operands(qk, v, ga, state, n_steps, step_index):
    nb = qk.shape[0]
    nq = nb // n_steps
    assert nq * n_steps == nb
    g = jnp.exp(_ret_decay_table())
    cos, sin = _rope_tables(PAST_LEN + jnp.arange(1, dtype=F32))
    r3 = lambda t: t.reshape(nb, 1, t.shape[-1])
    tab = pl.BlockSpec((1, RET_DK // 2), lambda *ids: (0, 0))
    vec = lambda n: pl.BlockSpec((nq, 1, n), lambda *ids: (step_index(*ids), 0, 0))
    st = pl.BlockSpec((nq, RET_HEADS, RET_DK, RET_DV), lambda *ids: (step_index(*ids), 0, 0, 0))
    args = [g, r3(qk[:, :RET_QK]), r3(qk[:, RET_QK:]), r3(v), r3(ga), cos, sin, state]
    in_specs = [pl.BlockSpec(memory_space=pltpu.SMEM), vec(RET_QK), vec(RET_QK), vec(RET_V), vec(RET_V),
                tab, tab, st]
    out_specs = [vec(RET_V), st]
    out_shape = [jax.ShapeDtypeStruct((nb, 1, RET_V), BF16), jax.ShapeDtypeStruct(state.shape, F32)]
    return args, in_specs, out_specs, out_shape


def _head_sums(xs, ones_bd):
    rows = xs[0].shape[0]
    stack = jnp.concatenate(
        [x[:, g * MXU_DIM:(g + 1) * MXU_DIM] for x in xs for g in range(N_GROUPS)], axis=0)
    s = jnp.dot(stack.astype(BF16), ones_bd, preferred_element_type=F32)
    return [jnp.concatenate([s[(i * N_GROUPS + g) * rows:(i * N_GROUPS + g + 1) * rows]
                             for g in range(N_GROUPS)], axis=-1) for i in range(len(xs))]


def _rwkv_prep(rkv, rkv_prev, wa, wa_prev, mu_rkv, mu_wa, w0, w2p, a0, a2p, k_k, k_a, ones_bd):
    z = rkv + (rkv_prev - rkv) * mu_rkv
    zwa = wa + (wa_prev - wa) * mu_wa
    r = z[:, :RW_C]
    k = z[:, RW_C:2 * RW_C]
    v = z[:, 2 * RW_C:]
    wpre = w0 + _dot(jnp.tanh(zwa), w2p)
    log_decay = -math.exp(-0.5) * _sigmoid(wpre)
    a = _sigmoid(a0 + _dot(zwa, a2p))
    kk = k * k_k
    kk = kk * lax.rsqrt(jnp.maximum(_head_sums([kk * kk], ones_bd)[0], 1e-24))
    k = k * (1.0 + (a - 1.0) * k_a)
    return r, log_decay, k, v, kk, a


def _rwkv_post(o, r, k, v, gb, r_k, ln_w, ln_b, ones_bd):
    o_sum, rk_sum = _head_sums([o, r * k * r_k], ones_bd)
    d = o - o_sum * (1.0 / RW_HEAD)
    var = _head_sums([d * d], ones_bd)[0] * (1.0 / RW_HEAD)
    on = d * lax.rsqrt(var + RW_GN_EPS) * ln_w + ln_b
    return (on + rk_sum * v) * _silu(gb)


def _block_diag(x, head_of_lane):
    xb = x.astype(BF16)
    zero = jnp.zeros_like(xb)
    return jnp.concatenate(
        [jnp.where(head_of_lane == j, xb, zero) for j in range(HEADS_PER_GROUP)], axis=0)


def _rwkv_chunk_kernel(sh_ref, gb_ref, mu_rkv_ref, mu_wa_ref, w0_ref, w2_ref, a0_ref, a2_ref,
                       kk_ref, ka_ref, rk_ref, lnw_ref, lnb_ref, ones_ref,
                       rt_dec_ref, rt_q_ref, rt_k_ref, rt_v_ref, rt_ga_ref, rt_cos_ref, rt_sin_ref, rt_s_ref,
                       y_ref, so_ref, rt_y_ref, rt_so_ref, carry, state):
    _ret_step(rt_dec_ref, rt_q_ref, rt_k_ref, rt_v_ref, rt_ga_ref, rt_cos_ref, rt_sin_ref, rt_s_ref,
              rt_y_ref, rt_so_ref)
    c = pl.program_id(1)
    nc = pl.num_programs(1)
    C = RW_CHUNK
    n_seq = sh_ref.shape[0]

    @pl.when(c == 0)
    def _():
        carry[...] = jnp.zeros_like(carry)
        state[...] = jnp.zeros_like(state)

    ones_bd = ones_ref[...]
    row = lax.broadcasted_iota(jnp.int32, (C, 1), 0)
    ti = lax.broadcasted_iota(jnp.int32, (C, C), 0)
    tj = lax.broadcasted_iota(jnp.int32, (C, C), 1)
    tri = (ti >= tj).astype(BF16)

    tok = []
    for q in range(n_seq):
        sh = sh_ref[q].astype(F32)
        sh_prev = jnp.where(row == 0, carry[q], pltpu.roll(sh, 1, 0))
        carry[q] = sh[C - 1:C, :]
        rkv, wa = sh[:, :3 * RW_C], sh[:, 3 * RW_C:]
        rkv_prev, wa_prev = sh_prev[:, :3 * RW_C], sh_prev[:, 3 * RW_C:]
        r, lw, k, v, kk, a = _rwkv_prep(rkv, rkv_prev, wa, wa_prev, mu_rkv_ref[...], mu_wa_ref[...],
                                        w0_ref[...], w2_ref[...], a0_ref[...], a2_ref[...],
                                        kk_ref[...], ka_ref[...], ones_bd)
        lw_hi = lw.astype(BF16)
        lw_lo = (lw - lw_hi.astype(F32)).astype(BF16)
        cl = (jnp.dot(tri, lw_hi, preferred_element_type=F32)
              + jnp.dot(tri, lw_lo, preferred_element_type=F32))
        cl_last = cl[C - 1:C, :]
        e_neg = jnp.exp(-cl)
        e_rem = jnp.exp(cl_last - cl)
        beta = a * kk
        tok.append(dict(
            r=r, k=k, v=v,
            ag=-kk * jnp.exp(cl - lw), rg=r * jnp.exp(cl), bg=beta * e_neg, kg=k * e_neg,
            bg_c=beta * e_rem, kg_c=k * e_rem,
            g_chunk=jnp.exp(cl_last)))

    lane = lax.broadcasted_iota(jnp.int32, (1, MXU_DIM), 1)
    head_of_lane = lane >> HEAD_SHIFT
    t_col = lax.broadcasted_iota(jnp.int32, (C, MXU_DIM), 0)
    i_lane = lax.broadcasted_iota(jnp.int32, (C, MXU_DIM), 1) & (RW_HEAD - 1)
    strict = t_col > i_lane
    incl = t_col >= i_lane
    eye = (t_col == i_lane).astype(F32)
    vrow_head = lax.broadcasted_iota(jnp.int32, (MXU_DIM, MXU_DIM), 0) >> HEAD_SHIFT
    klane_head = lax.broadcasted_iota(jnp.int32, (MXU_DIM, MXU_DIM), 1) >> HEAD_SHIFT
    same_head = vrow_head == klane_head

    units = [(q, g) for q in range(n_seq) for g in range(N_GROUPS)]
    U = range(len(units))
    bd = lambda x: _block_diag(x, head_of_lane)
    grp = lambda name, u: tok[units[u][0]][name][:, units[u][1] * MXU_DIM:(units[u][1] + 1) * MXU_DIM]
    s_bd = [state[q, g] for q, g in units]
    lhs = [jnp.concatenate([grp("ag", u), grp("rg", u)], axis=0) for u in U]
    ab = [_dot_nt(lhs[u], bd(grp("bg", u))) for u in U]
    ak = [_dot_nt(lhs[u], bd(grp("kg", u))) for u in U]
    n_pow = [jnp.where(strict, ab[u][:C], 0.0) for u in U]
    a_ak = [jnp.where(strict, ak[u][:C], 0.0) for u in U]
    a_rb = [jnp.where(incl, ab[u][C:], 0.0) for u in U]
    a_rk = [jnp.where(incl, ak[u][C:], 0.0) for u in U]
    v_bd = [bd(grp("v", u)) for u in U]
    sv = [_dot_nt(lhs[u], s_bd[u]) for u in U]
    av = [_dot(jnp.concatenate([a_ak[u], a_rk[u]], axis=0), v_bd[u]) for u in U]
    t_inv = [eye + n_pow[u] for u in U]
    n_pow = [_dot(n_pow[u], bd(n_pow[u])) for u in U]
    for _ in range(int(math.log2(C)) - 2):
        prod = [_dot(jnp.concatenate([n_pow[u], t_inv[u]], axis=0), bd(n_pow[u])) for u in U]
        n_pow = [prod[u][:C] for u in U]
        t_inv = [t_inv[u] + prod[u][C:] for u in U]
    t_inv = [t_inv[u] + _dot(t_inv[u], bd(n_pow[u])) for u in U]
    p = [_dot(t_inv[u], bd(sv[u][:C] + av[u][:C])) for u in U]
    o = [sv[u][C:] + _dot(a_rb[u], bd(p[u])) + av[u][C:] for u in U]
    for u, (q, g) in enumerate(units):
        upd = _dot_tn(jnp.concatenate([p[u], grp("v", u)], axis=0),
                      jnp.concatenate([grp("bg_c", u), grp("kg_c", u)], axis=0))
        state[q, g] = s_bd[u] * grp("g_chunk", u) + jnp.where(same_head, upd, 0.0)
    for q in range(n_seq):
        o_q = jnp.concatenate(o[q * N_GROUPS:(q + 1) * N_GROUPS], axis=-1)
        t = tok[q]
        y = _rwkv_post(o_q, t["r"], t["k"], t["v"], gb_ref[q].astype(F32), rk_ref[...], lnw_ref[...],
                       lnb_ref[...], ones_bd)
        y_ref[q] = y.astype(y_ref.dtype)

    @pl.when(c == nc - 1)
    def _():
        for q in range(n_seq):
            for g in range(N_GROUPS):
                for j in range(HEADS_PER_GROUP):
                    blk = slice(j * RW_HEAD, (j + 1) * RW_HEAD)
                    so_ref[q, g * HEADS_PER_GROUP + j] = state[q, g, blk, blk]


def _rwkv_consts(p):
    row = lambda t: t.reshape(1, -1).astype(F32)
    zeros = jnp.zeros((RW_LORA, RW_C), F32)
    w2p = jnp.concatenate([p["rw_w2"], zeros], axis=0).astype(BF16)
    a2p = jnp.concatenate([zeros, p["rw_a2"]], axis=0).astype(BF16)
    hl = jnp.arange(MXU_DIM) // RW_HEAD
    ones_bd = (hl[:, None] == hl[None, :]).astype(BF16)
    mu = p["rw_mu"]
    return dict(mu_rkv=row(mu[:3 * RW_C]), mu_wa=row(mu[3 * RW_C:]), w0=row(p["rw_w0"]), w2p=w2p,
                a0=row(p["rw_a0"]), a2p=a2p, k_k=row(p["rw_k_k"]), k_a=row(p["rw_k_a"]),
                r_k=row(p["rw_r_k"]), ln_w=row(p["rw_ln_w"]), ln_b=row(p["rw_ln_b"]), ones_bd=ones_bd)


def _const_spec(arr, ngrid):
    zeros = (0,) * arr.ndim
    if ngrid == 1:
        return pl.BlockSpec(arr.shape, lambda i: zeros)
    return pl.BlockSpec(arr.shape, lambda i, j: zeros)


def _rwkv_prompt(sh, gb, cs, batch, seq, ret_sample):
    C = RW_CHUNK
    nq = RW_SEQS_PER_STEP
    nc = seq // C
    n_sh = 3 * RW_C + 2 * RW_LORA
    blk = lambda n: pl.BlockSpec((nq, C, n), lambda b, c: (b, c, 0))
    consts = [cs[n] for n in ("mu_rkv", "mu_wa", "w0", "w2p", "a0", "a2p", "k_k", "k_a", "r_k",
                              "ln_w", "ln_b", "ones_bd")]
    rt_args, rt_in, rt_out, rt_shape = _ret_step_operands(
        *ret_sample, n_steps=(batch // nq) * nc, step_index=lambda b, c: b * nc + c)
    y, s, rt_y, rt_s = pl.pallas_call(
        _rwkv_chunk_kernel,
        grid=(batch // nq, nc),
        in_specs=[blk(n_sh), blk(RW_C)] + [_const_spec(a, 2) for a in consts] + rt_in,
        out_specs=[blk(RW_C),
                   pl.BlockSpec((nq, RW_HEADS, RW_HEAD, RW_HEAD), lambda b, c: (b, 0, 0, 0))] + rt_out,
        out_shape=[jax.ShapeDtypeStruct((batch, seq, RW_C), BF16),
                   jax.ShapeDtypeStruct((batch, RW_HEADS, RW_HEAD, RW_HEAD), F32)] + rt_shape,
        scratch_shapes=[pltpu.VMEM((nq, 1, n_sh), F32),
                        pltpu.VMEM((nq, N_GROUPS, MXU_DIM, MXU_DIM), F32)],
        compiler_params=_params("arbitrary", "arbitrary"),
        name="rwkv_chunk",
    )(sh.reshape(batch, seq, n_sh), gb.reshape(batch, seq, RW_C), *consts, *rt_args)
    return y.reshape(batch * seq, RW_C), s, rt_y.reshape(rt_y.shape[0], RET_V), rt_s


def _rwkv_step_kernel(r_ref, k_ref, v_ref, wa_ref, gb_ref, s_ref,
                      mu_r_ref, mu_k_ref, mu_v_ref, mu_wa_ref, w0_ref, w2t_ref, a0_ref, a2t_ref,
                      kk_ref, ka_ref, rk_ref, lnw_ref, lnb_ref,
                      y_ref, so_ref, o_scr):
    nb = y_ref.shape[-1]
    lerp = lambda ref, mu: ref[:, :nb] + (ref[:, nb:] - ref[:, :nb]) * mu[...]
    r = lerp(r_ref, mu_r_ref)
    k = lerp(k_ref, mu_k_ref)
    v = lerp(v_ref, mu_v_ref)
    zwa = lerp(wa_ref, mu_wa_ref)
    wpre = w0_ref[...] + _dot(w2t_ref[...], jnp.tanh(zwa[:RW_LORA]))
    decay = jnp.exp(-math.exp(-0.5) * _sigmoid(wpre))
    a = _sigmoid(a0_ref[...] + _dot(a2t_ref[...], zwa[RW_LORA:]))
    kk = k * kk_ref[...]
    kk = kk * lax.rsqrt(jnp.maximum(jnp.sum(kk * kk, axis=0, keepdims=True), 1e-24))
    k = k * (1.0 + (a - 1.0) * ka_ref[...])
    beta = a * kk
    for i in range(RW_HEAD):
        s = s_ref[i]
        sk = jnp.sum(s * kk, axis=0, keepdims=True)
        s_new = s * decay - sk * beta + v[i:i + 1, :] * k
        so_ref[i] = s_new
        o_scr[i:i + 1, :] = jnp.sum(s_new * r, axis=0, keepdims=True)
    o = o_scr[...]
    d = o - jnp.mean(o, axis=0, keepdims=True)
    var = jnp.mean(d * d, axis=0, keepdims=True)
    on = d * lax.rsqrt(var + RW_GN_EPS) * lnw_ref[...] + lnb_ref[...]
    bonus = jnp.sum(r * k * rk_ref[...], axis=0, keepdims=True) * v
    y_ref[...] = ((on + bonus) * _silu(gb_ref[...])).astype(y_ref.dtype)


def _rwkv_sample(sh2, gb, p, state):
    nb = gb.shape[0]
    sht = sh2.T
    st = jnp.transpose(state, (0, 2, 3, 4, 1))
    col = lambda t: t.reshape(-1, 1).astype(F32)
    mu = col(p["rw_mu"])
    n_head_blocks = RW_C // RW_HEAD
    rows = lambda off: pl.BlockSpec((RW_HEAD, 2 * nb), lambda h: (h + off, 0))
    colblk = lambda off: pl.BlockSpec((RW_HEAD, 1), lambda h: (h + off, 0))
    lora_blk = pl.BlockSpec((2 * RW_LORA, 2 * nb), lambda h: (3 * RW_C // (2 * RW_LORA), 0))
    st_blk = pl.BlockSpec((None, None, RW_HEAD, RW_HEAD, nb), lambda h: (0, h, 0, 0, 0))
    wt_blk = pl.BlockSpec((RW_HEAD, RW_LORA), lambda h: (h, 0))
    yt, so = pl.pallas_call(
        _rwkv_step_kernel,
        grid=(RW_HEADS,),
        in_specs=[rows(0), rows(n_head_blocks), rows(2 * n_head_blocks), lora_blk,
                  pl.BlockSpec((RW_HEAD, nb), lambda h: (h, 0)), st_blk,
                  colblk(0), colblk(n_head_blocks), colblk(2 * n_head_blocks),
                  pl.BlockSpec((2 * RW_LORA, 1), lambda h: (0, 0)),
                  colblk(0), wt_blk, colblk(0), wt_blk] + [colblk(0)] * 5,
        out_specs=[pl.BlockSpec((RW_HEAD, nb), lambda h: (h, 0)), st_blk],
        out_shape=[jax.ShapeDtypeStruct((RW_C, nb), BF16), jax.ShapeDtypeStruct(st.shape, F32)],
        scratch_shapes=[pltpu.VMEM((RW_HEAD, nb), F32)],
        compiler_params=_params("arbitrary"),
        name="rwkv_step",
    )(sht, sht, sht, sht, gb.T, st,
      mu[:3 * RW_C], mu[:3 * RW_C], mu[:3 * RW_C], mu[3 * RW_C:],
      col(p["rw_w0"]), p["rw_w2"].T.astype(BF16), col(p["rw_a0"]), p["rw_a2"].T.astype(BF16),
      col(p["rw_k_k"]), col(p["rw_k_a"]), col(p["rw_r_k"]), col(p["rw_ln_w"]), col(p["rw_ln_b"]))
    return yt.T, jnp.transpose(so, (0, 4, 1, 2, 3))


def _tail_kernel(ya_ref, yb_ref, m_ref, x_ref, p_ref, wda_ref, wdb_ref, wout_ref, wple_ref, wgate_ref,
                 pg_ref, fg_ref, y_ref):
    m = m_ref[...].astype(F32)
    merged = (_sigmoid(m[:, :D_MODEL]) * jnp.dot(ya_ref[...], wda_ref[...], preferred_element_type=F32)
              + _sigmoid(m[:, D_MODEL:]) * jnp.dot(yb_ref[...], wdb_ref[...], preferred_element_type=F32))
    x = x_ref[...] + _dot(merged, wout_ref[...])
    gate = _sigmoid(_dot(_rms(x, pg_ref[...]), wgate_ref[...]))
    x = x + _dot(p_ref[...], wple_ref[...]) * gate
    y_ref[...] = _rms(x, fg_ref[...])


def _tail(ya, yb, m, x, p, w, tm):
    rows = x.shape[0]
    tile = lambda n: pl.BlockSpec((tm, n), lambda i: (i, 0))
    consts = [w["wda"], w["wdb"], w["wout"], w["wple"], w["wgate"], w["ple_g"], w["final_g"]]
    return pl.pallas_call(
        _tail_kernel,
        grid=(rows // tm,),
        in_specs=[tile(RET_V), tile(RW_C), tile(2 * D_MODEL), tile(D_MODEL), tile(PLE_DIM)]
        + [_const_spec(a, 1) for a in consts],
        out_specs=tile(D_MODEL),
        out_shape=jax.ShapeDtypeStruct((rows, D_MODEL), F32),
        compiler_params=_params("arbitrary"),
        name="tail",
    )(ya, yb, m, x, p, *consts)


def _layer_weights(p):
    return dict(
        w_in=p["w_in"].astype(BF16),
        wda=p["w_down_a"].astype(BF16), wdb=p["w_down_b"].astype(BF16), wout=p["w_out"].astype(BF16),
        wple=p["w_ple"].astype(BF16), wgate=p["w_ple_gate"].astype(BF16),
        ple_g=p["ple_norm_g"].reshape(1, -1), final_g=p["final_norm_g"].reshape(1, -1),
    )


def _layer_paths(x_p, pe_p, x_s, h_prev, s_ret, s_rw, pe_s, p, w, cs):
    batch, seq, d = x_p.shape
    rows = batch * seq
    nb = x_s.shape[0]
    xp2 = x_p.reshape(rows, d)
    xs2 = x_s.reshape(nb, d)
    shift_p = _rmsnorm(x_p[:, -1, :], p["norm_g"], F32, batch)
    h_s = _rmsnorm(xs2, p["norm_g"], F32, nb)
    hcat = jnp.concatenate([h_s, h_prev], axis=0)
    qk_s, v_s, ga_s, sh_s, gb_s, m_s = _in_proj(hcat, p["norm_g"], w["w_in"], False, F32, 2 * nb)
    ya_p, sh, gb, m, ret_p = _in_proj_retention(xp2, p["norm_g"], w["w_in"], batch, seq, PROMPT_PROJ_ROWS)
    yb_p, rw_p, ya_s, ret_s = _rwkv_prompt(sh, gb, cs, batch, seq,
                                           (qk_s[:nb], v_s[:nb], ga_s[:nb], s_ret))
    y_p = _tail(ya_p, yb_p, m, xp2, pe_p.reshape(rows, PLE_DIM), w, PROMPT_TAIL_ROWS)
    yb_s, rw_s = _rwkv_sample(sh_s, gb_s[:nb], p, s_rw)
    y_s = _tail(ya_s, yb_s, m_s[:nb], xs2, pe_s.reshape(nb, PLE_DIM), w, nb)
    return (y_p.reshape(batch, seq, d), shift_p, ret_p, rw_p,
            y_s.reshape(nb, 1, d), h_s, ret_s, rw_s)


def kernel(x_prompt, x_sample, state_ret, state_rwkv, state_shift, p_prompt, p_sample, norm_g, w_in, rw_mu, rw_w0, rw_w2, rw_a0, rw_a2, rw_k_k, rw_k_a, rw_r_k, rw_ln_w, rw_ln_b, w_down_a, w_down_b, w_out, w_ple, ple_norm_g, w_ple_gate, final_norm_g):
    assert norm_g.shape[0] == 1, "single-layer step"
    p = dict(norm_g=norm_g[0], w_in=w_in[0], rw_mu=rw_mu[0], rw_w0=rw_w0[0], rw_w2=rw_w2[0],
             rw_a0=rw_a0[0], rw_a2=rw_a2[0], rw_k_k=rw_k_k[0], rw_k_a=rw_k_a[0], rw_r_k=rw_r_k[0],
             rw_ln_w=rw_ln_w[0], rw_ln_b=rw_ln_b[0], w_down_a=w_down_a[0], w_down_b=w_down_b[0],
             w_out=w_out[0], w_ple=w_ple[0], ple_norm_g=ple_norm_g[0], w_ple_gate=w_ple_gate[0],
             final_norm_g=final_norm_g)
    w = _layer_weights(p)
    cs = _rwkv_consts(p)
    y_p, sh_p, ret_p, rw_p, y_s, sh_s, ret_s, rw_s = _layer_paths(
        x_prompt, p_prompt[0], x_sample, state_shift[0], state_ret[0], state_rwkv, p_sample[0], p, w, cs)
    return (y_p, y_s, ret_p[None], rw_p[None], sh_p[None], ret_s[None], rw_s, sh_s[None])
```

```python
import functools
import math

import jax
import jax.numpy as jnp
from jax import lax
from jax.experimental import pallas as pl
from jax.experimental.pallas import tpu as pltpu

F32 = jnp.float32
BF16 = jnp.bfloat16

D_MODEL = 1024
RET_HEADS = 4
RET_DK = 256
RET_DV = 512
RET_QK = RET_HEADS * RET_DK
RET_V = RET_HEADS * RET_DV
ROPE_BASE = 10000.0
RW_HEAD = 64
RW_HEADS = D_MODEL // RW_HEAD
RW_C = RW_HEADS * RW_HEAD
RW_LORA = 64
RW_GN_EPS = 1e-5 * RW_HEAD
RW_CHUNK = 64
PLE_DIM = 256
NORM_EPS = 1e-6
PAST_LEN = 16384

LANES = 128
MXU_DIM = 256
HEADS_PER_GROUP = MXU_DIM // RW_HEAD
N_GROUPS = RW_C // MXU_DIM
HEAD_SHIFT = RW_HEAD.bit_length() - 1
VMEM_LIMIT_BYTES = 56 * 1024 * 1024
PROMPT_PROJ_ROWS = 256
PROMPT_TAIL_ROWS = 256
RW_SEQS_PER_STEP = 4


def _params(*sem):
    return pltpu.CompilerParams(dimension_semantics=sem, vmem_limit_bytes=VMEM_LIMIT_BYTES)


def _dot(a, b):
    return jnp.dot(a.astype(BF16), b.astype(BF16), preferred_element_type=F32)


def _dot_nt(a, b):
    return lax.dot_general(a.astype(BF16), b.astype(BF16), (((1,), (1,)), ((), ())),
                           preferred_element_type=F32)


def _dot_tn(a, b):
    return lax.dot_general(a.astype(BF16), b.astype(BF16), (((0,), (0,)), ((), ())),
                           preferred_element_type=F32)


def _sigmoid(x):
    return 0.5 * jnp.tanh(0.5 * x) + 0.5


def _silu(x):
    return x * _sigmoid(x)


def _rms(x, g):
    return x * lax.rsqrt(jnp.mean(x * x, axis=-1, keepdims=True) + NORM_EPS) * g


def _rmsnorm_kernel(x_ref, g_ref, o_ref):
    o_ref[...] = _rms(x_ref[...], g_ref[...]).astype(o_ref.dtype)


def _rmsnorm(x, g, out_dtype, tm):
    m, d = x.shape
    return pl.pallas_call(
        _rmsnorm_kernel,
        grid=(m // tm,),
        in_specs=[pl.BlockSpec((tm, d), lambda i: (i, 0)), pl.BlockSpec((1, d), lambda i: (0, 0))],
        out_specs=pl.BlockSpec((tm, d), lambda i: (i, 0)),
        out_shape=jax.ShapeDtypeStruct((m, d), out_dtype),
        compiler_params=_params("arbitrary"),
        name="rmsnorm",
    )(x, g.reshape(1, d))


PROJ_WIDTHS = (2 * RET_QK, RET_V, RET_V, 3 * RW_C + 2 * RW_LORA, RW_C, 2 * D_MODEL)


def _in_proj_kernel(x_ref, g_ref, w_ref, *out_refs, normalize):
    x = x_ref[...]
    h = (_rms(x, g_ref[...]) if normalize else x).astype(BF16)
    off = 0
    for o_ref, n in zip(out_refs, PROJ_WIDTHS):
        o_ref[...] = jnp.dot(h, w_ref[:, off:off + n], preferred_element_type=F32).astype(o_ref.dtype)
        off += n


def _in_proj(x, g, w_in, normalize, out_dtype, tm):
    m, d = x.shape
    n_all = w_in.shape[1]
    return pl.pallas_call(
        functools.partial(_in_proj_kernel, normalize=normalize),
        grid=(m // tm,),
        in_specs=[pl.BlockSpec((tm, d), lambda i: (i, 0)), pl.BlockSpec((1, d), lambda i: (0, 0)),
                  pl.BlockSpec((d, n_all), lambda i: (0, 0), pipeline_mode=pl.Buffered(1))],
        out_specs=[pl.BlockSpec((tm, n), lambda i: (i, 0)) for n in PROJ_WIDTHS],
        out_shape=[jax.ShapeDtypeStruct((m, n), out_dtype) for n in PROJ_WIDTHS],
        compiler_params=_params("arbitrary"),
        name="in_proj",
    )(x, g.reshape(1, d), w_in)


def _in_proj_ret_kernel(dec_ref, x_ref, g_ref, w_ref, cos_ref, sin_ref,
                        mu_rkv_ref, mu_wa_ref, w0_ref, w2_ref, a0_ref, a2_ref, kk_ref, ka_ref, ones_ref,
                        ya_ref, tok_ref, gb_ref, m_ref, s_ref, carry):
    @pl.when(pl.program_id(1) == 0)
    def _():
        s_ref[...] = jnp.zeros_like(s_ref)
        carry[...] = jnp.zeros_like(carry)

    tm = x_ref.shape[0]
    h = _rms(x_ref[...], g_ref[...]).astype(BF16)
    o_qk, o_v, o_ga, o_sh, o_gb, o_m = [sum(PROJ_WIDTHS[:i]) for i in range(len(PROJ_WIDTHS))]
    proj = lambda off, n: jnp.dot(h, w_ref[:, off:off + n], preferred_element_type=F32)
    sh = proj(o_sh, PROJ_WIDTHS[3])
    row = lax.broadcasted_iota(jnp.int32, (tm, 1), 0)
    sh_prev = jnp.where(row == 0, carry[...], pltpu.roll(sh, 1, 0))
    carry[...] = sh[tm - 1:tm, :]
    r, lw, k, vv, kk, a = _rwkv_prep(sh[:, :3 * RW_C], sh_prev[:, :3 * RW_C], sh[:, 3 * RW_C:],
                                     sh_prev[:, 3 * RW_C:], mu_rkv_ref[...], mu_wa_ref[...],
                                     w0_ref[...], w2_ref[...], a0_ref[...], a2_ref[...],
                                     kk_ref[...], ka_ref[...], ones_ref[...])
    lw_hi = lw.astype(BF16)
    cols = (r, k, vv, kk, a * kk, lw_hi, lw - lw_hi.astype(F32))
    for i, c in enumerate(cols):
        tok_ref[:, i * RW_C:(i + 1) * RW_C] = c.astype(tok_ref.dtype)
    qk = proj(o_qk, 2 * RET_QK)
    v = proj(o_v, RET_V)
    ga = proj(o_ga, RET_V)
    gb_ref[...] = proj(o_gb, PROJ_WIDTHS[4]).astype(gb_ref.dtype)
    m_ref[...] = proj(o_m, PROJ_WIDTHS[5]).astype(m_ref.dtype)
    _ret_chunk(qk[:, :RET_QK], qk[:, RET_QK:], v, ga, cos_ref[...], sin_ref[...], dec_ref, s_ref, ya_ref)


RW_TOK_COLS = 7 * RW_C


def _in_proj_retention(x, g, w_in, cs, batch, seq, tm):
    m, d = x.shape
    nt = seq // tm
    n_all = w_in.shape[1]
    log_g = _ret_decay_table()
    dec = jnp.stack([log_g, jnp.exp(tm * log_g)], axis=1).reshape(-1)
    half = RET_DK // 2
    cos, sin = _rope_tables(jnp.arange(seq, dtype=F32))
    row = lambda b, t: (b * nt + t, 0)
    widths = (RET_V, RW_TOK_COLS, RW_C, 2 * D_MODEL)
    consts = [cs[n] for n in ("mu_rkv", "mu_wa", "w0", "w2p", "a0", "a2p", "k_k", "k_a", "ones_bd")]
    return pl.pallas_call(
        _in_proj_ret_kernel,
        grid=(batch, nt),
        in_specs=[pl.BlockSpec(memory_space=pltpu.SMEM),
                  pl.BlockSpec((tm, d), row), pl.BlockSpec((1, d), lambda b, t: (0, 0)),
                  pl.BlockSpec((d, n_all), lambda b, t: (0, 0), pipeline_mode=pl.Buffered(1)),
                  pl.BlockSpec((tm, half), lambda b, t: (t, 0)),
                  pl.BlockSpec((tm, half), lambda b, t: (t, 0))] + [_const_spec(a, 2) for a in consts],
        out_specs=[pl.BlockSpec((tm, n), row) for n in widths]
        + [pl.BlockSpec((None, RET_HEADS, RET_DK, RET_DV), lambda b, t: (b, 0, 0, 0))],
        out_shape=[jax.ShapeDtypeStruct((m, n), BF16) for n in widths]
        + [jax.ShapeDtypeStruct((batch, RET_HEADS, RET_DK, RET_DV), F32)],
        scratch_shapes=[pltpu.VMEM((1, PROJ_WIDTHS[3]), F32)],
        compiler_params=_params("arbitrary", "arbitrary"),
        name="in_proj_retention",
    )(dec, x, g.reshape(1, d), w_in, cos, sin, *consts)


def _rope(x, cos, sin):
    half = x.shape[-1] // 2
    x1, x2 = x[:, :half], x[:, half:]
    return jnp.concatenate([x1 * cos - x2 * sin, x2 * cos + x1 * sin], axis=-1)


def _rope_tables(pos):
    half = RET_DK // 2
    inv = ROPE_BASE ** (-jnp.arange(half, dtype=F32) / half)
    ang = pos[:, None] * inv[None, :]
    return jnp.cos(ang), jnp.sin(ang)


def _ret_chunk(q, k, v, ga, cos, sin, dec_ref, s_ref, y_ref):
    C = q.shape[0]
    H = range(RET_HEADS)
    ti = lax.broadcasted_iota(jnp.int32, (C, C), 0)
    tj = lax.broadcasted_iota(jnp.int32, (C, C), 1)
    rel = (ti - tj).astype(F32)
    idx = lax.broadcasted_iota(jnp.int32, (C, 1), 0).astype(F32)
    lg = [dec_ref[2 * h] for h in H]
    qh = [_rope(q[:, h * RET_DK:(h + 1) * RET_DK], cos, sin) for h in H]
    kh = [_rope(k[:, h * RET_DK:(h + 1) * RET_DK], cos, sin) * (RET_DK ** -0.5) for h in H]
    vh = [v[:, h * RET_DV:(h + 1) * RET_DV].astype(BF16) for h in H]
    s = [s_ref[h] for h in H]
    inner = [_dot_nt(qh[h], kh[h]) * jnp.where(rel >= 0, jnp.exp(jnp.maximum(rel, 0.0) * lg[h]), 0.0)
             for h in H]
    o = [_dot(inner[h], vh[h]) + _dot(qh[h] * jnp.exp((idx + 1.0) * lg[h]), s[h]) for h in H]
    for h in H:
        s_ref[h] = dec_ref[2 * h + 1] * s[h] + _dot_tn(kh[h] * jnp.exp((C - 1.0 - idx) * lg[h]), vh[h])
    for h in H:
        on = o[h] * lax.rsqrt(jnp.mean(o[h] * o[h], axis=-1, keepdims=True) + NORM_EPS)
        y_ref[:, h * RET_DV:(h + 1) * RET_DV] = (
            on * _silu(ga[:, h * RET_DV:(h + 1) * RET_DV])).astype(y_ref.dtype)


def _ret_decay_table():
    log_g = jnp.log(1.0 - jnp.exp2(-5.0 - jnp.arange(RET_HEADS, dtype=F32)))
    return log_g


def _ret_step(dec_ref, q_ref, k_ref, v_ref, ga_ref, cos_ref, sin_ref, s_ref, y_ref, so_ref):
    row0 = lax.broadcasted_iota(jnp.int32, (8, 1), 0)
    cos = cos_ref[...]
    sin = sin_ref[...]
    for b, h in [(b, h) for b in range(q_ref.shape[0]) for h in range(RET_HEADS)]:
        g = dec_ref[h]
        q = _rope(q_ref[b, :, h * RET_DK:(h + 1) * RET_DK], cos, sin)
        k = _rope(k_ref[b, :, h * RET_DK:(h + 1) * RET_DK], cos, sin) * (RET_DK ** -0.5)
        v = v_ref[b, :, h * RET_DV:(h + 1) * RET_DV]
        s = s_ref[b, h]
        qk = jnp.sum(q * k, axis=-1, keepdims=True)
        q8 = jnp.broadcast_to(q, (8, RET_DK))
        o = qk * v + g * _dot(q8, s)[0:1, :]
        k_hi = k.astype(BF16).astype(F32)
        k_lo = k - k_hi
        v_hi = v.astype(BF16).astype(F32)
        v_lo = v - v_hi
        k8 = jnp.where(row0 < 2, k_hi, jnp.where(row0 == 2, k_lo, 0.0))
        v8 = jnp.where((row0 == 0) | (row0 == 2), v_hi, jnp.where(row0 == 1, v_lo, 0.0))
        so_ref[b, h] = g * s + _dot_tn(k8, v8)
        o = o * lax.rsqrt(jnp.mean(o * o, axis=-1, keepdims=True) + NORM_EPS)
        y_ref[b, :, h * RET_DV:(h + 1) * RET_DV] = (
            o * _silu(ga_ref[b, :, h * RET_DV:(h + 1) * RET_DV])).astype(y_ref.dtype)


def _ret_step_operands(qk, v, ga, state, n_steps, step_index):
    nb = qk.shape[0]
    nq = nb // n_steps
    assert nq * n_steps == nb
    g = jnp.exp(_ret_decay_table())
    cos, sin = _rope_tables(PAST_LEN + jnp.arange(1, dtype=F32))
    r3 = lambda t: t.reshape(nb, 1, t.shape[-1])
    tab = pl.BlockSpec((1, RET_DK // 2), lambda *ids: (0, 0))
    vec = lambda n: pl.BlockSpec((nq, 1, n), lambda *ids: (step_index(*ids), 0, 0))
    st = pl.BlockSpec((nq, RET_HEADS, RET_DK, RET_DV), lambda *ids: (step_index(*ids), 0, 0, 0))
    args = [g, r3(qk[:, :RET_QK]), r3(qk[:, RET_QK:]), r3(v), r3(ga), cos, sin, state]
    in_specs = [pl.BlockSpec(memory_space=pltpu.SMEM), vec(RET_QK), vec(RET_QK), vec(RET_V), vec(RET_V),
                tab, tab, st]
    out_specs = [vec(RET_V), st]
    out_shape = [jax.ShapeDtypeStruct((nb, 1, RET_V), BF16), jax.ShapeDtypeStruct(state.shape, F32)]
    return args, in_specs, out_specs, out_shape


def _head_sums(xs, ones_bd):
    rows = xs[0].shape[0]
    stack = jnp.concatenate(
        [x[:, g * MXU_DIM:(g + 1) * MXU_DIM] for x in xs for g in range(N_GROUPS)], axis=0)
    s = jnp.dot(stack.astype(BF16), ones_bd, preferred_element_type=F32)
    return [jnp.concatenate([s[(i * N_GROUPS + g) * rows:(i * N_GROUPS + g + 1) * rows]
                             for g in range(N_GROUPS)], axis=-1) for i in range(len(xs))]


def _rwkv_prep(rkv, rkv_prev, wa, wa_prev, mu_rkv, mu_wa, w0, w2p, a0, a2p, k_k, k_a, ones_bd):
    z = rkv + (rkv_prev - rkv) * mu_rkv
    zwa = wa + (wa_prev - wa) * mu_wa
    r = z[:, :RW_C]
    k = z[:, RW_C:2 * RW_C]
    v = z[:, 2 * RW_C:]
    wpre = w0 + _dot(jnp.tanh(zwa), w2p)
    log_decay = -math.exp(-0.5) * _sigmoid(wpre)
    a = _sigmoid(a0 + _dot(zwa, a2p))
    kk = k * k_k
    kk = kk * lax.rsqrt(jnp.maximum(_head_sums([kk * kk], ones_bd)[0], 1e-24))
    k = k * (1.0 + (a - 1.0) * k_a)
    return r, log_decay, k, v, kk, a


def _rwkv_post(o, r, k, v, gb, r_k, ln_w, ln_b, ones_bd):
    o_sum, rk_sum = _head_sums([o, r * k * r_k], ones_bd)
    d = o - o_sum * (1.0 / RW_HEAD)
    var = _head_sums([d * d], ones_bd)[0] * (1.0 / RW_HEAD)
    on = d * lax.rsqrt(var + RW_GN_EPS) * ln_w + ln_b
    return (on + rk_sum * v) * _silu(gb)


def _block_diag(x, head_of_lane):
    xb = x.astype(BF16)
    zero = jnp.zeros_like(xb)
    return jnp.concatenate(
        [jnp.where(head_of_lane == j, xb, zero) for j in range(HEADS_PER_GROUP)], axis=0)


def _rwkv_chunk_kernel(tok_ref, gb_ref, rk_ref, lnw_ref, lnb_ref, ones_ref,
                       rt_dec_ref, rt_q_ref, rt_k_ref, rt_v_ref, rt_ga_ref, rt_cos_ref, rt_sin_ref, rt_s_ref,
                       y_ref, so_ref, rt_y_ref, rt_so_ref, state):
    _ret_step(rt_dec_ref, rt_q_ref, rt_k_ref, rt_v_ref, rt_ga_ref, rt_cos_ref, rt_sin_ref, rt_s_ref,
              rt_y_ref, rt_so_ref)
    c = pl.program_id(1)
    nc = pl.num_programs(1)
    C = RW_CHUNK
    n_seq = tok_ref.shape[0]

    @pl.when(c == 0)
    def _():
        state[...] = jnp.zeros_like(state)

    ones_bd = ones_ref[...]
    ti = lax.broadcasted_iota(jnp.int32, (C, C), 0)
    tj = lax.broadcasted_iota(jnp.int32, (C, C), 1)
    tri = (ti >= tj).astype(BF16)

    tok = []
    for q in range(n_seq):
        col = lambda i: tok_ref[q, :, i * RW_C:(i + 1) * RW_C]
        r, k, v, kk, beta = [col(i).astype(F32) for i in range(5)]
        lw_hi, lw_lo = col(5), col(6)
        lw = lw_hi.astype(F32) + lw_lo.astype(F32)
        cl = (jnp.dot(tri, lw_hi, preferred_element_type=F32)
              + jnp.dot(tri, lw_lo, preferred_element_type=F32))
        cl_last = cl[C - 1:C, :]
        e_neg = jnp.exp(-cl)
        e_rem = jnp.exp(cl_last - cl)
        tok.append(dict(
            r=r, k=k, v=v,
            ag=-kk * jnp.exp(cl - lw), rg=r * jnp.exp(cl), bg=beta * e_neg, kg=k * e_neg,
            bg_c=beta * e_rem, kg_c=k * e_rem,
            g_chunk=jnp.exp(cl_last)))

    lane = lax.broadcasted_iota(jnp.int32, (1, MXU_DIM), 1)
    head_of_lane = lane >> HEAD_SHIFT
    t_col = lax.broadcasted_iota(jnp.int32, (C, MXU_DIM), 0)
    i_lane = lax.broadcasted_iota(jnp.int32, (C, MXU_DIM), 1) & (RW_HEAD - 1)
    strict = t_col > i_lane
    incl = t_col >= i_lane
    eye = (t_col == i_lane).astype(F32)
    vrow_head = lax.broadcasted_iota(jnp.int32, (MXU_DIM, MXU_DIM), 0) >> HEAD_SHIFT
    klane_head = lax.broadcasted_iota(jnp.int32, (MXU_DIM, MXU_DIM), 1) >> HEAD_SHIFT
    same_head = vrow_head == klane_head

    units = [(q, g) for q in range(n_seq) for g in range(N_GROUPS)]
    U = range(len(units))
    bd = lambda x: _block_diag(x, head_of_lane)
    grp = lambda name, u: tok[units[u][0]][name][:, units[u][1] * MXU_DIM:(units[u][1] + 1) * MXU_DIM]
    s_bd = [state[q, g] for q, g in units]
    lhs = [jnp.concatenate([grp("ag", u), grp("rg", u)], axis=0) for u in U]
    ab = [_dot_nt(lhs[u], bd(grp("bg", u))) for u in U]
    ak = [_dot_nt(lhs[u], bd(grp("kg", u))) for u in U]
    n_pow = [jnp.where(strict, ab[u][:C], 0.0) for u in U]
    a_ak = [jnp.where(strict, ak[u][:C], 0.0) for u in U]
    a_rb = [jnp.where(incl, ab[u][C:], 0.0) for u in U]
    a_rk = [jnp.where(incl, ak[u][C:], 0.0) for u in U]
    v_bd = [bd(grp("v", u)) for u in U]
    sv = [_dot_nt(lhs[u], s_bd[u]) for u in U]
    av = [_dot(jnp.concatenate([a_ak[u], a_rk[u]], axis=0), v_bd[u]) for u in U]
    t_inv = [eye + n_pow[u] for u in U]
    n_pow = [_dot(n_pow[u], bd(n_pow[u])) for u in U]
    for _ in range(int(math.log2(C)) - 2):
        prod = [_dot(jnp.concatenate([n_pow[u], t_inv[u]], axis=0), bd(n_pow[u])) for u in U]
        n_pow = [prod[u][:C] for u in U]
        t_inv = [t_inv[u] + prod[u][C:] for u in U]
    t_inv = [t_inv[u] + _dot(t_inv[u], bd(n_pow[u])) for u in U]
    p = [_dot(t_inv[u], bd(sv[u][:C] + av[u][:C])) for u in U]
    o = [sv[u][C:] + _dot(a_rb[u], bd(p[u])) + av[u][C:] for u in U]
    for u, (q, g) in enumerate(units):
        upd = _dot_tn(jnp.concatenate([p[u], grp("v", u)], axis=0),
                      jnp.concatenate([grp("bg_c", u), grp("kg_c", u)], axis=0))
        state[q, g] = s_bd[u] * grp("g_chunk", u) + jnp.where(same_head, upd, 0.0)
    for q in range(n_seq):
        o_q = jnp.concatenate(o[q * N_GROUPS:(q + 1) * N_GROUPS], axis=-1)
        t = tok[q]
        y = _rwkv_post(o_q, t["r"], t["k"], t["v"], gb_ref[q].astype(F32), rk_ref[...], lnw_ref[...],
                       lnb_ref[...], ones_bd)
        y_ref[q] = y.astype(y_ref.dtype)

    @pl.when(c == nc - 1)
    def _():
        for q in range(n_seq):
            for g in range(N_GROUPS):
                for j in range(HEADS_PER_GROUP):
                    blk = slice(j * RW_HEAD, (j + 1) * RW_HEAD)
                    so_ref[q, g * HEADS_PER_GROUP + j] = state[q, g, blk, blk]


def _rwkv_consts(p):
    row = lambda t: t.reshape(1, -1).astype(F32)
    zeros = jnp.zeros((RW_LORA, RW_C), F32)
    w2p = jnp.concatenate([p["rw_w2"], zeros], axis=0).astype(BF16)
    a2p = jnp.concatenate([zeros, p["rw_a2"]], axis=0).astype(BF16)
    hl = jnp.arange(MXU_DIM) // RW_HEAD
    ones_bd = (hl[:, None] == hl[None, :]).astype(BF16)
    mu = p["rw_mu"]
    return dict(mu_rkv=row(mu[:3 * RW_C]), mu_wa=row(mu[3 * RW_C:]), w0=row(p["rw_w0"]), w2p=w2p,
                a0=row(p["rw_a0"]), a2p=a2p, k_k=row(p["rw_k_k"]), k_a=row(p["rw_k_a"]),
                r_k=row(p["rw_r_k"]), ln_w=row(p["rw_ln_w"]), ln_b=row(p["rw_ln_b"]), ones_bd=ones_bd)


def _const_spec(arr, ngrid):
    zeros = (0,) * arr.ndim
    if ngrid == 1:
        return pl.BlockSpec(arr.shape, lambda i: zeros)
    return pl.BlockSpec(arr.shape, lambda i, j: zeros)


def _rwkv_prompt(tokv, gb, cs, batch, seq, ret_sample):
    C = RW_CHUNK
    nq = RW_SEQS_PER_STEP
    nc = seq // C
    blk = lambda n: pl.BlockSpec((nq, C, n), lambda b, c: (b, c, 0))
    consts = [cs[n] for n in ("r_k", "ln_w", "ln_b", "ones_bd")]
    rt_args, rt_in, rt_out, rt_shape = _ret_step_operands(
        *ret_sample, n_steps=(batch // nq) * nc, step_index=lambda b, c: b * nc + c)
    y, s, rt_y, rt_s = pl.pallas_call(
        _rwkv_chunk_kernel,
        grid=(batch // nq, nc),
        in_specs=[blk(RW_TOK_COLS), blk(RW_C)] + [_const_spec(a, 2) for a in consts] + rt_in,
        out_specs=[blk(RW_C),
                   pl.BlockSpec((nq, RW_HEADS, RW_HEAD, RW_HEAD), lambda b, c: (b, 0, 0, 0))] + rt_out,
        out_shape=[jax.ShapeDtypeStruct((batch, seq, RW_C), BF16),
                   jax.ShapeDtypeStruct((batch, RW_HEADS, RW_HEAD, RW_HEAD), F32)] + rt_shape,
        scratch_shapes=[pltpu.VMEM((nq, N_GROUPS, MXU_DIM, MXU_DIM), F32)],
        compiler_params=_params("arbitrary", "arbitrary"),
        name="rwkv_chunk",
    )(tokv.reshape(batch, seq, RW_TOK_COLS), gb.reshape(batch, seq, RW_C), *consts, *rt_args)
    return y.reshape(batch * seq, RW_C), s, rt_y.reshape(rt_y.shape[0], RET_V), rt_s


def _rwkv_step_kernel(r_ref, k_ref, v_ref, wa_ref, gb_ref, s_ref,
                      mu_r_ref, mu_k_ref, mu_v_ref, mu_wa_ref, w0_ref, w2t_ref, a0_ref, a2t_ref,
                      kk_ref, ka_ref, rk_ref, lnw_ref, lnb_ref,
                      y_ref, so_ref, o_scr):
    nb = y_ref.shape[-1]
    lerp = lambda ref, mu: ref[:, :nb] + (ref[:, nb:] - ref[:, :nb]) * mu[...]
    r = lerp(r_ref, mu_r_ref)
    k = lerp(k_ref, mu_k_ref)
    v = lerp(v_ref, mu_v_ref)
    zwa = lerp(wa_ref, mu_wa_ref)
    wpre = w0_ref[...] + _dot(w2t_ref[...], jnp.tanh(zwa[:RW_LORA]))
    decay = jnp.exp(-math.exp(-0.5) * _sigmoid(wpre))
    a = _sigmoid(a0_ref[...] + _dot(a2t_ref[...], zwa[RW_LORA:]))
    kk = k * kk_ref[...]
    kk = kk * lax.rsqrt(jnp.maximum(jnp.sum(kk * kk, axis=0, keepdims=True), 1e-24))
    k = k * (1.0 + (a - 1.0) * ka_ref[...])
    beta = a * kk
    for i in range(RW_HEAD):
        s = s_ref[i]
        sk = jnp.sum(s * kk, axis=0, keepdims=True)
        s_new = s * decay - sk * beta + v[i:i + 1, :] * k
        so_ref[i] = s_new
        o_scr[i:i + 1, :] = jnp.sum(s_new * r, axis=0, keepdims=True)
    o = o_scr[...]
    d = o - jnp.mean(o, axis=0, keepdims=True)
    var = jnp.mean(d * d, axis=0, keepdims=True)
    on = d * lax.rsqrt(var + RW_GN_EPS) * lnw_ref[...] + lnb_ref[...]
    bonus = jnp.sum(r * k * rk_ref[...], axis=0, keepdims=True) * v
    y_ref[...] = ((on + bonus) * _silu(gb_ref[...])).astype(y_ref.dtype)


def _rwkv_sample(sh2, gb, p, state):
    nb = gb.shape[0]
    sht = sh2.T
    st = jnp.transpose(state, (0, 2, 3, 4, 1))
    col = lambda t: t.reshape(-1, 1).astype(F32)
    mu = col(p["rw_mu"])
    n_head_blocks = RW_C // RW_HEAD
    rows = lambda off: pl.BlockSpec((RW_HEAD, 2 * nb), lambda h: (h + off, 0))
    colblk = lambda off: pl.BlockSpec((RW_HEAD, 1), lambda h: (h + off, 0))
    lora_blk = pl.BlockSpec((2 * RW_LORA, 2 * nb), lambda h: (3 * RW_C // (2 * RW_LORA), 0))
    st_blk = pl.BlockSpec((None, None, RW_HEAD, RW_HEAD, nb), lambda h: (0, h, 0, 0, 0))
    wt_blk = pl.BlockSpec((RW_HEAD, RW_LORA), lambda h: (h, 0))
    yt, so = pl.pallas_call(
        _rwkv_step_kernel,
        grid=(RW_HEADS,),
        in_specs=[rows(0), rows(n_head_blocks), rows(2 * n_head_blocks), lora_blk,
                  pl.BlockSpec((RW_HEAD, nb), lambda h: (h, 0)), st_blk,
                  colblk(0), colblk(n_head_blocks), colblk(2 * n_head_blocks),
                  pl.BlockSpec((2 * RW_LORA, 1), lambda h: (0, 0)),
                  colblk(0), wt_blk, colblk(0), wt_blk] + [colblk(0)] * 5,
        out_specs=[pl.BlockSpec((RW_HEAD, nb), lambda h: (h, 0)), st_blk],
        out_shape=[jax.ShapeDtypeStruct((RW_C, nb), BF16), jax.ShapeDtypeStruct(st.shape, F32)],
        scratch_shapes=[pltpu.VMEM((RW_HEAD, nb), F32)],
        compiler_params=_params("arbitrary"),
        name="rwkv_step",
    )(sht, sht, sht, sht, gb.T, st,
      mu[:3 * RW_C], mu[:3 * RW_C], mu[:3 * RW_C], mu[3 * RW_C:],
      col(p["rw_w0"]), p["rw_w2"].T.astype(BF16), col(p["rw_a0"]), p["rw_a2"].T.astype(BF16),
      col(p["rw_k_k"]), col(p["rw_k_a"]), col(p["rw_r_k"]), col(p["rw_ln_w"]), col(p["rw_ln_b"]))
    return yt.T, jnp.transpose(so, (0, 4, 1, 2, 3))


def _tail_kernel(ya_ref, yb_ref, m_ref, x_ref, p_ref, wda_ref, wdb_ref, wout_ref, wple_ref, wgate_ref,
                 pg_ref, fg_ref, y_ref):
    m = m_ref[...].astype(F32)
    merged = (_sigmoid(m[:, :D_MODEL]) * jnp.dot(ya_ref[...], wda_ref[...], preferred_element_type=F32)
              + _sigmoid(m[:, D_MODEL:]) * jnp.dot(yb_ref[...], wdb_ref[...], preferred_element_type=F32))
    x = x_ref[...] + _dot(merged, wout_ref[...])
    gate = _sigmoid(_dot(_rms(x, pg_ref[...]), wgate_ref[...]))
    x = x + _dot(p_ref[...], wple_ref[...]) * gate
    y_ref[...] = _rms(x, fg_ref[...])


def _tail(ya, yb, m, x, p, w, tm):
    rows = x.shape[0]
    tile = lambda n: pl.BlockSpec((tm, n), lambda i: (i, 0))
    consts = [w["wda"], w["wdb"], w["wout"], w["wple"], w["wgate"], w["ple_g"], w["final_g"]]
    return pl.pallas_call(
        _tail_kernel,
        grid=(rows // tm,),
        in_specs=[tile(RET_V), tile(RW_C), tile(2 * D_MODEL), tile(D_MODEL), tile(PLE_DIM)]
        + [_const_spec(a, 1) for a in consts],
        out_specs=tile(D_MODEL),
        out_shape=jax.ShapeDtypeStruct((rows, D_MODEL), F32),
        compiler_params=_params("arbitrary"),
        name="tail",
    )(ya, yb, m, x, p, *consts)


def _layer_weights(p):
    return dict(
        w_in=p["w_in"].astype(BF16),
        wda=p["w_down_a"].astype(BF16), wdb=p["w_down_b"].astype(BF16), wout=p["w_out"].astype(BF16),
        wple=p["w_ple"].astype(BF16), wgate=p["w_ple_gate"].astype(BF16),
        ple_g=p["ple_norm_g"].reshape(1, -1), final_g=p["final_norm_g"].reshape(1, -1),
    )


def _layer_paths(x_p, pe_p, x_s, h_prev, s_ret, s_rw, pe_s, p, w, cs):
    batch, seq, d = x_p.shape
    rows = batch * seq
    nb = x_s.shape[0]
    xp2 = x_p.reshape(rows, d)
    xs2 = x_s.reshape(nb, d)
    shift_p = _rmsnorm(x_p[:, -1, :], p["norm_g"], F32, batch)
    h_s = _rmsnorm(xs2, p["norm_g"], F32, nb)
    hcat = jnp.concatenate([h_s, h_prev], axis=0)
    qk_s, v_s, ga_s, sh_s, gb_s, m_s = _in_proj(hcat, p["norm_g"], w["w_in"], False, F32, 2 * nb)
    ya_p, tokv, gb, m, ret_p = _in_proj_retention(xp2, p["norm_g"], w["w_in"], cs, batch, seq,
                                                  PROMPT_PROJ_ROWS)
    yb_p, rw_p, ya_s, ret_s = _rwkv_prompt(tokv, gb, cs, batch, seq,
                                           (qk_s[:nb], v_s[:nb], ga_s[:nb], s_ret))
    y_p = _tail(ya_p, yb_p, m, xp2, pe_p.reshape(rows, PLE_DIM), w, PROMPT_TAIL_ROWS)
    yb_s, rw_s = _rwkv_sample(sh_s, gb_s[:nb], p, s_rw)
    y_s = _tail(ya_s, yb_s, m_s[:nb], xs2, pe_s.reshape(nb, PLE_DIM), w, nb)
    return (y_p.reshape(batch, seq, d), shift_p, ret_p, rw_p,
            y_s.reshape(nb, 1, d), h_s, ret_s, rw_s)


def kernel(x_prompt, x_sample, state_ret, state_rwkv, state_shift, p_prompt, p_sample, norm_g, w_in, rw_mu, rw_w0, rw_w2, rw_a0, rw_a2, rw_k_k, rw_k_a, rw_r_k, rw_ln_w, rw_ln_b, w_down_a, w_down_b, w_out, w_ple, ple_norm_g, w_ple_gate, final_norm_g):
    assert norm_g.shape[0] == 1, "single-layer step"
    p = dict(norm_g=norm_g[0], w_in=w_in[0], rw_mu=rw_mu[0], rw_w0=rw_w0[0], rw_w2=rw_w2[0],
             rw_a0=rw_a0[0], rw_a2=rw_a2[0], rw_k_k=rw_k_k[0], rw_k_a=rw_k_a[0], rw_r_k=rw_r_k[0],
             rw_ln_w=rw_ln_w[0], rw_ln_b=rw_ln_b[0], w_down_a=w_down_a[0], w_down_b=w_down_b[0],
             w_out=w_out[0], w_ple=w_ple[0], ple_norm_g=ple_norm_g[0], w_ple_gate=w_ple_gate[0],
             final_norm_g=final_norm_g)
    w = _layer_weights(p)
    cs = _rwkv_consts(p)
    y_p, sh_p, ret_p, rw_p, y_s, sh_s, ret_s, rw_s = _layer_paths(
        x_prompt, p_prompt[0], x_sample, state_shift[0], state_ret[0], state_rwkv, p_sample[0], p, w, cs)
    return (y_p, y_s, ret_p[None], rw_p[None], sh_p[None], ret_s[None], rw_s, sh_s[None])
```

```python
import functools
import math

import jax
import jax.numpy as jnp
from jax import lax
from jax.experimental import pallas as pl
from jax.experimental.pallas import tpu as pltpu

F32 = jnp.float32
BF16 = jnp.bfloat16

D_MODEL = 1024
RET_HEADS = 4
RET_DK = 256
RET_DV = 512
RET_QK = RET_HEADS * RET_DK
RET_V = RET_HEADS * RET_DV
ROPE_BASE = 10000.0
RW_HEAD = 64
RW_HEADS = D_MODEL // RW_HEAD
RW_C = RW_HEADS * RW_HEAD
RW_LORA = 64
RW_GN_EPS = 1e-5 * RW_HEAD
RW_CHUNK = 64
PLE_DIM = 256
NORM_EPS = 1e-6
PAST_LEN = 16384

LANES = 128
MXU_DIM = 256
HEADS_PER_GROUP = MXU_DIM // RW_HEAD
N_GROUPS = RW_C // MXU_DIM
HEAD_SHIFT = RW_HEAD.bit_length() - 1
VMEM_LIMIT_BYTES = 56 * 1024 * 1024
PROMPT_PROJ_ROWS = 256
PROMPT_TAIL_ROWS = 512
RW_SEQS_PER_STEP = 4


def _params(*sem):
    return pltpu.CompilerParams(dimension_semantics=sem, vmem_limit_bytes=VMEM_LIMIT_BYTES)


def _dot(a, b):
    return jnp.dot(a.astype(BF16), b.astype(BF16), preferred_element_type=F32)


def _dot_nt(a, b):
    return lax.dot_general(a.astype(BF16), b.astype(BF16), (((1,), (1,)), ((), ())),
                           preferred_element_type=F32)


def _dot_tn(a, b):
    return lax.dot_general(a.astype(BF16), b.astype(BF16), (((0,), (0,)), ((), ())),
                           preferred_element_type=F32)


def _sigmoid(x):
    return 0.5 * jnp.tanh(0.5 * x) + 0.5


def _silu(x):
    return x * _sigmoid(x)


def _rms(x, g):
    return x * lax.rsqrt(jnp.mean(x * x, axis=-1, keepdims=True) + NORM_EPS) * g


def _rmsnorm_kernel(x_ref, g_ref, o_ref):
    o_ref[...] = _rms(x_ref[...], g_ref[...]).astype(o_ref.dtype)


def _rmsnorm(x, g, out_dtype, tm):
    m, d = x.shape
    return pl.pallas_call(
        _rmsnorm_kernel,
        grid=(m // tm,),
        in_specs=[pl.BlockSpec((tm, d), lambda i: (i, 0)), pl.BlockSpec((1, d), lambda i: (0, 0))],
        out_specs=pl.BlockSpec((tm, d), lambda i: (i, 0)),
        out_shape=jax.ShapeDtypeStruct((m, d), out_dtype),
        compiler_params=_params("arbitrary"),
        name="rmsnorm",
    )(x, g.reshape(1, d))


PROJ_WIDTHS = (2 * RET_QK, RET_V, RET_V, 3 * RW_C + 2 * RW_LORA, RW_C, 2 * D_MODEL)


def _in_proj_kernel(x_ref, g_ref, w_ref, *out_refs, normalize):
    x = x_ref[...]
    h = (_rms(x, g_ref[...]) if normalize else x).astype(BF16)
    off = 0
    for o_ref, n in zip(out_refs, PROJ_WIDTHS):
        o_ref[...] = jnp.dot(h, w_ref[:, off:off + n], preferred_element_type=F32).astype(o_ref.dtype)
        off += n


def _in_proj(x, g, w_in, normalize, out_dtype, tm):
    m, d = x.shape
    n_all = w_in.shape[1]
    return pl.pallas_call(
        functools.partial(_in_proj_kernel, normalize=normalize),
        grid=(m // tm,),
        in_specs=[pl.BlockSpec((tm, d), lambda i: (i, 0)), pl.BlockSpec((1, d), lambda i: (0, 0)),
                  pl.BlockSpec((d, n_all), lambda i: (0, 0), pipeline_mode=pl.Buffered(1))],
        out_specs=[pl.BlockSpec((tm, n), lambda i: (i, 0)) for n in PROJ_WIDTHS],
        out_shape=[jax.ShapeDtypeStruct((m, n), out_dtype) for n in PROJ_WIDTHS],
        compiler_params=_params("arbitrary"),
        name="in_proj",
    )(x, g.reshape(1, d), w_in)


def _in_proj_ret_kernel(dec_ref, x_ref, g_ref, w_ref, cos_ref, sin_ref,
                        ya_ref, sh_ref, gb_ref, m_ref, s_ref):
    @pl.when(pl.program_id(1) == 0)
    def _():
        s_ref[...] = jnp.zeros_like(s_ref)

    h = _rms(x_ref[...], g_ref[...]).astype(BF16)
    o_qk, o_v, o_ga, o_sh, o_gb, o_m = [sum(PROJ_WIDTHS[:i]) for i in range(len(PROJ_WIDTHS))]
    proj = lambda off, n: jnp.dot(h, w_ref[:, off:off + n], preferred_element_type=F32)
    qk = proj(o_qk, 2 * RET_QK)
    v = proj(o_v, RET_V)
    ga = proj(o_ga, RET_V)
    sh_ref[...] = proj(o_sh, PROJ_WIDTHS[3]).astype(sh_ref.dtype)
    gb_ref[...] = proj(o_gb, PROJ_WIDTHS[4]).astype(gb_ref.dtype)
    m_ref[...] = proj(o_m, PROJ_WIDTHS[5]).astype(m_ref.dtype)
    _ret_chunk(qk[:, :RET_QK], qk[:, RET_QK:], v, ga, cos_ref[...], sin_ref[...], dec_ref, s_ref, ya_ref)


def _in_proj_retention(x, g, w_in, batch, seq, tm):
    m, d = x.shape
    nt = seq // tm
    n_all = w_in.shape[1]
    log_g = _ret_decay_table()
    dec = jnp.stack([log_g, jnp.exp(tm * log_g)], axis=1).reshape(-1)
    half = RET_DK // 2
    cos, sin = _rope_tables(jnp.arange(seq, dtype=F32))
    row = lambda b, t: (b * nt + t, 0)
    widths = (RET_V,) + PROJ_WIDTHS[3:]
    return pl.pallas_call(
        _in_proj_ret_kernel,
        grid=(batch, nt),
        in_specs=[pl.BlockSpec(memory_space=pltpu.SMEM),
                  pl.BlockSpec((tm, d), row), pl.BlockSpec((1, d), lambda b, t: (0, 0)),
                  pl.BlockSpec((d, n_all), lambda b, t: (0, 0), pipeline_mode=pl.Buffered(1)),
                  pl.BlockSpec((tm, half), lambda b, t: (t, 0)),
                  pl.BlockSpec((tm, half), lambda b, t: (t, 0))],
        out_specs=[pl.BlockSpec((tm, n), row) for n in widths]
        + [pl.BlockSpec((None, RET_HEADS, RET_DK, RET_DV), lambda b, t: (b, 0, 0, 0))],
        out_shape=[jax.ShapeDtypeStruct((m, n), BF16) for n in widths]
        + [jax.ShapeDtypeStruct((batch, RET_HEADS, RET_DK, RET_DV), F32)],
        compiler_params=_params("arbitrary", "arbitrary"),
        name="in_proj_retention",
    )(dec, x, g.reshape(1, d), w_in, cos, sin)


def _rope(x, cos, sin):
    half = x.shape[-1] // 2
    x1, x2 = x[:, :half], x[:, half:]
    return jnp.concatenate([x1 * cos - x2 * sin, x2 * cos + x1 * sin], axis=-1)


def _rope_tables(pos):
    half = RET_DK // 2
    inv = ROPE_BASE ** (-jnp.arange(half, dtype=F32) / half)
    ang = pos[:, None] * inv[None, :]
    return jnp.cos(ang), jnp.sin(ang)


def _ret_chunk(q, k, v, ga, cos, sin, dec_ref, s_ref, y_ref):
    C = q.shape[0]
    H = range(RET_HEADS)
    ti = lax.broadcasted_iota(jnp.int32, (C, C), 0)
    tj = lax.broadcasted_iota(jnp.int32, (C, C), 1)
    rel = (ti - tj).astype(F32)
    idx = lax.broadcasted_iota(jnp.int32, (C, 1), 0).astype(F32)
    lg = [dec_ref[2 * h] for h in H]
    qh = [_rope(q[:, h * RET_DK:(h + 1) * RET_DK], cos, sin) for h in H]
    kh = [_rope(k[:, h * RET_DK:(h + 1) * RET_DK], cos, sin) * (RET_DK ** -0.5) for h in H]
    vh = [v[:, h * RET_DV:(h + 1) * RET_DV].astype(BF16) for h in H]
    s = [s_ref[h] for h in H]
    inner = [_dot_nt(qh[h], kh[h]) * jnp.where(rel >= 0, jnp.exp(jnp.maximum(rel, 0.0) * lg[h]), 0.0)
             for h in H]
    o = [_dot(inner[h], vh[h]) + _dot(qh[h] * jnp.exp((idx + 1.0) * lg[h]), s[h]) for h in H]
    for h in H:
        s_ref[h] = dec_ref[2 * h + 1] * s[h] + _dot_tn(kh[h] * jnp.exp((C - 1.0 - idx) * lg[h]), vh[h])
    for h in H:
        on = o[h] * lax.rsqrt(jnp.mean(o[h] * o[h], axis=-1, keepdims=True) + NORM_EPS)
        y_ref[:, h * RET_DV:(h + 1) * RET_DV] = (
            on * _silu(ga[:, h * RET_DV:(h + 1) * RET_DV])).astype(y_ref.dtype)


def _ret_decay_table():
    log_g = jnp.log(1.0 - jnp.exp2(-5.0 - jnp.arange(RET_HEADS, dtype=F32)))
    return log_g


def _ret_step(dec_ref, q_ref, k_ref, v_ref, ga_ref, cos_ref, sin_ref, s_ref, y_ref, so_ref):
    row0 = lax.broadcasted_iota(jnp.int32, (8, 1), 0)
    cos = cos_ref[...]
    sin = sin_ref[...]
    for b, h in [(b, h) for b in range(q_ref.shape[0]) for h in range(RET_HEADS)]:
        g = dec_ref[h]
        q = _rope(q_ref[b, :, h * RET_DK:(h + 1) * RET_DK], cos, sin)
        k = _rope(k_ref[b, :, h * RET_DK:(h + 1) * RET_DK], cos, sin) * (RET_DK ** -0.5)
        v = v_ref[b, :, h * RET_DV:(h + 1) * RET_DV]
        s = s_ref[b, h]
        qk = jnp.sum(q * k, axis=-1, keepdims=True)
        q8 = jnp.broadcast_to(q, (8, RET_DK))
        o = qk * v + g * _dot(q8, s)[0:1, :]
        k_hi = k.astype(BF16).astype(F32)
        k_lo = k - k_hi
        v_hi = v.astype(BF16).astype(F32)
        v_lo = v - v_hi
        k8 = jnp.where(row0 < 2, k_hi, jnp.where(row0 == 2, k_lo, 0.0))
        v8 = jnp.where((row0 == 0) | (row0 == 2), v_hi, jnp.where(row0 == 1, v_lo, 0.0))
        so_ref[b, h] = g * s + _dot_tn(k8, v8)
        o = o * lax.rsqrt(jnp.mean(o * o, axis=-1, keepdims=True) + NORM_EPS)
        y_ref[b, :, h * RET_DV:(h + 1) * RET_DV] = (
            o * _silu(ga_ref[b, :, h * RET_DV:(h + 1) * RET_DV])).astype(y_ref.dtype)


def _ret_step_operands(qk, v, ga, state, n_steps, step_index):
    nb = state.shape[0]
    nq = nb // n_steps
    assert nq * n_steps == nb
    g = jnp.exp(_ret_decay_table())
    cos, sin = _rope_tables(PAST_LEN + jnp.arange(1, dtype=F32))
    r3 = lambda t: t.reshape(t.shape[0], 1, t.shape[-1])
    tab = pl.BlockSpec((1, RET_DK // 2), lambda *ids: (0, 0))
    vec = lambda n, col=0: pl.BlockSpec((nq, 1, n), lambda *ids: (step_index(*ids), 0, col))
    st = pl.BlockSpec((nq, RET_HEADS, RET_DK, RET_DV), lambda *ids: (step_index(*ids), 0, 0, 0))
    args = [g, r3(qk), r3(qk), r3(v), r3(ga), cos, sin, state]
    in_specs = [pl.BlockSpec(memory_space=pltpu.SMEM), vec(RET_QK, 0), vec(RET_QK, 1), vec(RET_V),
                vec(RET_V), tab, tab, st]
    out_specs = [vec(RET_V), st]
    out_shape = [jax.ShapeDtypeStruct((nb, 1, RET_V), BF16), jax.ShapeDtypeStruct(state.shape, F32)]
    return args, in_specs, out_specs, out_shape


def _head_sums(xs, ones_bd):
    rows = xs[0].shape[0]
    stack = jnp.concatenate(
        [x[:, g * MXU_DIM:(g + 1) * MXU_DIM] for x in xs for g in range(N_GROUPS)], axis=0)
    s = jnp.dot(stack.astype(BF16), ones_bd, preferred_element_type=F32)
    return [jnp.concatenate([s[(i * N_GROUPS + g) * rows:(i * N_GROUPS + g + 1) * rows]
                             for g in range(N_GROUPS)], axis=-1) for i in range(len(xs))]


def _rwkv_prep(rkv, rkv_prev, wa, wa_prev, mu_rkv, mu_wa, w0, w2p, a0, a2p, k_k, k_a, ones_bd):
    z = rkv + (rkv_prev - rkv) * mu_rkv
    zwa = wa + (wa_prev - wa) * mu_wa
    r = z[:, :RW_C]
    k = z[:, RW_C:2 * RW_C]
    v = z[:, 2 * RW_C:]
    wpre = w0 + _dot(jnp.tanh(zwa), w2p)
    log_decay = -math.exp(-0.5) * _sigmoid(wpre)
    a = _sigmoid(a0 + _dot(zwa, a2p))
    kk = k * k_k
    kk = kk * lax.rsqrt(jnp.maximum(_head_sums([kk * kk], ones_bd)[0], 1e-24))
    k = k * (1.0 + (a - 1.0) * k_a)
    return r, log_decay, k, v, kk, a


def _rwkv_post(o, r, k, v, gb, r_k, ln_w, ln_b, ones_bd):
    o_sum, rk_sum = _head_sums([o, r * k * r_k], ones_bd)
    d = o - o_sum * (1.0 / RW_HEAD)
    var = _head_sums([d * d], ones_bd)[0] * (1.0 / RW_HEAD)
    on = d * lax.rsqrt(var + RW_GN_EPS) * ln_w + ln_b
    return (on + rk_sum * v) * _silu(gb)


def _block_diag(x, head_of_lane):
    xb = x.astype(BF16)
    zero = jnp.zeros_like(xb)
    return jnp.concatenate(
        [jnp.where(head_of_lane == j, xb, zero) for j in range(HEADS_PER_GROUP)], axis=0)


def _rwkv_chunk_kernel(sh_ref, gb_ref, vec_ref, mu_wa_ref, w2_ref, a2_ref, ones_ref,
                       rt_dec_ref, rt_q_ref, rt_k_ref, rt_v_ref, rt_ga_ref, rt_cos_ref, rt_sin_ref, rt_s_ref,
                       y_ref, so_ref, rt_y_ref, rt_so_ref, carry, state):
    _ret_step(rt_dec_ref, rt_q_ref, rt_k_ref, rt_v_ref, rt_ga_ref, rt_cos_ref, rt_sin_ref, rt_s_ref,
              rt_y_ref, rt_so_ref)
    c = pl.program_id(1)
    nc = pl.num_programs(1)
    C = RW_CHUNK
    n_seq = sh_ref.shape[0]

    @pl.when(c == 0)
    def _():
        carry[...] = jnp.zeros_like(carry)
        state[...] = jnp.zeros_like(state)

    ones_bd = ones_ref[...]
    vec = {n: vec_ref[i:i + 1, :] for i, n in enumerate(RW_VEC_ROWS)}
    mu_rkv = jnp.concatenate([vec["mu_r"], vec["mu_k"], vec["mu_v"]], axis=1)
    row = lax.broadcasted_iota(jnp.int32, (C, 1), 0)
    ti = lax.broadcasted_iota(jnp.int32, (C, C), 0)
    tj = lax.broadcasted_iota(jnp.int32, (C, C), 1)
    tri = (ti >= tj).astype(BF16)

    tok = []
    for q in range(n_seq):
        sh = sh_ref[q].astype(F32)
        sh_prev = jnp.where(row == 0, carry[q], pltpu.roll(sh, 1, 0))
        carry[q] = sh[C - 1:C, :]
        rkv, wa = sh[:, :3 * RW_C], sh[:, 3 * RW_C:]
        rkv_prev, wa_prev = sh_prev[:, :3 * RW_C], sh_prev[:, 3 * RW_C:]
        r, lw, k, v, kk, a = _rwkv_prep(rkv, rkv_prev, wa, wa_prev, mu_rkv, mu_wa_ref[...],
                                        vec["w0"], w2_ref[...], vec["a0"], a2_ref[...],
                                        vec["k_k"], vec["k_a"], ones_bd)
        lw_hi = lw.astype(BF16)
        lw_lo = (lw - lw_hi.astype(F32)).astype(BF16)
        cl = (jnp.dot(tri, lw_hi, preferred_element_type=F32)
              + jnp.dot(tri, lw_lo, preferred_element_type=F32))
        cl_last = cl[C - 1:C, :]
        e_neg = jnp.exp(-cl)
        e_rem = jnp.exp(cl_last - cl)
        beta = a * kk
        tok.append(dict(
            r=r, k=k, v=v,
            ag=-kk * jnp.exp(cl - lw), rg=r * jnp.exp(cl), bg=beta * e_neg, kg=k * e_neg,
            bg_c=beta * e_rem, kg_c=k * e_rem,
            g_chunk=jnp.exp(cl_last)))

    lane = lax.broadcasted_iota(jnp.int32, (1, MXU_DIM), 1)
    head_of_lane = lane >> HEAD_SHIFT
    t_col = lax.broadcasted_iota(jnp.int32, (C, MXU_DIM), 0)
    i_lane = lax.broadcasted_iota(jnp.int32, (C, MXU_DIM), 1) & (RW_HEAD - 1)
    strict = t_col > i_lane
    incl = t_col >= i_lane
    eye = (t_col == i_lane).astype(F32)
    vrow_head = lax.broadcasted_iota(jnp.int32, (MXU_DIM, MXU_DIM), 0) >> HEAD_SHIFT
    klane_head = lax.broadcasted_iota(jnp.int32, (MXU_DIM, MXU_DIM), 1) >> HEAD_SHIFT
    same_head = vrow_head == klane_head

    units = [(q, g) for q in range(n_seq) for g in range(N_GROUPS)]
    U = range(len(units))
    bd = lambda x: _block_diag(x, head_of_lane)
    grp = lambda name, u: tok[units[u][0]][name][:, units[u][1] * MXU_DIM:(units[u][1] + 1) * MXU_DIM]
    s_bd = [state[q, g] for q, g in units]
    lhs = [jnp.concatenate([grp("ag", u), grp("rg", u)], axis=0) for u in U]
    ab = [_dot_nt(lhs[u], bd(grp("bg", u))) for u in U]
    ak = [_dot_nt(lhs[u], bd(grp("kg", u))) for u in U]
    n_pow = [jnp.where(strict, ab[u][:C], 0.0) for u in U]
    a_ak = [jnp.where(strict, ak[u][:C], 0.0) for u in U]
    a_rb = [jnp.where(incl, ab[u][C:], 0.0) for u in U]
    a_rk = [jnp.where(incl, ak[u][C:], 0.0) for u in U]
    v_bd = [bd(grp("v", u)) for u in U]
    sv = [_dot_nt(lhs[u], s_bd[u]) for u in U]
    av = [_dot(jnp.concatenate([a_ak[u], a_rk[u]], axis=0), v_bd[u]) for u in U]
    t_inv = [eye + n_pow[u] for u in U]
    n_pow = [_dot(n_pow[u], bd(n_pow[u])) for u in U]
    for _ in range(int(math.log2(C)) - 2):
        prod = [_dot(jnp.concatenate([n_pow[u], t_inv[u]], axis=0), bd(n_pow[u])) for u in U]
        n_pow = [prod[u][:C] for u in U]
        t_inv = [t_inv[u] + prod[u][C:] for u in U]
    t_inv = [t_inv[u] + _dot(t_inv[u], bd(n_pow[u])) for u in U]
    p = [_dot(t_inv[u], bd(sv[u][:C] + av[u][:C])) for u in U]
    o = [sv[u][C:] + _dot(a_rb[u], bd(p[u])) + av[u][C:] for u in U]
    for u, (q, g) in enumerate(units):
        upd = _dot_tn(jnp.concatenate([p[u], grp("v", u)], axis=0),
                      jnp.concatenate([grp("bg_c", u), grp("kg_c", u)], axis=0))
        state[q, g] = s_bd[u] * grp("g_chunk", u) + jnp.where(same_head, upd, 0.0)
    for q in range(n_seq):
        o_q = jnp.concatenate(o[q * N_GROUPS:(q + 1) * N_GROUPS], axis=-1)
        t = tok[q]
        y = _rwkv_post(o_q, t["r"], t["k"], t["v"], gb_ref[q].astype(F32), vec["r_k"], vec["ln_w"],
                       vec["ln_b"], ones_bd)
        y_ref[q] = y.astype(y_ref.dtype)

    @pl.when(c == nc - 1)
    def _():
        for q in range(n_seq):
            for g in range(N_GROUPS):
                for j in range(HEADS_PER_GROUP):
                    blk = slice(j * RW_HEAD, (j + 1) * RW_HEAD)
                    so_ref[q, g * HEADS_PER_GROUP + j] = state[q, g, blk, blk]


RW_VEC_ROWS = ("mu_r", "mu_k", "mu_v", "w0", "a0", "k_k", "k_a", "r_k", "ln_w", "ln_b")


def _rwkv_consts(p):
    mu = p["rw_mu"]
    vec = jnp.stack([mu[:RW_C], mu[RW_C:2 * RW_C], mu[2 * RW_C:3 * RW_C], p["rw_w0"], p["rw_a0"],
                     p["rw_k_k"], p["rw_k_a"], p["rw_r_k"].reshape(-1), p["rw_ln_w"],
                     p["rw_ln_b"]]).astype(F32)
    zeros = jnp.zeros((RW_LORA, RW_C), F32)
    w2p = jnp.concatenate([p["rw_w2"], zeros], axis=0).astype(BF16)
    a2p = jnp.concatenate([zeros, p["rw_a2"]], axis=0).astype(BF16)
    hl = jnp.arange(MXU_DIM) // RW_HEAD
    ones_bd = (hl[:, None] == hl[None, :]).astype(BF16)
    return dict(vec=vec, mu_wa=mu[3 * RW_C:].reshape(1, -1).astype(F32), w2p=w2p, a2p=a2p,
                ones_bd=ones_bd)


def _const_spec(arr, ngrid):
    zeros = (0,) * arr.ndim
    if ngrid == 1:
        return pl.BlockSpec(arr.shape, lambda i: zeros, pipeline_mode=pl.Buffered(1))
    return pl.BlockSpec(arr.shape, lambda i, j: zeros, pipeline_mode=pl.Buffered(1))


def _rwkv_prompt(sh, gb, cs, batch, seq, ret_sample):
    C = RW_CHUNK
    nq = RW_SEQS_PER_STEP
    nc = seq // C
    n_sh = 3 * RW_C + 2 * RW_LORA
    blk = lambda n: pl.BlockSpec((nq, C, n), lambda b, c: (b, c, 0))
    consts = [cs[n] for n in ("vec", "mu_wa", "w2p", "a2p", "ones_bd")]
    rt_args, rt_in, rt_out, rt_shape = _ret_step_operands(
        *ret_sample, n_steps=(batch // nq) * nc, step_index=lambda b, c: b * nc + c)
    y, s, rt_y, rt_s = pl.pallas_call(
        _rwkv_chunk_kernel,
        grid=(batch // nq, nc),
        in_specs=[blk(n_sh), blk(RW_C)] + [_const_spec(a, 2) for a in consts] + rt_in,
        out_specs=[blk(RW_C),
                   pl.BlockSpec((nq, RW_HEADS, RW_HEAD, RW_HEAD), lambda b, c: (b, 0, 0, 0))] + rt_out,
        out_shape=[jax.ShapeDtypeStruct((batch, seq, RW_C), BF16),
                   jax.ShapeDtypeStruct((batch, RW_HEADS, RW_HEAD, RW_HEAD), F32)] + rt_shape,
        scratch_shapes=[pltpu.VMEM((nq, 1, n_sh), F32),
                        pltpu.VMEM((nq, N_GROUPS, MXU_DIM, MXU_DIM), F32)],
        compiler_params=_params("arbitrary", "arbitrary"),
        name="rwkv_chunk",
    )(sh.reshape(batch, seq, n_sh), gb.reshape(batch, seq, RW_C), *consts, *rt_args)
    return y.reshape(batch * seq, RW_C), s, rt_y.reshape(rt_y.shape[0], RET_V), rt_s


def _rwkv_step_kernel(r_ref, k_ref, v_ref, wa_ref, gb_ref, s_ref, col_ref, mu_wa_ref, w2t_ref, a2t_ref,
                      y_ref, so_ref, o_scr):
    nb = y_ref.shape[-1]
    col = {n: col_ref[:, i:i + 1] for i, n in enumerate(RW_VEC_ROWS)}
    lerp = lambda ref, mu: ref[:, :nb] + (ref[:, nb:2 * nb] - ref[:, :nb]) * mu
    r = lerp(r_ref, col["mu_r"])
    k = lerp(k_ref, col["mu_k"])
    v = lerp(v_ref, col["mu_v"])
    zwa = lerp(wa_ref, mu_wa_ref[...])
    wpre = col["w0"] + _dot(w2t_ref[...], jnp.tanh(zwa[:RW_LORA]))
    decay = jnp.exp(-math.exp(-0.5) * _sigmoid(wpre))
    a = _sigmoid(col["a0"] + _dot(a2t_ref[...], zwa[RW_LORA:]))
    kk = k * col["k_k"]
    kk = kk * lax.rsqrt(jnp.maximum(jnp.sum(kk * kk, axis=0, keepdims=True), 1e-24))
    k = k * (1.0 + (a - 1.0) * col["k_a"])
    beta = a * kk
    for i in range(RW_HEAD):
        s = s_ref[i]
        sk = jnp.sum(s * kk, axis=0, keepdims=True)
        s_new = s * decay - sk * beta + v[i:i + 1, :] * k
        so_ref[i] = s_new
        o_scr[i:i + 1, :] = jnp.sum(s_new * r, axis=0, keepdims=True)
    o = o_scr[...]
    d = o - jnp.mean(o, axis=0, keepdims=True)
    var = jnp.mean(d * d, axis=0, keepdims=True)
    on = d * lax.rsqrt(var + RW_GN_EPS) * col["ln_w"] + col["ln_b"]
    bonus = jnp.sum(r * k * col["r_k"], axis=0, keepdims=True) * v
    y_ref[...] = ((on + bonus) * _silu(gb_ref[:, :nb])).astype(y_ref.dtype)


def _rwkv_sample(sh2, gb2, p, cs, state):
    nb = state.shape[1]
    sht = sh2.T
    gbt = gb2.T
    st = jnp.transpose(state, (0, 2, 3, 4, 1))
    n_head_blocks = RW_C // RW_HEAD
    rows = lambda off: pl.BlockSpec((RW_HEAD, 2 * nb), lambda h: (h + off, 0))
    lora_blk = pl.BlockSpec((2 * RW_LORA, 2 * nb), lambda h: (3 * RW_C // (2 * RW_LORA), 0))
    st_blk = pl.BlockSpec((None, None, RW_HEAD, RW_HEAD, nb), lambda h: (0, h, 0, 0, 0))
    wt_blk = pl.BlockSpec((RW_HEAD, RW_LORA), lambda h: (h, 0))
    yt, so = pl.pallas_call(
        _rwkv_step_kernel,
        grid=(RW_HEADS,),
        in_specs=[rows(0), rows(n_head_blocks), rows(2 * n_head_blocks), lora_blk, rows(0), st_blk,
                  pl.BlockSpec((RW_HEAD, len(RW_VEC_ROWS)), lambda h: (h, 0)),
                  pl.BlockSpec((2 * RW_LORA, 1), lambda h: (0, 0)), wt_blk, wt_blk],
        out_specs=[pl.BlockSpec((RW_HEAD, nb), lambda h: (h, 0)), st_blk],
        out_shape=[jax.ShapeDtypeStruct((RW_C, nb), BF16), jax.ShapeDtypeStruct(st.shape, F32)],
        scratch_shapes=[pltpu.VMEM((RW_HEAD, nb), F32)],
        compiler_params=_params("arbitrary"),
        name="rwkv_step",
    )(sht, sht, sht, sht, gbt, st, cs["vec"].T, cs["mu_wa"].reshape(-1, 1),
      p["rw_w2"].T.astype(BF16), p["rw_a2"].T.astype(BF16))
    return yt.T, jnp.transpose(so, (0, 4, 1, 2, 3))


def _tail_kernel(ya_ref, yb_ref, m_ref, x_ref, p_ref, wda_ref, wdb_ref, wout_ref, wple_ref, wgate_ref,
                 pg_ref, fg_ref, y_ref):
    m = m_ref[...].astype(F32)
    merged = (_sigmoid(m[:, :D_MODEL]) * jnp.dot(ya_ref[...], wda_ref[...], preferred_element_type=F32)
              + _sigmoid(m[:, D_MODEL:]) * jnp.dot(yb_ref[...], wdb_ref[...], preferred_element_type=F32))
    x = x_ref[...] + _dot(merged, wout_ref[...])
    gate = _sigmoid(_dot(_rms(x, pg_ref[...]), wgate_ref[...]))
    x = x + _dot(p_ref[...], wple_ref[...]) * gate
    y_ref[...] = _rms(x, fg_ref[...])


def _tail(ya, yb, m, x, p, w, tm):
    rows = x.shape[0]
    tile = lambda n: pl.BlockSpec((tm, n), lambda i: (i, 0))
    consts = [w["wda"], w["wdb"], w["wout"], w["wple"], w["wgate"], w["ple_g"], w["final_g"]]
    return pl.pallas_call(
        _tail_kernel,
        grid=(rows // tm,),
        in_specs=[tile(RET_V), tile(RW_C), tile(2 * D_MODEL), tile(D_MODEL), tile(PLE_DIM)]
        + [_const_spec(a, 1) for a in consts],
        out_specs=tile(D_MODEL),
        out_shape=jax.ShapeDtypeStruct((rows, D_MODEL), F32),
        compiler_params=_params("arbitrary"),
        name="tail",
    )(ya, yb, m, x, p, *consts)


def _layer_weights(p):
    return dict(
        w_in=p["w_in"].astype(BF16),
        wda=p["w_down_a"].astype(BF16), wdb=p["w_down_b"].astype(BF16), wout=p["w_out"].astype(BF16),
        wple=p["w_ple"].astype(BF16), wgate=p["w_ple_gate"].astype(BF16),
        ple_g=p["ple_norm_g"].reshape(1, -1), final_g=p["final_norm_g"].reshape(1, -1),
    )


def _layer_paths(x_p, pe_p, x_s, h_prev, s_ret, s_rw, pe_s, p, w, cs):
    batch, seq, d = x_p.shape
    rows = batch * seq
    nb = x_s.shape[0]
    xp2 = x_p.reshape(rows, d)
    xs2 = x_s.reshape(nb, d)
    shift_p = _rmsnorm(x_p[:, -1, :], p["norm_g"], F32, batch)
    h_s = _rmsnorm(xs2, p["norm_g"], F32, nb)
    hcat = jnp.concatenate([h_s, h_prev], axis=0)
    qk_s, v_s, ga_s, sh_s, gb_s, m_s = _in_proj(hcat, p["norm_g"], w["w_in"], False, F32, 2 * nb)
    ya_p, sh, gb, m, ret_p = _in_proj_retention(xp2, p["norm_g"], w["w_in"], batch, seq, PROMPT_PROJ_ROWS)
    yb_p, rw_p, ya_s, ret_s = _rwkv_prompt(sh, gb, cs, batch, seq, (qk_s, v_s, ga_s, s_ret))
    y_p = _tail(ya_p, yb_p, m, xp2, pe_p.reshape(rows, PLE_DIM), w, PROMPT_TAIL_ROWS)
    yb_s, rw_s = _rwkv_sample(sh_s, gb_s, p, cs, s_rw)
    y_s = _tail(ya_s, yb_s, m_s, xs2, pe_s.reshape(nb, PLE_DIM), w, nb)
    return (y_p.reshape(batch, seq, d), shift_p, ret_p, rw_p,
            y_s.reshape(nb, 1, d), h_s, ret_s, rw_s)


def kernel(x_prompt, x_sample, state_ret, state_rwkv, state_shift, p_prompt, p_sample, norm_g, w_in, rw_mu, rw_w0, rw_w2, rw_a0, rw_a2, rw_k_k, rw_k_a, rw_r_k, rw_ln_w, rw_ln_b, w_down_a, w_down_b, w_out, w_ple, ple_norm_g, w_ple_gate, final_norm_g):
    assert norm_g.shape[0] == 1, "single-layer step"
    p = dict(norm_g=norm_g[0], w_in=w_in[0], rw_mu=rw_mu[0], rw_w0=rw_w0[0], rw_w2=rw_w2[0],
             rw_a0=rw_a0[0], rw_a2=rw_a2[0], rw_k_k=rw_k_k[0], rw_k_a=rw_k_a[0], rw_r_k=rw_r_k[0],
             rw_ln_w=rw_ln_w[0], rw_ln_b=rw_ln_b[0], w_down_a=w_down_a[0], w_down_b=w_down_b[0],
             w_out=w_out[0], w_ple=w_ple[0], ple_norm_g=ple_norm_g[0], w_ple_gate=w_ple_gate[0],
             final_norm_g=final_norm_g)
    w = _layer_weights(p)
    cs = _rwkv_consts(p)
    y_p, sh_p, ret_p, rw_p, y_s, sh_s, ret_s, rw_s = _layer_paths(
        x_prompt, p_prompt[0], x_sample, state_shift[0], state_ret[0], state_rwkv, p_sample[0], p, w, cs)
    return (y_p, y_s, ret_p[None], rw_p[None], sh_p[None], ret_s[None], rw_s, sh_s[None])
```

```python
import functools
import math

import jax
import jax.numpy as jnp
from jax import lax
from jax.experimental import pallas as pl
from jax.experimental.pallas import tpu as pltpu

F32 = jnp.float32
BF16 = jnp.bfloat16

D_MODEL = 1024
RET_HEADS = 4
RET_DK = 256
RET_DV = 512
RET_QK = RET_HEADS * RET_DK
RET_V = RET_HEADS * RET_DV
ROPE_BASE = 10000.0
RW_HEAD = 64
RW_HEADS = D_MODEL // RW_HEAD
RW_C = RW_HEADS * RW_HEAD
RW_LORA = 64
RW_GN_EPS = 1e-5 * RW_HEAD
RW_CHUNK = 64
PLE_DIM = 256
NORM_EPS = 1e-6
PAST_LEN = 16384

LANES = 128
MXU_DIM = 256
HEADS_PER_GROUP = MXU_DIM // RW_HEAD
N_GROUPS = RW_C // MXU_DIM
HEAD_SHIFT = RW_HEAD.bit_length() - 1
VMEM_LIMIT_BYTES = 56 * 1024 * 1024
PROMPT_PROJ_ROWS = 256
PROMPT_TAIL_ROWS = 512
RW_SEQS_PER_STEP = 4


def _params(*sem):
    return pltpu.CompilerParams(dimension_semantics=sem, vmem_limit_bytes=VMEM_LIMIT_BYTES)


def _dot(a, b):
    return jnp.dot(a.astype(BF16), b.astype(BF16), preferred_element_type=F32)


def _dot_nt(a, b):
    return lax.dot_general(a.astype(BF16), b.astype(BF16), (((1,), (1,)), ((), ())),
                           preferred_element_type=F32)


def _dot_tn(a, b):
    return lax.dot_general(a.astype(BF16), b.astype(BF16), (((0,), (0,)), ((), ())),
                           preferred_element_type=F32)


def _sigmoid(x):
    return 0.5 * jnp.tanh(0.5 * x) + 0.5


def _silu(x):
    return x * _sigmoid(x)


def _rms(x, g):
    return x * lax.rsqrt(jnp.mean(x * x, axis=-1, keepdims=True) + NORM_EPS) * g


def _rmsnorm_kernel(x_ref, g_ref, o_ref):
    o_ref[...] = _rms(x_ref[...], g_ref[...]).astype(o_ref.dtype)


def _rmsnorm(x, g, out_dtype, tm):
    m, d = x.shape
    return pl.pallas_call(
        _rmsnorm_kernel,
        grid=(m // tm,),
        in_specs=[pl.BlockSpec((tm, d), lambda i: (i, 0)), pl.BlockSpec((1, d), lambda i: (0, 0))],
        out_specs=pl.BlockSpec((tm, d), lambda i: (i, 0)),
        out_shape=jax.ShapeDtypeStruct((m, d), out_dtype),
        compiler_params=_params("arbitrary"),
        name="rmsnorm",
    )(x, g.reshape(1, d))


PROJ_WIDTHS = (2 * RET_QK, RET_V, RET_V, 3 * RW_C + 2 * RW_LORA, RW_C, 2 * D_MODEL)


def _in_proj_kernel(x_ref, g_ref, w_ref, *out_refs, normalize):
    x = x_ref[...]
    h = (_rms(x, g_ref[...]) if normalize else x).astype(BF16)
    off = 0
    for o_ref, n in zip(out_refs, PROJ_WIDTHS):
        o_ref[...] = jnp.dot(h, w_ref[:, off:off + n], preferred_element_type=F32).astype(o_ref.dtype)
        off += n


def _in_proj(x, g, w_in, normalize, out_dtype, tm):
    m, d = x.shape
    n_all = w_in.shape[1]
    return pl.pallas_call(
        functools.partial(_in_proj_kernel, normalize=normalize),
        grid=(m // tm,),
        in_specs=[pl.BlockSpec((tm, d), lambda i: (i, 0)), pl.BlockSpec((1, d), lambda i: (0, 0)),
                  pl.BlockSpec((d, n_all), lambda i: (0, 0), pipeline_mode=pl.Buffered(1))],
        out_specs=[pl.BlockSpec((tm, n), lambda i: (i, 0)) for n in PROJ_WIDTHS],
        out_shape=[jax.ShapeDtypeStruct((m, n), out_dtype) for n in PROJ_WIDTHS],
        compiler_params=_params("arbitrary"),
        name="in_proj",
    )(x, g.reshape(1, d), w_in)


def _in_proj_ret_kernel(dec_ref, x_ref, g_ref, w_ref, cos_ref, sin_ref, mu_ref,
                        ya_ref, sh_ref, gb_ref, m_ref, s_ref, carry):
    @pl.when(pl.program_id(1) == 0)
    def _():
        s_ref[...] = jnp.zeros_like(s_ref)
        carry[...] = jnp.zeros_like(carry)

    tm = x_ref.shape[0]
    h = _rms(x_ref[...], g_ref[...]).astype(BF16)
    o_qk, o_v, o_ga, o_sh, o_gb, o_m = [sum(PROJ_WIDTHS[:i]) for i in range(len(PROJ_WIDTHS))]
    proj = lambda off, n: jnp.dot(h, w_ref[:, off:off + n], preferred_element_type=F32)
    qk = proj(o_qk, 2 * RET_QK)
    v = proj(o_v, RET_V)
    ga = proj(o_ga, RET_V)
    sh = proj(o_sh, PROJ_WIDTHS[3])
    row = lax.broadcasted_iota(jnp.int32, (tm, 1), 0)
    sh_prev = jnp.where(row == 0, carry[...], pltpu.roll(sh, 1, 0))
    carry[...] = sh[tm - 1:tm, :]
    sh_ref[...] = (sh + (sh_prev - sh) * mu_ref[...]).astype(sh_ref.dtype)
    gb_ref[...] = proj(o_gb, PROJ_WIDTHS[4]).astype(gb_ref.dtype)
    m_ref[...] = proj(o_m, PROJ_WIDTHS[5]).astype(m_ref.dtype)
    _ret_chunk(qk[:, :RET_QK], qk[:, RET_QK:], v, ga, cos_ref[...], sin_ref[...], dec_ref, s_ref, ya_ref)


def _in_proj_retention(x, g, w_in, mu, batch, seq, tm):
    m, d = x.shape
    nt = seq // tm
    n_all = w_in.shape[1]
    log_g = _ret_decay_table()
    dec = jnp.stack([log_g, jnp.exp(tm * log_g)], axis=1).reshape(-1)
    half = RET_DK // 2
    cos, sin = _rope_tables(jnp.arange(seq, dtype=F32))
    row = lambda b, t: (b * nt + t, 0)
    widths = (RET_V,) + PROJ_WIDTHS[3:]
    return pl.pallas_call(
        _in_proj_ret_kernel,
        grid=(batch, nt),
        in_specs=[pl.BlockSpec(memory_space=pltpu.SMEM),
                  pl.BlockSpec((tm, d), row), pl.BlockSpec((1, d), lambda b, t: (0, 0)),
                  pl.BlockSpec((d, n_all), lambda b, t: (0, 0), pipeline_mode=pl.Buffered(1)),
                  pl.BlockSpec((tm, half), lambda b, t: (t, 0)),
                  pl.BlockSpec((tm, half), lambda b, t: (t, 0)), _const_spec(mu, 2)],
        out_specs=[pl.BlockSpec((tm, n), row) for n in widths]
        + [pl.BlockSpec((None, RET_HEADS, RET_DK, RET_DV), lambda b, t: (b, 0, 0, 0))],
        out_shape=[jax.ShapeDtypeStruct((m, n), BF16) for n in widths]
        + [jax.ShapeDtypeStruct((batch, RET_HEADS, RET_DK, RET_DV), F32)],
        scratch_shapes=[pltpu.VMEM((1, PROJ_WIDTHS[3]), F32)],
        compiler_params=_params("arbitrary", "arbitrary"),
        name="in_proj_retention",
    )(dec, x, g.reshape(1, d), w_in, cos, sin, mu)


def _rope(x, cos, sin):
    half = x.shape[-1] // 2
    x1, x2 = x[:, :half], x[:, half:]
    return jnp.concatenate([x1 * cos - x2 * sin, x2 * cos + x1 * sin], axis=-1)


def _rope_tables(pos):
    half = RET_DK // 2
    inv = ROPE_BASE ** (-jnp.arange(half, dtype=F32) / half)
    ang = pos[:, None] * inv[None, :]
    return jnp.cos(ang), jnp.sin(ang)


def _ret_chunk(q, k, v, ga, cos, sin, dec_ref, s_ref, y_ref):
    C = q.shape[0]
    H = range(RET_HEADS)
    ti = lax.broadcasted_iota(jnp.int32, (C, C), 0)
    tj = lax.broadcasted_iota(jnp.int32, (C, C), 1)
    rel = (ti - tj).astype(F32)
    idx = lax.broadcasted_iota(jnp.int32, (C, 1), 0).astype(F32)
    lg = [dec_ref[2 * h] for h in H]
    qh = [_rope(q[:, h * RET_DK:(h + 1) * RET_DK], cos, sin) for h in H]
    kh = [_rope(k[:, h * RET_DK:(h + 1) * RET_DK], cos, sin) * (RET_DK ** -0.5) for h in H]
    vh = [v[:, h * RET_DV:(h + 1) * RET_DV].astype(BF16) for h in H]
    s = [s_ref[h] for h in H]
    inner = [_dot_nt(qh[h], kh[h]) * jnp.where(rel >= 0, jnp.exp(jnp.maximum(rel, 0.0) * lg[h]), 0.0)
             for h in H]
    o = [_dot(inner[h], vh[h]) + _dot(qh[h] * jnp.exp((idx + 1.0) * lg[h]), s[h]) for h in H]
    for h in H:
        s_ref[h] = dec_ref[2 * h + 1] * s[h] + _dot_tn(kh[h] * jnp.exp((C - 1.0 - idx) * lg[h]), vh[h])
    for h in H:
        on = o[h] * lax.rsqrt(jnp.mean(o[h] * o[h], axis=-1, keepdims=True) + NORM_EPS)
        y_ref[:, h * RET_DV:(h + 1) * RET_DV] = (
            on * _silu(ga[:, h * RET_DV:(h + 1) * RET_DV])).astype(y_ref.dtype)


def _ret_decay_table():
    log_g = jnp.log(1.0 - jnp.exp2(-5.0 - jnp.arange(RET_HEADS, dtype=F32)))
    return log_g


def _ret_step(dec_ref, q_ref, k_ref, v_ref, ga_ref, cos_ref, sin_ref, s_ref, y_ref, so_ref):
    row0 = lax.broadcasted_iota(jnp.int32, (8, 1), 0)
    cos = cos_ref[...]
    sin = sin_ref[...]
    for b, h in [(b, h) for b in range(q_ref.shape[0]) for h in range(RET_HEADS)]:
        g = dec_ref[h]
        q = _rope(q_ref[b, :, h * RET_DK:(h + 1) * RET_DK], cos, sin)
        k = _rope(k_ref[b, :, h * RET_DK:(h + 1) * RET_DK], cos, sin) * (RET_DK ** -0.5)
        v = v_ref[b, :, h * RET_DV:(h + 1) * RET_DV]
        s = s_ref[b, h]
        qk = jnp.sum(q * k, axis=-1, keepdims=True)
        q8 = jnp.broadcast_to(q, (8, RET_DK))
        o = qk * v + g * _dot(q8, s)[0:1, :]
        k_hi = k.astype(BF16).astype(F32)
        k_lo = k - k_hi
        v_hi = v.astype(BF16).astype(F32)
        v_lo = v - v_hi
        k8 = jnp.where(row0 < 2, k_hi, jnp.where(row0 == 2, k_lo, 0.0))
        v8 = jnp.where((row0 == 0) | (row0 == 2), v_hi, jnp.where(row0 == 1, v_lo, 0.0))
        so_ref[b, h] = g * s + _dot_tn(k8, v8)
        o = o * lax.rsqrt(jnp.mean(o * o, axis=-1, keepdims=True) + NORM_EPS)
        y_ref[b, :, h * RET_DV:(h + 1) * RET_DV] = (
            o * _silu(ga_ref[b, :, h * RET_DV:(h + 1) * RET_DV])).astype(y_ref.dtype)


def _ret_step_operands(qk, v, ga, state, n_steps, step_index):
    nb = state.shape[0]
    nq = nb // n_steps
    assert nq * n_steps == nb
    g = jnp.exp(_ret_decay_table())
    cos, sin = _rope_tables(PAST_LEN + jnp.arange(1, dtype=F32))
    r3 = lambda t: t.reshape(t.shape[0], 1, t.shape[-1])
    tab = pl.BlockSpec((1, RET_DK // 2), lambda *ids: (0, 0))
    vec = lambda n, col=0: pl.BlockSpec((nq, 1, n), lambda *ids: (step_index(*ids), 0, col))
    st = pl.BlockSpec((nq, RET_HEADS, RET_DK, RET_DV), lambda *ids: (step_index(*ids), 0, 0, 0))
    args = [g, r3(qk), r3(qk), r3(v), r3(ga), cos, sin, state]
    in_specs = [pl.BlockSpec(memory_space=pltpu.SMEM), vec(RET_QK, 0), vec(RET_QK, 1), vec(RET_V),
                vec(RET_V), tab, tab, st]
    out_specs = [vec(RET_V), st]
    out_shape = [jax.ShapeDtypeStruct((nb, 1, RET_V), BF16), jax.ShapeDtypeStruct(state.shape, F32)]
    return args, in_specs, out_specs, out_shape


def _head_sums(xs, ones_bd):
    rows = xs[0].shape[0]
    stack = jnp.concatenate(
        [x[:, g * MXU_DIM:(g + 1) * MXU_DIM] for x in xs for g in range(N_GROUPS)], axis=0)
    s = jnp.dot(stack.astype(BF16), ones_bd, preferred_element_type=F32)
    return [jnp.concatenate([s[(i * N_GROUPS + g) * rows:(i * N_GROUPS + g + 1) * rows]
                             for g in range(N_GROUPS)], axis=-1) for i in range(len(xs))]


def _rwkv_prep(z, zwa, w0, w2p, a0, a2p, k_k, k_a, ones_bd):
    r = z[:, :RW_C]
    k = z[:, RW_C:2 * RW_C]
    v = z[:, 2 * RW_C:]
    wpre = w0 + _dot(jnp.tanh(zwa), w2p)
    log_decay = -math.exp(-0.5) * _sigmoid(wpre)
    a = _sigmoid(a0 + _dot(zwa, a2p))
    kk = k * k_k
    kk = kk * lax.rsqrt(jnp.maximum(_head_sums([kk * kk], ones_bd)[0], 1e-24))
    k = k * (1.0 + (a - 1.0) * k_a)
    return r, log_decay, k, v, kk, a


def _rwkv_post(o, r, k, v, gb, r_k, ln_w, ln_b, ones_bd):
    o_sum, rk_sum = _head_sums([o, r * k * r_k], ones_bd)
    d = o - o_sum * (1.0 / RW_HEAD)
    var = _head_sums([d * d], ones_bd)[0] * (1.0 / RW_HEAD)
    on = d * lax.rsqrt(var + RW_GN_EPS) * ln_w + ln_b
    return (on + rk_sum * v) * _silu(gb)


def _block_diag(x, head_of_lane):
    xb = x.astype(BF16)
    zero = jnp.zeros_like(xb)
    return jnp.concatenate(
        [jnp.where(head_of_lane == j, xb, zero) for j in range(HEADS_PER_GROUP)], axis=0)


def _rwkv_chunk_kernel(sh_ref, gb_ref, vec_ref, w2_ref, a2_ref, ones_ref,
                       rt_dec_ref, rt_q_ref, rt_k_ref, rt_v_ref, rt_ga_ref, rt_cos_ref, rt_sin_ref, rt_s_ref,
                       y_ref, so_ref, rt_y_ref, rt_so_ref, state):
    _ret_step(rt_dec_ref, rt_q_ref, rt_k_ref, rt_v_ref, rt_ga_ref, rt_cos_ref, rt_sin_ref, rt_s_ref,
              rt_y_ref, rt_so_ref)
    c = pl.program_id(1)
    nc = pl.num_programs(1)
    C = RW_CHUNK
    n_seq = sh_ref.shape[0]

    @pl.when(c == 0)
    def _():
        state[...] = jnp.zeros_like(state)

    ones_bd = ones_ref[...]
    vec = {n: vec_ref[i:i + 1, :] for i, n in enumerate(RW_VEC_ROWS)}
    ti = lax.broadcasted_iota(jnp.int32, (C, C), 0)
    tj = lax.broadcasted_iota(jnp.int32, (C, C), 1)
    tri = (ti >= tj).astype(BF16)

    tok = []
    for q in range(n_seq):
        z = sh_ref[q].astype(F32)
        r, lw, k, v, kk, a = _rwkv_prep(z[:, :3 * RW_C], z[:, 3 * RW_C:], vec["w0"], w2_ref[...],
                                        vec["a0"], a2_ref[...], vec["k_k"], vec["k_a"], ones_bd)
        lw_hi = lw.astype(BF16)
        lw_lo = (lw - lw_hi.astype(F32)).astype(BF16)
        cl = (jnp.dot(tri, lw_hi, preferred_element_type=F32)
              + jnp.dot(tri, lw_lo, preferred_element_type=F32))
        cl_last = cl[C - 1:C, :]
        e_neg = jnp.exp(-cl)
        g_chunk = jnp.exp(cl_last)
        e_rem = e_neg * g_chunk
        beta = a * kk
        b16 = lambda x: x.astype(BF16)
        tok.append(dict(
            r=r, k=k, v=v,
            ag=b16(-kk * jnp.exp(cl - lw)), rg=b16(r * jnp.exp(cl)), bg=b16(beta * e_neg),
            kg=b16(k * e_neg), bg_c=b16(beta * e_rem), kg_c=b16(k * e_rem), g_chunk=g_chunk))

    lane = lax.broadcasted_iota(jnp.int32, (1, MXU_DIM), 1)
    head_of_lane = lane >> HEAD_SHIFT
    t_col = lax.broadcasted_iota(jnp.int32, (C, MXU_DIM), 0)
    i_lane = lax.broadcasted_iota(jnp.int32, (C, MXU_DIM), 1) & (RW_HEAD - 1)
    strict = t_col > i_lane
    incl = t_col >= i_lane
    eye = (t_col == i_lane).astype(F32)
    vrow_head = lax.broadcasted_iota(jnp.int32, (MXU_DIM, MXU_DIM), 0) >> HEAD_SHIFT
    klane_head = lax.broadcasted_iota(jnp.int32, (MXU_DIM, MXU_DIM), 1) >> HEAD_SHIFT
    same_head = vrow_head == klane_head

    units = [(q, g) for q in range(n_seq) for g in range(N_GROUPS)]
    U = range(len(units))
    bd = lambda x: _block_diag(x, head_of_lane)
    grp = lambda name, u: tok[units[u][0]][name][:, units[u][1] * MXU_DIM:(units[u][1] + 1) * MXU_DIM]
    s_bd = [state[q, g] for q, g in units]
    lhs = [jnp.concatenate([grp("ag", u), grp("rg", u)], axis=0) for u in U]
    ab = [_dot_nt(lhs[u], bd(grp("bg", u))) for u in U]
    ak = [_dot_nt(lhs[u], bd(grp("kg", u))) for u in U]
    n_pow = [jnp.where(strict, ab[u][:C], 0.0) for u in U]
    a_ak = [jnp.where(strict, ak[u][:C], 0.0) for u in U]
    a_rb = [jnp.where(incl, ab[u][C:], 0.0) for u in U]
    a_rk = [jnp.where(incl, ak[u][C:], 0.0) for u in U]
    v_bd = [bd(grp("v", u)) for u in U]
    sv = [_dot_nt(lhs[u], s_bd[u]) for u in U]
    av = [_dot(jnp.concatenate([a_ak[u], a_rk[u]], axis=0), v_bd[u]) for u in U]
    t_inv = [eye + n_pow[u] for u in U]
    n_pow = [_dot(n_pow[u], bd(n_pow[u])) for u in U]
    for _ in range(int(math.log2(C)) - 2):
        prod = [_dot(jnp.concatenate([n_pow[u], t_inv[u]], axis=0), bd(n_pow[u])) for u in U]
        n_pow = [prod[u][:C] for u in U]
        t_inv = [t_inv[u] + prod[u][C:] for u in U]
    t_inv = [t_inv[u] + _dot(t_inv[u], bd(n_pow[u])) for u in U]
    p = [_dot(t_inv[u], bd(sv[u][:C] + av[u][:C])) for u in U]
    o = [sv[u][C:] + _dot(a_rb[u], bd(p[u])) + av[u][C:] for u in U]
    for u, (q, g) in enumerate(units):
        upd = _dot_tn(jnp.concatenate([p[u], grp("v", u)], axis=0),
                      jnp.concatenate([grp("bg_c", u), grp("kg_c", u)], axis=0))
        state[q, g] = s_bd[u] * grp("g_chunk", u) + jnp.where(same_head, upd, 0.0)
    for q in range(n_seq):
        o_q = jnp.concatenate(o[q * N_GROUPS:(q + 1) * N_GROUPS], axis=-1)
        t = tok[q]
        y = _rwkv_post(o_q, t["r"], t["k"], t["v"], gb_ref[q].astype(F32), vec["r_k"], vec["ln_w"],
                       vec["ln_b"], ones_bd)
        y_ref[q] = y.astype(y_ref.dtype)

    @pl.when(c == nc - 1)
    def _():
        for q in range(n_seq):
            for g in range(N_GROUPS):
                for j in range(HEADS_PER_GROUP):
                    blk = slice(j * RW_HEAD, (j + 1) * RW_HEAD)
                    so_ref[q, g * HEADS_PER_GROUP + j] = state[q, g, blk, blk]


RW_VEC_ROWS = ("mu_r", "mu_k", "mu_v", "w0", "a0", "k_k", "k_a", "r_k", "ln_w", "ln_b")


def _rwkv_consts(p):
    mu = p["rw_mu"]
    vec = jnp.stack([mu[:RW_C], mu[RW_C:2 * RW_C], mu[2 * RW_C:3 * RW_C], p["rw_w0"], p["rw_a0"],
                     p["rw_k_k"], p["rw_k_a"], p["rw_r_k"].reshape(-1), p["rw_ln_w"],
                     p["rw_ln_b"]]).astype(F32)
    zeros = jnp.zeros((RW_LORA, RW_C), F32)
    w2p = jnp.concatenate([p["rw_w2"], zeros], axis=0).astype(BF16)
    a2p = jnp.concatenate([zeros, p["rw_a2"]], axis=0).astype(BF16)
    hl = jnp.arange(MXU_DIM) // RW_HEAD
    ones_bd = (hl[:, None] == hl[None, :]).astype(BF16)
    return dict(vec=vec, mu_wa=mu[3 * RW_C:].reshape(1, -1).astype(F32), w2p=w2p, a2p=a2p,
                ones_bd=ones_bd)


def _const_spec(arr, ngrid):
    zeros = (0,) * arr.ndim
    if ngrid == 1:
        return pl.BlockSpec(arr.shape, lambda i: zeros, pipeline_mode=pl.Buffered(1))
    return pl.BlockSpec(arr.shape, lambda i, j: zeros, pipeline_mode=pl.Buffered(1))


def _rwkv_prompt(sh, gb, cs, batch, seq, ret_sample):
    C = RW_CHUNK
    nq = RW_SEQS_PER_STEP
    nc = seq // C
    n_sh = 3 * RW_C + 2 * RW_LORA
    blk = lambda n: pl.BlockSpec((nq, C, n), lambda b, c: (b, c, 0))
    consts = [cs[n] for n in ("vec", "w2p", "a2p", "ones_bd")]
    rt_args, rt_in, rt_out, rt_shape = _ret_step_operands(
        *ret_sample, n_steps=(batch // nq) * nc, step_index=lambda b, c: b * nc + c)
    y, s, rt_y, rt_s = pl.pallas_call(
        _rwkv_chunk_kernel,
        grid=(batch // nq, nc),
        in_specs=[blk(n_sh), blk(RW_C)] + [_const_spec(a, 2) for a in consts] + rt_in,
        out_specs=[blk(RW_C),
                   pl.BlockSpec((nq, RW_HEADS, RW_HEAD, RW_HEAD), lambda b, c: (b, 0, 0, 0))] + rt_out,
        out_shape=[jax.ShapeDtypeStruct((batch, seq, RW_C), BF16),
                   jax.ShapeDtypeStruct((batch, RW_HEADS, RW_HEAD, RW_HEAD), F32)] + rt_shape,
        scratch_shapes=[pltpu.VMEM((nq, N_GROUPS, MXU_DIM, MXU_DIM), F32)],
        compiler_params=_params("arbitrary", "arbitrary"),
        name="rwkv_chunk",
    )(sh.reshape(batch, seq, n_sh), gb.reshape(batch, seq, RW_C), *consts, *rt_args)
    return y.reshape(batch * seq, RW_C), s, rt_y.reshape(rt_y.shape[0], RET_V), rt_s


def _rwkv_step_kernel(r_ref, k_ref, v_ref, wa_ref, gb_ref, s_ref, col_ref, mu_wa_ref, w2t_ref, a2t_ref,
                      y_ref, so_ref, o_scr):
    nb = y_ref.shape[-1]
    col = {n: col_ref[:, i:i + 1] for i, n in enumerate(RW_VEC_ROWS)}
    lerp = lambda ref, mu: ref[:, :nb] + (ref[:, nb:2 * nb] - ref[:, :nb]) * mu
    r = lerp(r_ref, col["mu_r"])
    k = lerp(k_ref, col["mu_k"])
    v = lerp(v_ref, col["mu_v"])
    zwa = lerp(wa_ref, mu_wa_ref[...])
    wpre = col["w0"] + _dot(w2t_ref[...], jnp.tanh(zwa[:RW_LORA]))
    decay = jnp.exp(-math.exp(-0.5) * _sigmoid(wpre))
    a = _sigmoid(col["a0"] + _dot(a2t_ref[...], zwa[RW_LORA:]))
    kk = k * col["k_k"]
    kk = kk * lax.rsqrt(jnp.maximum(jnp.sum(kk * kk, axis=0, keepdims=True), 1e-24))
    k = k * (1.0 + (a - 1.0) * col["k_a"])
    beta = a * kk
    for i in range(RW_HEAD):
        s = s_ref[i]
        sk = jnp.sum(s * kk, axis=0, keepdims=True)
        s_new = s * decay - sk * beta + v[i:i + 1, :] * k
        so_ref[i] = s_new
        o_scr[i:i + 1, :] = jnp.sum(s_new * r, axis=0, keepdims=True)
    o = o_scr[...]
    d = o - jnp.mean(o, axis=0, keepdims=True)
    var = jnp.mean(d * d, axis=0, keepdims=True)
    on = d * lax.rsqrt(var + RW_GN_EPS) * col["ln_w"] + col["ln_b"]
    bonus = jnp.sum(r * k * col["r_k"], axis=0, keepdims=True) * v
    y_ref[...] = ((on + bonus) * _silu(gb_ref[:, :nb])).astype(y_ref.dtype)


def _rwkv_sample(sh2, gb2, p, cs, state):
    nb = state.shape[1]
    sht = sh2.T
    gbt = gb2.T
    st = jnp.transpose(state, (0, 2, 3, 4, 1))
    n_head_blocks = RW_C // RW_HEAD
    rows = lambda off: pl.BlockSpec((RW_HEAD, 2 * nb), lambda h: (h + off, 0))
    lora_blk = pl.BlockSpec((2 * RW_LORA, 2 * nb), lambda h: (3 * RW_C // (2 * RW_LORA), 0))
    st_blk = pl.BlockSpec((None, None, RW_HEAD, RW_HEAD, nb), lambda h: (0, h, 0, 0, 0))
    wt_blk = pl.BlockSpec((RW_HEAD, RW_LORA), lambda h: (h, 0))
    yt, so = pl.pallas_call(
        _rwkv_step_kernel,
        grid=(RW_HEADS,),
        in_specs=[rows(0), rows(n_head_blocks), rows(2 * n_head_blocks), lora_blk, rows(0), st_blk,
                  pl.BlockSpec((RW_HEAD, len(RW_VEC_ROWS)), lambda h: (h, 0)),
                  pl.BlockSpec((2 * RW_LORA, 1), lambda h: (0, 0)), wt_blk, wt_blk],
        out_specs=[pl.BlockSpec((RW_HEAD, nb), lambda h: (h, 0)), st_blk],
        out_shape=[jax.ShapeDtypeStruct((RW_C, nb), BF16), jax.ShapeDtypeStruct(st.shape, F32)],
        scratch_shapes=[pltpu.VMEM((RW_HEAD, nb), F32)],
        compiler_params=_params("arbitrary"),
        name="rwkv_step",
    )(sht, sht, sht, sht, gbt, st, cs["vec"].T, cs["mu_wa"].reshape(-1, 1),
      p["rw_w2"].T.astype(BF16), p["rw_a2"].T.astype(BF16))
    return yt.T, jnp.transpose(so, (0, 4, 1, 2, 3))


def _tail_kernel(ya_ref, yb_ref, m_ref, x_ref, p_ref, wda_ref, wdb_ref, wout_ref, wple_ref, wgate_ref,
                 pg_ref, fg_ref, y_ref):
    m = m_ref[...].astype(F32)
    merged = (_sigmoid(m[:, :D_MODEL]) * jnp.dot(ya_ref[...], wda_ref[...], preferred_element_type=F32)
              + _sigmoid(m[:, D_MODEL:]) * jnp.dot(yb_ref[...], wdb_ref[...], preferred_element_type=F32))
    x = x_ref[...] + _dot(merged, wout_ref[...])
    gate = _sigmoid(_dot(_rms(x, pg_ref[...]), wgate_ref[...]))
    x = x + _dot(p_ref[...], wple_ref[...]) * gate
    y_ref[...] = _rms(x, fg_ref[...])


def _tail(ya, yb, m, x, p, w, tm):
    rows = x.shape[0]
    tile = lambda n: pl.BlockSpec((tm, n), lambda i: (i, 0))
    consts = [w["wda"], w["wdb"], w["wout"], w["wple"], w["wgate"], w["ple_g"], w["final_g"]]
    return pl.pallas_call(
        _tail_kernel,
        grid=(rows // tm,),
        in_specs=[tile(RET_V), tile(RW_C), tile(2 * D_MODEL), tile(D_MODEL), tile(PLE_DIM)]
        + [_const_spec(a, 1) for a in consts],
        out_specs=tile(D_MODEL),
        out_shape=jax.ShapeDtypeStruct((rows, D_MODEL), F32),
        compiler_params=_params("arbitrary"),
        name="tail",
    )(ya, yb, m, x, p, *consts)


def _layer_weights(p):
    return dict(
        w_in=p["w_in"].astype(BF16),
        wda=p["w_down_a"].astype(BF16), wdb=p["w_down_b"].astype(BF16), wout=p["w_out"].astype(BF16),
        wple=p["w_ple"].astype(BF16), wgate=p["w_ple_gate"].astype(BF16),
        ple_g=p["ple_norm_g"].reshape(1, -1), final_g=p["final_norm_g"].reshape(1, -1),
    )


def _layer_paths(x_p, pe_p, x_s, h_prev, s_ret, s_rw, pe_s, p, w, cs):
    batch, seq, d = x_p.shape
    rows = batch * seq
    nb = x_s.shape[0]
    xp2 = x_p.reshape(rows, d)
    xs2 = x_s.reshape(nb, d)
    shift_p = _rmsnorm(x_p[:, -1, :], p["norm_g"], F32, batch)
    h_s = _rmsnorm(xs2, p["norm_g"], F32, nb)
    hcat = jnp.concatenate([h_s, h_prev], axis=0)
    qk_s, v_s, ga_s, sh_s, gb_s, m_s = _in_proj(hcat, p["norm_g"], w["w_in"], False, F32, 2 * nb)
    ya_p, sh, gb, m, ret_p = _in_proj_retention(xp2, p["norm_g"], w["w_in"], p["rw_mu"].reshape(1, -1),
                                                batch, seq, PROMPT_PROJ_ROWS)
    yb_p, rw_p, ya_s, ret_s = _rwkv_prompt(sh, gb, cs, batch, seq, (qk_s, v_s, ga_s, s_ret))
    y_p = _tail(ya_p, yb_p, m, xp2, pe_p.reshape(rows, PLE_DIM), w, PROMPT_TAIL_ROWS)
    yb_s, rw_s = _rwkv_sample(sh_s, gb_s, p, cs, s_rw)
    y_s = _tail(ya_s, yb_s, m_s, xs2, pe_s.reshape(nb, PLE_DIM), w, nb)
    return (y_p.reshape(batch, seq, d), shift_p, ret_p, rw_p,
            y_s.reshape(nb, 1, d), h_s, ret_s, rw_s)


def kernel(x_prompt, x_sample, state_ret, state_rwkv, state_shift, p_prompt, p_sample, norm_g, w_in, rw_mu, rw_w0, rw_w2, rw_a0, rw_a2, rw_k_k, rw_k_a, rw_r_k, rw_ln_w, rw_ln_b, w_down_a, w_down_b, w_out, w_ple, ple_norm_g, w_ple_gate, final_norm_g):
    assert norm_g.shape[0] == 1, "single-layer step"
    p = dict(norm_g=norm_g[0], w_in=w_in[0], rw_mu=rw_mu[0], rw_w0=rw_w0[0], rw_w2=rw_w2[0],
             rw_a0=rw_a0[0], rw_a2=rw_a2[0], rw_k_k=rw_k_k[0], rw_k_a=rw_k_a[0], rw_r_k=rw_r_k[0],
             rw_ln_w=rw_ln_w[0], rw_ln_b=rw_ln_b[0], w_down_a=w_down_a[0], w_down_b=w_down_b[0],
             w_out=w_out[0], w_ple=w_ple[0], ple_norm_g=ple_norm_g[0], w_ple_gate=w_ple_gate[0],
             final_norm_g=final_norm_g)
    w = _layer_weights(p)
    cs = _rwkv_consts(p)
    y_p, sh_p, ret_p, rw_p, y_s, sh_s, ret_s, rw_s = _layer_paths(
        x_prompt, p_prompt[0], x_sample, state_shift[0], state_ret[0], state_rwkv, p_sample[0], p, w, cs)
    return (y_p, y_s, ret_p[None], rw_p[None], sh_p[None], ret_s[None], rw_s, sh_s[None])
```

```python
import math

import jax
import jax.numpy as jnp
from jax import lax
from jax.experimental import pallas as pl
from jax.experimental.pallas import tpu as pltpu

F32 = jnp.float32
BF16 = jnp.bfloat16

D_MODEL = 1024
RET_HEADS = 4
RET_DK = 256
RET_DV = 512
RET_QK = RET_HEADS * RET_DK
RET_V = RET_HEADS * RET_DV
ROPE_BASE = 10000.0
RW_HEAD = 64
RW_HEADS = D_MODEL // RW_HEAD
RW_C = RW_HEADS * RW_HEAD
RW_LORA = 64
RW_GN_EPS = 1e-5 * RW_HEAD
RW_CHUNK = 64
PLE_DIM = 256
NORM_EPS = 1e-6
PAST_LEN = 16384

LANES = 128
MXU_DIM = 256
HEADS_PER_GROUP = MXU_DIM // RW_HEAD
N_GROUPS = RW_C // MXU_DIM
HEAD_SHIFT = RW_HEAD.bit_length() - 1
VMEM_LIMIT_BYTES = 56 * 1024 * 1024
PROMPT_PROJ_ROWS = 256
PROMPT_TAIL_ROWS = 512
RW_SEQS_PER_STEP = 4


def _params(*sem):
    return pltpu.CompilerParams(dimension_semantics=sem, vmem_limit_bytes=VMEM_LIMIT_BYTES)


def _dot(a, b):
    return jnp.dot(a.astype(BF16), b.astype(BF16), preferred_element_type=F32)


def _dot_nt(a, b):
    return lax.dot_general(a.astype(BF16), b.astype(BF16), (((1,), (1,)), ((), ())),
                           preferred_element_type=F32)


def _dot_tn(a, b):
    return lax.dot_general(a.astype(BF16), b.astype(BF16), (((0,), (0,)), ((), ())),
                           preferred_element_type=F32)


def _sigmoid(x):
    return 0.5 * jnp.tanh(0.5 * x) + 0.5


def _silu(x):
    return x * _sigmoid(x)


def _rms(x, g):
    return x * lax.rsqrt(jnp.mean(x * x, axis=-1, keepdims=True) + NORM_EPS) * g


def _rmsnorm_kernel(x_ref, g_ref, o_ref):
    o_ref[...] = _rms(x_ref[...], g_ref[...]).astype(o_ref.dtype)


def _rmsnorm(x, g, out_dtype, tm):
    m, d = x.shape
    return pl.pallas_call(
        _rmsnorm_kernel,
        grid=(m // tm,),
        in_specs=[pl.BlockSpec((tm, d), lambda i: (i, 0)), pl.BlockSpec((1, d), lambda i: (0, 0))],
        out_specs=pl.BlockSpec((tm, d), lambda i: (i, 0)),
        out_shape=jax.ShapeDtypeStruct((m, d), out_dtype),
        compiler_params=_params("arbitrary"),
        name="rmsnorm",
    )(x, g.reshape(1, d))


PROJ_WIDTHS = (2 * RET_QK, RET_V, RET_V, 3 * RW_C + 2 * RW_LORA, RW_C, 2 * D_MODEL)


SAMPLE_PROJ_K_ROWS = 128


def _sample_proj_kernel(h_ref, w_ref, wb_ref, *out_refs):
    @pl.when(pl.program_id(0) == 0)
    def _():
        for o_ref in out_refs:
            o_ref[...] = jnp.zeros_like(o_ref)

    wb = w_ref[...].astype(BF16)
    wb_ref[...] = wb
    h = h_ref[...].astype(BF16)
    off = 0
    for o_ref, n in zip(out_refs, PROJ_WIDTHS):
        o_ref[...] += jnp.dot(h, wb[:, off:off + n], preferred_element_type=F32)
        off += n


def _sample_proj(h, w_in):
    r, d = h.shape
    n_all = w_in.shape[1]
    tk = SAMPLE_PROJ_K_ROWS
    outs = pl.pallas_call(
        _sample_proj_kernel,
        grid=(d // tk,),
        in_specs=[pl.BlockSpec((r, tk), lambda k: (0, k)), pl.BlockSpec((tk, n_all), lambda k: (k, 0))],
        out_specs=[pl.BlockSpec((tk, n_all), lambda k: (k, 0))]
        + [pl.BlockSpec((r, n), lambda k: (0, 0)) for n in PROJ_WIDTHS],
        out_shape=[jax.ShapeDtypeStruct((d, n_all), BF16)]
        + [jax.ShapeDtypeStruct((r, n), F32) for n in PROJ_WIDTHS],
        compiler_params=_params("arbitrary"),
        name="sample_proj",
    )(h, w_in)
    return outs[0], outs[1:]


def _in_proj_ret_kernel(dec_ref, x_ref, g_ref, w_ref, cos_ref, sin_ref, mu_ref,
                        ya_ref, sh_ref, gb_ref, m_ref, s_ref, carry):
    @pl.when(pl.program_id(1) == 0)
    def _():
        s_ref[...] = jnp.zeros_like(s_ref)
        carry[...] = jnp.zeros_like(carry)

    tm = x_ref.shape[0]
    h = _rms(x_ref[...], g_ref[...]).astype(BF16)
    o_qk, o_v, o_ga, o_sh, o_gb, o_m = [sum(PROJ_WIDTHS[:i]) for i in range(len(PROJ_WIDTHS))]
    proj = lambda off, n: jnp.dot(h, w_ref[:, off:off + n], preferred_element_type=F32)
    qk = proj(o_qk, 2 * RET_QK)
    v = proj(o_v, RET_V)
    ga = proj(o_ga, RET_V)
    sh = proj(o_sh, PROJ_WIDTHS[3])
    row = lax.broadcasted_iota(jnp.int32, (tm, 1), 0)
    sh_prev = jnp.where(row == 0, carry[...], pltpu.roll(sh, 1, 0))
    carry[...] = sh[tm - 1:tm, :]
    sh_ref[...] = (sh + (sh_prev - sh) * mu_ref[...]).astype(sh_ref.dtype)
    gb_ref[...] = _silu(proj(o_gb, PROJ_WIDTHS[4])).astype(gb_ref.dtype)
    m_ref[...] = proj(o_m, PROJ_WIDTHS[5]).astype(m_ref.dtype)
    _ret_chunk(qk[:, :RET_QK], qk[:, RET_QK:], v, ga, cos_ref[...], sin_ref[...], dec_ref, s_ref, ya_ref)


def _in_proj_retention(x, g, w_in, mu, batch, seq, tm):
    m, d = x.shape
    nt = seq // tm
    n_all = w_in.shape[1]
    log_g = _ret_decay_table()
    dec = jnp.stack([log_g, jnp.exp(tm * log_g)], axis=1).reshape(-1)
    half = RET_DK // 2
    cos, sin = _rope_tables(jnp.arange(seq, dtype=F32))
    row = lambda b, t: (b * nt + t, 0)
    widths = (RET_V,) + PROJ_WIDTHS[3:]
    return pl.pallas_call(
        _in_proj_ret_kernel,
        grid=(batch, nt),
        in_specs=[pl.BlockSpec(memory_space=pltpu.SMEM),
                  pl.BlockSpec((tm, d), row), pl.BlockSpec((1, d), lambda b, t: (0, 0)),
                  pl.BlockSpec((d, n_all), lambda b, t: (0, 0), pipeline_mode=pl.Buffered(1)),
                  pl.BlockSpec((tm, half), lambda b, t: (t, 0)),
                  pl.BlockSpec((tm, half), lambda b, t: (t, 0)), _const_spec(mu, 2)],
        out_specs=[pl.BlockSpec((tm, n), row) for n in widths]
        + [pl.BlockSpec((None, RET_HEADS, RET_DK, RET_DV), lambda b, t: (b, 0, 0, 0))],
        out_shape=[jax.ShapeDtypeStruct((m, n), BF16) for n in widths]
        + [jax.ShapeDtypeStruct((batch, RET_HEADS, RET_DK, RET_DV), F32)],
        scratch_shapes=[pltpu.VMEM((1, PROJ_WIDTHS[3]), F32)],
        compiler_params=_params("arbitrary", "arbitrary"),
        name="in_proj_retention",
    )(dec, x, g.reshape(1, d), w_in, cos, sin, mu)


def _rope(x, cos, sin):
    half = x.shape[-1] // 2
    x1, x2 = x[:, :half], x[:, half:]
    return jnp.concatenate([x1 * cos - x2 * sin, x2 * cos + x1 * sin], axis=-1)


def _rope_tables(pos):
    half = RET_DK // 2
    inv = ROPE_BASE ** (-jnp.arange(half, dtype=F32) / half)
    ang = pos[:, None] * inv[None, :]
    return jnp.cos(ang), jnp.sin(ang)


def _ret_chunk(q, k, v, ga, cos, sin, dec_ref, s_ref, y_ref):
    C = q.shape[0]
    H = range(RET_HEADS)
    ti = lax.broadcasted_iota(jnp.int32, (C, C), 0)
    tj = lax.broadcasted_iota(jnp.int32, (C, C), 1)
    rel = (ti - tj).astype(F32)
    idx = lax.broadcasted_iota(jnp.int32, (C, 1), 0).astype(F32)
    lg = [dec_ref[2 * h] for h in H]
    qh = [_rope(q[:, h * RET_DK:(h + 1) * RET_DK], cos, sin) for h in H]
    kh = [_rope(k[:, h * RET_DK:(h + 1) * RET_DK], cos, sin) * (RET_DK ** -0.5) for h in H]
    vh = [v[:, h * RET_DV:(h + 1) * RET_DV].astype(BF16) for h in H]
    s = [s_ref[h] for h in H]
    inner = [_dot_nt(qh[h], kh[h]) * jnp.where(rel >= 0, jnp.exp(jnp.maximum(rel, 0.0) * lg[h]), 0.0)
             for h in H]
    o = [_dot(inner[h], vh[h]) + _dot(qh[h] * jnp.exp((idx + 1.0) * lg[h]), s[h]) for h in H]
    for h in H:
        s_ref[h] = dec_ref[2 * h + 1] * s[h] + _dot_tn(kh[h] * jnp.exp((C - 1.0 - idx) * lg[h]), vh[h])
    for h in H:
        on = o[h] * lax.rsqrt(jnp.mean(o[h] * o[h], axis=-1, keepdims=True) + NORM_EPS)
        y_ref[:, h * RET_DV:(h + 1) * RET_DV] = (
            on * _silu(ga[:, h * RET_DV:(h + 1) * RET_DV])).astype(y_ref.dtype)


def _ret_decay_table():
    log_g = jnp.log(1.0 - jnp.exp2(-5.0 - jnp.arange(RET_HEADS, dtype=F32)))
    return log_g


def _ret_step(dec_ref, q_ref, k_ref, v_ref, ga_ref, cos_ref, sin_ref, s_ref, y_ref, so_ref):
    row0 = lax.broadcasted_iota(jnp.int32, (8, 1), 0)
    cos = cos_ref[...]
    sin = sin_ref[...]
    for b, h in [(b, h) for b in range(q_ref.shape[0]) for h in range(RET_HEADS)]:
        g = dec_ref[h]
        q = _rope(q_ref[b, :, h * RET_DK:(h + 1) * RET_DK], cos, sin)
        k = _rope(k_ref[b, :, h * RET_DK:(h + 1) * RET_DK], cos, sin) * (RET_DK ** -0.5)
        v = v_ref[b, :, h * RET_DV:(h + 1) * RET_DV]
        s = s_ref[b, h]
        qk = jnp.sum(q * k, axis=-1, keepdims=True)
        q8 = jnp.broadcast_to(q, (8, RET_DK))
        o = qk * v + g * _dot(q8, s)[0:1, :]
        k_hi = k.astype(BF16).astype(F32)
        k_lo = k - k_hi
        v_hi = v.astype(BF16).astype(F32)
        v_lo = v - v_hi
        k8 = jnp.where(row0 < 2, k_hi, jnp.where(row0 == 2, k_lo, 0.0))
        v8 = jnp.where((row0 == 0) | (row0 == 2), v_hi, jnp.where(row0 == 1, v_lo, 0.0))
        so_ref[b, h] = g * s + _dot_tn(k8, v8)
        o = o * lax.rsqrt(jnp.mean(o * o, axis=-1, keepdims=True) + NORM_EPS)
        y_ref[b, :, h * RET_DV:(h + 1) * RET_DV] = (
            o * _silu(ga_ref[b, :, h * RET_DV:(h + 1) * RET_DV])).astype(y_ref.dtype)


def _ret_step_operands(qk, v, ga, state, n_steps, step_index):
    nb = state.shape[0]
    nq = nb // n_steps
    assert nq * n_steps == nb
    g = jnp.exp(_ret_decay_table())
    cos, sin = _rope_tables(PAST_LEN + jnp.arange(1, dtype=F32))
    r3 = lambda t: t.reshape(t.shape[0], 1, t.shape[-1])
    tab = pl.BlockSpec((1, RET_DK // 2), lambda *ids: (0, 0))
    vec = lambda n, col=0: pl.BlockSpec((nq, 1, n), lambda *ids: (step_index(*ids), 0, col))
    st = pl.BlockSpec((nq, RET_HEADS, RET_DK, RET_DV), lambda *ids: (step_index(*ids), 0, 0, 0))
    args = [g, r3(qk), r3(qk), r3(v), r3(ga), cos, sin, state]
    in_specs = [pl.BlockSpec(memory_space=pltpu.SMEM), vec(RET_QK, 0), vec(RET_QK, 1), vec(RET_V),
                vec(RET_V), tab, tab, st]
    out_specs = [vec(RET_V), st]
    out_shape = [jax.ShapeDtypeStruct((nb, 1, RET_V), BF16), jax.ShapeDtypeStruct(state.shape, F32)]
    return args, in_specs, out_specs, out_shape


def _head_sums(xs, ones_bd):
    rows = xs[0].shape[0]
    stack = jnp.concatenate(
        [x[:, g * MXU_DIM:(g + 1) * MXU_DIM] for x in xs for g in range(N_GROUPS)], axis=0)
    s = jnp.dot(stack.astype(BF16), ones_bd, preferred_element_type=F32)
    return [jnp.concatenate([s[(i * N_GROUPS + g) * rows:(i * N_GROUPS + g + 1) * rows]
                             for g in range(N_GROUPS)], axis=-1) for i in range(len(xs))]


def _rwkv_prep(z, zwa, w0, w2p, a0, a2p, k_k, k_a, ones_bd):
    r = z[:, :RW_C]
    k = z[:, RW_C:2 * RW_C]
    v = z[:, 2 * RW_C:]
    wpre = w0 + _dot(jnp.tanh(zwa), w2p)
    log_decay = -math.exp(-0.5) * _sigmoid(wpre)
    a = _sigmoid(a0 + _dot(zwa, a2p))
    kk = k * k_k
    kk = kk * lax.rsqrt(jnp.maximum(_head_sums([kk * kk], ones_bd)[0], 1e-24))
    k = k * (1.0 + (a - 1.0) * k_a)
    return r, log_decay, k, v, kk, a


def _rwkv_post(o, r, k, v, gate, r_k, ln_w, ln_b, ones_bd):
    o_sum, rk_sum = _head_sums([o, r * k * r_k], ones_bd)
    d = o - o_sum * (1.0 / RW_HEAD)
    var = _head_sums([d * d], ones_bd)[0] * (1.0 / RW_HEAD)
    on = d * lax.rsqrt(var + RW_GN_EPS) * ln_w + ln_b
    return (on + rk_sum * v) * gate


def _block_diag(x, head_of_lane):
    xb = x.astype(BF16)
    zero = jnp.zeros_like(xb)
    return jnp.concatenate(
        [jnp.where(head_of_lane == j, xb, zero) for j in range(HEADS_PER_GROUP)], axis=0)


def _rwkv_chunk_kernel(sh_ref, gb_ref, vec_ref, w2_ref, a2_ref, ones_ref,
                       rt_dec_ref, rt_q_ref, rt_k_ref, rt_v_ref, rt_ga_ref, rt_cos_ref, rt_sin_ref, rt_s_ref,
                       y_ref, so_ref, rt_y_ref, rt_so_ref, state):
    _ret_step(rt_dec_ref, rt_q_ref, rt_k_ref, rt_v_ref, rt_ga_ref, rt_cos_ref, rt_sin_ref, rt_s_ref,
              rt_y_ref, rt_so_ref)
    c = pl.program_id(1)
    nc = pl.num_programs(1)
    C = RW_CHUNK
    n_seq = sh_ref.shape[0]

    @pl.when(c == 0)
    def _():
        state[...] = jnp.zeros_like(state)

    ones_bd = ones_ref[...]
    vec = {n: vec_ref[i:i + 1, :] for i, n in enumerate(RW_VEC_ROWS)}
    ti = lax.broadcasted_iota(jnp.int32, (C, C), 0)
    tj = lax.broadcasted_iota(jnp.int32, (C, C), 1)
    tri = (ti >= tj).astype(BF16)

    tok = []
    for q in range(n_seq):
        z = sh_ref[q].astype(F32)
        r, lw, k, v, kk, a = _rwkv_prep(z[:, :3 * RW_C], z[:, 3 * RW_C:], vec["w0"], w2_ref[...],
                                        vec["a0"], a2_ref[...], vec["k_k"], vec["k_a"], ones_bd)
        lw_hi = lw.astype(BF16)
        lw_lo = (lw - lw_hi.astype(F32)).astype(BF16)
        cl = (jnp.dot(tri, lw_hi, preferred_element_type=F32)
              + jnp.dot(tri, lw_lo, preferred_element_type=F32))
        cl_last = cl[C - 1:C, :]
        e_neg = jnp.exp(-cl)
        g_chunk = jnp.exp(cl_last)
        e_rem = e_neg * g_chunk
        beta = a * kk
        b16 = lambda x: x.astype(BF16)
        tok.append(dict(
            r=r, k=k, v=v,
            ag=b16(-kk * jnp.exp(cl - lw)), rg=b16(r * jnp.exp(cl)), bg=b16(beta * e_neg),
            kg=b16(k * e_neg), bg_c=b16(beta * e_rem), kg_c=b16(k * e_rem), g_chunk=g_chunk))

    lane = lax.broadcasted_iota(jnp.int32, (1, MXU_DIM), 1)
    head_of_lane = lane >> HEAD_SHIFT
    t_col = lax.broadcasted_iota(jnp.int32, (C, MXU_DIM), 0)
    i_lane = lax.broadcasted_iota(jnp.int32, (C, MXU_DIM), 1) & (RW_HEAD - 1)
    strict = t_col > i_lane
    incl = t_col >= i_lane
    eye = (t_col == i_lane).astype(F32)
    vrow_head = lax.broadcasted_iota(jnp.int32, (MXU_DIM, MXU_DIM), 0) >> HEAD_SHIFT
    klane_head = lax.broadcasted_iota(jnp.int32, (MXU_DIM, MXU_DIM), 1) >> HEAD_SHIFT
    same_head = vrow_head == klane_head

    units = [(q, g) for q in range(n_seq) for g in range(N_GROUPS)]
    U = range(len(units))
    bd = lambda x: _block_diag(x, head_of_lane)
    grp = lambda name, u: tok[units[u][0]][name][:, units[u][1] * MXU_DIM:(units[u][1] + 1) * MXU_DIM]
    s_bd = [state[q, g] for q, g in units]
    lhs = [jnp.concatenate([grp("ag", u), grp("rg", u)], axis=0) for u in U]
    ab = [_dot_nt(lhs[u], bd(grp("bg", u))) for u in U]
    ak = [_dot_nt(lhs[u], bd(grp("kg", u))) for u in U]
    n_pow = [jnp.where(strict, ab[u][:C], 0.0) for u in U]
    a_ak = [jnp.where(strict, ak[u][:C], 0.0) for u in U]
    a_rb = [jnp.where(incl, ab[u][C:], 0.0) for u in U]
    a_rk = [jnp.where(incl, ak[u][C:], 0.0) for u in U]
    v_bd = [bd(grp("v", u)) for u in U]
    sv = [_dot_nt(lhs[u], s_bd[u]) for u in U]
    av = [_dot(jnp.concatenate([a_ak[u], a_rk[u]], axis=0), v_bd[u]) for u in U]
    t_inv = [eye + n_pow[u] for u in U]
    n_pow = [_dot(n_pow[u], bd(n_pow[u])) for u in U]
    for _ in range(int(math.log2(C)) - 2):
        prod = [_dot(jnp.concatenate([n_pow[u], t_inv[u]], axis=0), bd(n_pow[u])) for u in U]
        n_pow = [prod[u][:C] for u in U]
        t_inv = [t_inv[u] + prod[u][C:] for u in U]
    t_inv = [t_inv[u] + _dot(t_inv[u], bd(n_pow[u])) for u in U]
    p = [_dot(t_inv[u], bd(sv[u][:C] + av[u][:C])) for u in U]
    o = [sv[u][C:] + _dot(a_rb[u], bd(p[u])) + av[u][C:] for u in U]
    for u, (q, g) in enumerate(units):
        upd = _dot_tn(jnp.concatenate([p[u], grp("v", u)], axis=0),
                      jnp.concatenate([grp("bg_c", u), grp("kg_c", u)], axis=0))
        state[q, g] = s_bd[u] * grp("g_chunk", u) + jnp.where(same_head, upd, 0.0)
    for q in range(n_seq):
        o_q = jnp.concatenate(o[q * N_GROUPS:(q + 1) * N_GROUPS], axis=-1)
        t = tok[q]
        y = _rwkv_post(o_q, t["r"], t["k"], t["v"], gb_ref[q].astype(F32), vec["r_k"], vec["ln_w"],
                       vec["ln_b"], ones_bd)
        y_ref[q] = y.astype(y_ref.dtype)

    @pl.when(c == nc - 1)
    def _():
        for q in range(n_seq):
            for g in range(N_GROUPS):
                for j in range(HEADS_PER_GROUP):
                    blk = slice(j * RW_HEAD, (j + 1) * RW_HEAD)
                    so_ref[q, g * HEADS_PER_GROUP + j] = state[q, g, blk, blk]


RW_VEC_ROWS = ("mu_r", "mu_k", "mu_v", "w0", "a0", "k_k", "k_a", "r_k", "ln_w", "ln_b")


def _rwkv_consts(p):
    mu = p["rw_mu"]
    vec = jnp.stack([mu[:RW_C], mu[RW_C:2 * RW_C], mu[2 * RW_C:3 * RW_C], p["rw_w0"], p["rw_a0"],
                     p["rw_k_k"], p["rw_k_a"], p["rw_r_k"].reshape(-1), p["rw_ln_w"],
                     p["rw_ln_b"]]).astype(F32)
    zeros = jnp.zeros((RW_LORA, RW_C), F32)
    w2p = jnp.concatenate([p["rw_w2"], zeros], axis=0).astype(BF16)
    a2p = jnp.concatenate([zeros, p["rw_a2"]], axis=0).astype(BF16)
    hl = jnp.arange(MXU_DIM) // RW_HEAD
    ones_bd = (hl[:, None] == hl[None, :]).astype(BF16)
    return dict(vec=vec, mu_wa=mu[3 * RW_C:].reshape(1, -1).astype(F32), w2p=w2p, a2p=a2p,
                ones_bd=ones_bd)


def _const_spec(arr, ngrid):
    zeros = (0,) * arr.ndim
    if ngrid == 1:
        return pl.BlockSpec(arr.shape, lambda i: zeros, pipeline_mode=pl.Buffered(1))
    return pl.BlockSpec(arr.shape, lambda i, j: zeros, pipeline_mode=pl.Buffered(1))


def _rwkv_prompt(sh, gb, cs, batch, seq, ret_sample):
    C = RW_CHUNK
    nq = RW_SEQS_PER_STEP
    nc = seq // C
    n_sh = 3 * RW_C + 2 * RW_LORA
    blk = lambda n: pl.BlockSpec((nq, C, n), lambda b, c: (b, c, 0))
    consts = [cs[n] for n in ("vec", "w2p", "a2p", "ones_bd")]
    rt_args, rt_in, rt_out, rt_shape = _ret_step_operands(
        *ret_sample, n_steps=(batch // nq) * nc, step_index=lambda b, c: b * nc + c)
    y, s, rt_y, rt_s = pl.pallas_call(
        _rwkv_chunk_kernel,
        grid=(batch // nq, nc),
        in_specs=[blk(n_sh), blk(RW_C)] + [_const_spec(a, 2) for a in consts] + rt_in,
        out_specs=[blk(RW_C),
                   pl.BlockSpec((nq, RW_HEADS, RW_HEAD, RW_HEAD), lambda b, c: (b, 0, 0, 0))] + rt_out,
        out_shape=[jax.ShapeDtypeStruct((batch, seq, RW_C), BF16),
                   jax.ShapeDtypeStruct((batch, RW_HEADS, RW_HEAD, RW_HEAD), F32)] + rt_shape,
        scratch_shapes=[pltpu.VMEM((nq, N_GROUPS, MXU_DIM, MXU_DIM), F32)],
        compiler_params=_params("arbitrary", "arbitrary"),
        name="rwkv_chunk",
    )(sh.reshape(batch, seq, n_sh), gb.reshape(batch, seq, RW_C), *consts, *rt_args)
    return y.reshape(batch * seq, RW_C), s, rt_y.reshape(rt_y.shape[0], RET_V), rt_s


def _rwkv_step_kernel(r_ref, k_ref, v_ref, wa_ref, gb_ref, s_ref, col_ref, mu_wa_ref, w2t_ref, a2t_ref,
                      y_ref, so_ref, o_scr):
    nb = y_ref.shape[-1]
    col = {n: col_ref[:, i:i + 1] for i, n in enumerate(RW_VEC_ROWS)}
    lerp = lambda ref, mu: ref[:, :nb] + (ref[:, nb:2 * nb] - ref[:, :nb]) * mu
    r = lerp(r_ref, col["mu_r"])
    k = lerp(k_ref, col["mu_k"])
    v = lerp(v_ref, col["mu_v"])
    zwa = lerp(wa_ref, mu_wa_ref[...])
    wpre = col["w0"] + _dot(w2t_ref[...], jnp.tanh(zwa[:RW_LORA]))
    decay = jnp.exp(-math.exp(-0.5) * _sigmoid(wpre))
    a = _sigmoid(col["a0"] + _dot(a2t_ref[...], zwa[RW_LORA:]))
    kk = k * col["k_k"]
    kk = kk * lax.rsqrt(jnp.maximum(jnp.sum(kk * kk, axis=0, keepdims=True), 1e-24))
    k = k * (1.0 + (a - 1.0) * col["k_a"])
    beta = a * kk
    for i in range(RW_HEAD):
        s = s_ref[i]
        sk = jnp.sum(s * kk, axis=0, keepdims=True)
        s_new = s * decay - sk * beta + v[i:i + 1, :] * k
        so_ref[i] = s_new
        o_scr[i:i + 1, :] = jnp.sum(s_new * r, axis=0, keepdims=True)
    o = o_scr[...]
    d = o - jnp.mean(o, axis=0, keepdims=True)
    var = jnp.mean(d * d, axis=0, keepdims=True)
    on = d * lax.rsqrt(var + RW_GN_EPS) * col["ln_w"] + col["ln_b"]
    bonus = jnp.sum(r * k * col["r_k"], axis=0, keepdims=True) * v
    y_ref[...] = ((on + bonus) * _silu(gb_ref[:, :nb])).astype(y_ref.dtype)


def _rwkv_sample(sh2, gb2, p, cs, state):
    nb = state.shape[1]
    sht = sh2.T
    gbt = gb2.T
    st = jnp.transpose(state, (0, 2, 3, 4, 1))
    n_head_blocks = RW_C // RW_HEAD
    rows = lambda off: pl.BlockSpec((RW_HEAD, 2 * nb), lambda h: (h + off, 0))
    lora_blk = pl.BlockSpec((2 * RW_LORA, 2 * nb), lambda h: (3 * RW_C // (2 * RW_LORA), 0))
    st_blk = pl.BlockSpec((None, None, RW_HEAD, RW_HEAD, nb), lambda h: (0, h, 0, 0, 0))
    wt_blk = pl.BlockSpec((RW_HEAD, RW_LORA), lambda h: (h, 0))
    yt, so = pl.pallas_call(
        _rwkv_step_kernel,
        grid=(RW_HEADS,),
        in_specs=[rows(0), rows(n_head_blocks), rows(2 * n_head_blocks), lora_blk, rows(0), st_blk,
                  pl.BlockSpec((RW_HEAD, len(RW_VEC_ROWS)), lambda h: (h, 0)),
                  pl.BlockSpec((2 * RW_LORA, 1), lambda h: (0, 0)), wt_blk, wt_blk],
        out_specs=[pl.BlockSpec((RW_HEAD, nb), lambda h: (h, 0)), st_blk],
        out_shape=[jax.ShapeDtypeStruct((RW_C, nb), BF16), jax.ShapeDtypeStruct(st.shape, F32)],
        scratch_shapes=[pltpu.VMEM((RW_HEAD, nb), F32)],
        compiler_params=_params("arbitrary"),
        name="rwkv_step",
    )(sht, sht, sht, sht, gbt, st, cs["vec"].T, cs["mu_wa"].reshape(-1, 1),
      p["rw_w2"].T.astype(BF16), p["rw_a2"].T.astype(BF16))
    return yt.T, jnp.transpose(so, (0, 4, 1, 2, 3))


def _tail_kernel(ya_ref, yb_ref, m_ref, x_ref, p_ref, wda_ref, wdb_ref, wout_ref, wple_ref, wgate_ref,
                 pg_ref, fg_ref, y_ref):
    m = m_ref[...].astype(F32)
    merged = (_sigmoid(m[:, :D_MODEL]) * jnp.dot(ya_ref[...], wda_ref[...], preferred_element_type=F32)
              + _sigmoid(m[:, D_MODEL:]) * jnp.dot(yb_ref[...], wdb_ref[...], preferred_element_type=F32))
    x = x_ref[...] + _dot(merged, wout_ref[...])
    gate = _sigmoid(_dot(_rms(x, pg_ref[...]), wgate_ref[...]))
    x = x + _dot(p_ref[...], wple_ref[...]) * gate
    y_ref[...] = _rms(x, fg_ref[...])


def _tail(ya, yb, m, x, p, w, tm):
    rows = x.shape[0]
    tile = lambda n: pl.BlockSpec((tm, n), lambda i: (i, 0))
    consts = [w["wda"], w["wdb"], w["wout"], w["wple"], w["wgate"], w["ple_g"], w["final_g"]]
    return pl.pallas_call(
        _tail_kernel,
        grid=(rows // tm,),
        in_specs=[tile(RET_V), tile(RW_C), tile(2 * D_MODEL), tile(D_MODEL), tile(PLE_DIM)]
        + [_const_spec(a, 1) for a in consts],
        out_specs=tile(D_MODEL),
        out_shape=jax.ShapeDtypeStruct((rows, D_MODEL), F32),
        compiler_params=_params("arbitrary"),
        name="tail",
    )(ya, yb, m, x, p, *consts)


def _layer_weights(p):
    return dict(
        wda=p["w_down_a"].astype(BF16), wdb=p["w_down_b"].astype(BF16), wout=p["w_out"].astype(BF16),
        wple=p["w_ple"].astype(BF16), wgate=p["w_ple_gate"].astype(BF16),
        ple_g=p["ple_norm_g"].reshape(1, -1), final_g=p["final_norm_g"].reshape(1, -1),
    )


def _layer_paths(x_p, pe_p, x_s, h_prev, s_ret, s_rw, pe_s, p, w, cs):
    batch, seq, d = x_p.shape
    rows = batch * seq
    nb = x_s.shape[0]
    xp2 = x_p.reshape(rows, d)
    xs2 = x_s.reshape(nb, d)
    shift_p = _rmsnorm(x_p[:, -1, :], p["norm_g"], F32, batch)
    h_s = _rmsnorm(xs2, p["norm_g"], F32, nb)
    hcat = jnp.concatenate([h_s, h_prev], axis=0)
    w_in, (qk_s, v_s, ga_s, sh_s, gb_s, m_s) = _sample_proj(hcat, p["w_in"])
    ya_p, sh, gb, m, ret_p = _in_proj_retention(xp2, p["norm_g"], w_in, p["rw_mu"].reshape(1, -1),
                                                batch, seq, PROMPT_PROJ_ROWS)
    yb_p, rw_p, ya_s, ret_s = _rwkv_prompt(sh, gb, cs, batch, seq, (qk_s, v_s, ga_s, s_ret))
    y_p = _tail(ya_p, yb_p, m, xp2, pe_p.reshape(rows, PLE_DIM), w, PROMPT_TAIL_ROWS)
    yb_s, rw_s = _rwkv_sample(sh_s, gb_s, p, cs, s_rw)
    y_s = _tail(ya_s, yb_s, m_s, xs2, pe_s.reshape(nb, PLE_DIM), w, nb)
    return (y_p.reshape(batch, seq, d), shift_p, ret_p, rw_p,
            y_s.reshape(nb, 1, d), h_s, ret_s, rw_s)


def kernel(x_prompt, x_sample, state_ret, state_rwkv, state_shift, p_prompt, p_sample, norm_g, w_in, rw_mu, rw_w0, rw_w2, rw_a0, rw_a2, rw_k_k, rw_k_a, rw_r_k, rw_ln_w, rw_ln_b, w_down_a, w_down_b, w_out, w_ple, ple_norm_g, w_ple_gate, final_norm_g):
    assert norm_g.shape[0] == 1, "single-layer step"
    p = dict(norm_g=norm_g[0], w_in=w_in[0], rw_mu=rw_mu[0], rw_w0=rw_w0[0], rw_w2=rw_w2[0],
             rw_a0=rw_a0[0], rw_a2=rw_a2[0], rw_k_k=rw_k_k[0], rw_k_a=rw_k_a[0], rw_r_k=rw_r_k[0],
             rw_ln_w=rw_ln_w[0], rw_ln_b=rw_ln_b[0], w_down_a=w_down_a[0], w_down_b=w_down_b[0],
             w_out=w_out[0], w_ple=w_ple[0], ple_norm_g=ple_norm_g[0], w_ple_gate=w_ple_gate[0],
             final_norm_g=final_norm_g)
    w = _layer_weights(p)
    cs = _rwkv_consts(p)
    y_p, sh_p, ret_p, rw_p, y_s, sh_s, ret_s, rw_s = _layer_paths(
        x_prompt, p_prompt[0], x_sample, state_shift[0], state_ret[0], state_rwkv, p_sample[0], p, w, cs)
    return (y_p, y_s, ret_p[None], rw_p[None], sh_p[None], ret_s[None], rw_s, sh_s[None])
```

```python
import math

import jax
import jax.numpy as jnp
from jax import lax
from jax.experimental import pallas as pl
from jax.experimental.pallas import tpu as pltpu

F32 = jnp.float32
BF16 = jnp.bfloat16

D_MODEL = 1024
RET_HEADS = 4
RET_DK = 256
RET_DV = 512
RET_QK = RET_HEADS * RET_DK
RET_V = RET_HEADS * RET_DV
ROPE_BASE = 10000.0
RW_HEAD = 64
RW_HEADS = D_MODEL // RW_HEAD
RW_C = RW_HEADS * RW_HEAD
RW_LORA = 64
RW_GN_EPS = 1e-5 * RW_HEAD
RW_CHUNK = 64
PLE_DIM = 256
NORM_EPS = 1e-6
PAST_LEN = 16384

LANES = 128
MXU_DIM = 256
HEADS_PER_GROUP = MXU_DIM // RW_HEAD
N_GROUPS = RW_C // MXU_DIM
HEAD_SHIFT = RW_HEAD.bit_length() - 1
VMEM_LIMIT_BYTES = 56 * 1024 * 1024
PROMPT_PROJ_ROWS = 256
PROMPT_TAIL_ROWS = 512
RW_SEQS_PER_STEP = 4


def _params(*sem):
    return pltpu.CompilerParams(dimension_semantics=sem, vmem_limit_bytes=VMEM_LIMIT_BYTES)


def _dot(a, b):
    return jnp.dot(a.astype(BF16), b.astype(BF16), preferred_element_type=F32)


def _dot_nt(a, b):
    return lax.dot_general(a.astype(BF16), b.astype(BF16), (((1,), (1,)), ((), ())),
                           preferred_element_type=F32)


def _dot_tn(a, b):
    return lax.dot_general(a.astype(BF16), b.astype(BF16), (((0,), (0,)), ((), ())),
                           preferred_element_type=F32)


def _sigmoid(x):
    return 0.5 * jnp.tanh(0.5 * x) + 0.5


def _silu(x):
    return x * _sigmoid(x)


def _rms(x, g):
    return x * lax.rsqrt(jnp.mean(x * x, axis=-1, keepdims=True) + NORM_EPS) * g


def _rmsnorm_kernel(x_ref, g_ref, o_ref):
    o_ref[...] = _rms(x_ref[...], g_ref[...]).astype(o_ref.dtype)


def _rmsnorm(x, g, out_dtype, tm):
    m, d = x.shape
    return pl.pallas_call(
        _rmsnorm_kernel,
        grid=(m // tm,),
        in_specs=[pl.BlockSpec((tm, d), lambda i: (i, 0)), pl.BlockSpec((1, d), lambda i: (0, 0))],
        out_specs=pl.BlockSpec((tm, d), lambda i: (i, 0)),
        out_shape=jax.ShapeDtypeStruct((m, d), out_dtype),
        compiler_params=_params("arbitrary"),
        name="rmsnorm",
    )(x, g.reshape(1, d))


PROJ_WIDTHS = (2 * RET_QK, RET_V, RET_V, 3 * RW_C + 2 * RW_LORA, RW_C, 2 * D_MODEL)


SAMPLE_PROJ_K_ROWS = 128


def _sample_proj_kernel(h_ref, w_ref, wb_ref, *out_refs):
    @pl.when(pl.program_id(0) == 0)
    def _():
        for o_ref in out_refs:
            o_ref[...] = jnp.zeros_like(o_ref)

    wb = w_ref[...].astype(BF16)
    wb_ref[...] = wb
    h = h_ref[...].astype(BF16)
    off = 0
    for o_ref, n in zip(out_refs, PROJ_WIDTHS):
        o_ref[...] += jnp.dot(h, wb[:, off:off + n], preferred_element_type=F32)
        off += n


def _sample_proj(h, w_in):
    r, d = h.shape
    n_all = w_in.shape[1]
    tk = SAMPLE_PROJ_K_ROWS
    outs = pl.pallas_call(
        _sample_proj_kernel,
        grid=(d // tk,),
        in_specs=[pl.BlockSpec((r, tk), lambda k: (0, k)), pl.BlockSpec((tk, n_all), lambda k: (k, 0))],
        out_specs=[pl.BlockSpec((tk, n_all), lambda k: (k, 0))]
        + [pl.BlockSpec((r, n), lambda k: (0, 0)) for n in PROJ_WIDTHS],
        out_shape=[jax.ShapeDtypeStruct((d, n_all), BF16)]
        + [jax.ShapeDtypeStruct((r, n), F32) for n in PROJ_WIDTHS],
        compiler_params=_params("arbitrary"),
        name="sample_proj",
    )(h, w_in)
    return outs[0], outs[1:]


def _in_proj_ret_kernel(dec_ref, x_ref, g_ref, w_ref, cos_ref, sin_ref, mu_ref,
                        ya_ref, sh_ref, gb_ref, m_ref, s_ref, carry):
    @pl.when(pl.program_id(1) == 0)
    def _():
        s_ref[...] = jnp.zeros_like(s_ref)
        carry[...] = jnp.zeros_like(carry)

    tm = x_ref.shape[0]
    h = _rms(x_ref[...], g_ref[...]).astype(BF16)
    o_qk, o_v, o_ga, o_sh, o_gb, o_m = [sum(PROJ_WIDTHS[:i]) for i in range(len(PROJ_WIDTHS))]
    proj = lambda off, n: jnp.dot(h, w_ref[:, off:off + n], preferred_element_type=F32)
    qk = proj(o_qk, 2 * RET_QK)
    v = proj(o_v, RET_V)
    ga = proj(o_ga, RET_V)
    sh = proj(o_sh, PROJ_WIDTHS[3])
    row = lax.broadcasted_iota(jnp.int32, (tm, 1), 0)
    sh_prev = jnp.where(row == 0, carry[...], pltpu.roll(sh, 1, 0))
    carry[...] = sh[tm - 1:tm, :]
    sh_ref[...] = (sh + (sh_prev - sh) * mu_ref[...]).astype(sh_ref.dtype)
    gb_ref[...] = _silu(proj(o_gb, PROJ_WIDTHS[4])).astype(gb_ref.dtype)
    m_ref[...] = proj(o_m, PROJ_WIDTHS[5]).astype(m_ref.dtype)
    _ret_chunk(qk[:, :RET_QK], qk[:, RET_QK:], v, ga, cos_ref[...], sin_ref[...], dec_ref, s_ref, ya_ref)


def _in_proj_retention(x, g, w_in, mu, batch, seq, tm):
    m, d = x.shape
    nt = seq // tm
    n_all = w_in.shape[1]
    log_g = _ret_decay_table()
    dec = jnp.stack([log_g, jnp.exp(tm * log_g)], axis=1).reshape(-1)
    half = RET_DK // 2
    cos, sin = _rope_tables(jnp.arange(seq, dtype=F32))
    row = lambda b, t: (b * nt + t, 0)
    widths = (RET_V,) + PROJ_WIDTHS[3:]
    return pl.pallas_call(
        _in_proj_ret_kernel,
        grid=(batch, nt),
        in_specs=[pl.BlockSpec(memory_space=pltpu.SMEM),
                  pl.BlockSpec((tm, d), row), pl.BlockSpec((1, d), lambda b, t: (0, 0)),
                  pl.BlockSpec((d, n_all), lambda b, t: (0, 0), pipeline_mode=pl.Buffered(1)),
                  pl.BlockSpec((tm, half), lambda b, t: (t, 0)),
                  pl.BlockSpec((tm, half), lambda b, t: (t, 0)), _const_spec(mu, 2)],
        out_specs=[pl.BlockSpec((tm, n), row) for n in widths]
        + [pl.BlockSpec((None, RET_HEADS, RET_DK, RET_DV), lambda b, t: (b, 0, 0, 0))],
        out_shape=[jax.ShapeDtypeStruct((m, n), BF16) for n in widths]
        + [jax.ShapeDtypeStruct((batch, RET_HEADS, RET_DK, RET_DV), F32)],
        scratch_shapes=[pltpu.VMEM((1, PROJ_WIDTHS[3]), F32)],
        compiler_params=_params("arbitrary", "arbitrary"),
        name="in_proj_retention",
    )(dec, x, g.reshape(1, d), w_in, cos, sin, mu)


def _rope(x, cos, sin):
    half = x.shape[-1] // 2
    x1, x2 = x[:, :half], x[:, half:]
    return jnp.concatenate([x1 * cos - x2 * sin, x2 * cos + x1 * sin], axis=-1)


def _rope_tables(pos):
    half = RET_DK // 2
    inv = ROPE_BASE ** (-jnp.arange(half, dtype=F32) / half)
    ang = pos[:, None] * inv[None, :]
    return jnp.cos(ang), jnp.sin(ang)


def _ret_chunk(q, k, v, ga, cos, sin, dec_ref, s_ref, y_ref):
    C = q.shape[0]
    H = range(RET_HEADS)
    ti = lax.broadcasted_iota(jnp.int32, (C, C), 0)
    tj = lax.broadcasted_iota(jnp.int32, (C, C), 1)
    rel = (ti - tj).astype(F32)
    idx = lax.broadcasted_iota(jnp.int32, (C, 1), 0).astype(F32)
    lg = [dec_ref[2 * h] for h in H]
    qh = [_rope(q[:, h * RET_DK:(h + 1) * RET_DK], cos, sin) for h in H]
    kh = [_rope(k[:, h * RET_DK:(h + 1) * RET_DK], cos, sin) * (RET_DK ** -0.5) for h in H]
    vh = [v[:, h * RET_DV:(h + 1) * RET_DV].astype(BF16) for h in H]
    s = [s_ref[h] for h in H]
    inner = [_dot_nt(qh[h], kh[h]) * jnp.where(rel >= 0, jnp.exp(jnp.maximum(rel, 0.0) * lg[h]), 0.0)
             for h in H]
    o = [_dot(inner[h], vh[h]) + _dot(qh[h] * jnp.exp((idx + 1.0) * lg[h]), s[h]) for h in H]
    for h in H:
        s_ref[h] = dec_ref[2 * h + 1] * s[h] + _dot_tn(kh[h] * jnp.exp((C - 1.0 - idx) * lg[h]), vh[h])
    for h in H:
        on = o[h] * lax.rsqrt(jnp.mean(o[h] * o[h], axis=-1, keepdims=True) + NORM_EPS)
        y_ref[:, h * RET_DV:(h + 1) * RET_DV] = (
            on * _silu(ga[:, h * RET_DV:(h + 1) * RET_DV])).astype(y_ref.dtype)


def _ret_decay_table():
    log_g = jnp.log(1.0 - jnp.exp2(-5.0 - jnp.arange(RET_HEADS, dtype=F32)))
    return log_g


def _ret_step(dec_ref, qk_ref, v_ref, ga_ref, cos_ref, sin_ref, s_ref, y_ref, so_ref, first_row):
    row0 = lax.broadcasted_iota(jnp.int32, (8, 1), 0)
    cos = cos_ref[...]
    sin = sin_ref[...]
    for b, h in [(b, h) for b in range(s_ref.shape[0]) for h in range(RET_HEADS)]:
        g = dec_ref[h]
        row = pl.ds(first_row + b, 1)
        q = _rope(qk_ref[row, h * RET_DK:(h + 1) * RET_DK], cos, sin)
        k = _rope(qk_ref[row, RET_QK + h * RET_DK:RET_QK + (h + 1) * RET_DK], cos, sin) * (RET_DK ** -0.5)
        v = v_ref[row, h * RET_DV:(h + 1) * RET_DV]
        s = s_ref[b, h]
        qk = jnp.sum(q * k, axis=-1, keepdims=True)
        q8 = jnp.broadcast_to(q, (8, RET_DK))
        o = qk * v + g * _dot(q8, s)[0:1, :]
        k_hi = k.astype(BF16).astype(F32)
        k_lo = k - k_hi
        v_hi = v.astype(BF16).astype(F32)
        v_lo = v - v_hi
        k8 = jnp.where(row0 < 2, k_hi, jnp.where(row0 == 2, k_lo, 0.0))
        v8 = jnp.where((row0 == 0) | (row0 == 2), v_hi, jnp.where(row0 == 1, v_lo, 0.0))
        so_ref[b, h] = g * s + _dot_tn(k8, v8)
        o = o * lax.rsqrt(jnp.mean(o * o, axis=-1, keepdims=True) + NORM_EPS)
        y_ref[b, :, h * RET_DV:(h + 1) * RET_DV] = (
            o * _silu(ga_ref[row, h * RET_DV:(h + 1) * RET_DV])).astype(y_ref.dtype)


def _ret_step_operands(qk, v, ga, state, n_steps, step_index):
    nb = state.shape[0]
    nq = nb // n_steps
    assert nq * n_steps == nb
    g = jnp.exp(_ret_decay_table())
    cos, sin = _rope_tables(PAST_LEN + jnp.arange(1, dtype=F32))
    tab = pl.BlockSpec((1, RET_DK // 2), lambda *ids: (0, 0))
    st = pl.BlockSpec((nq, RET_HEADS, RET_DK, RET_DV), lambda *ids: (step_index(*ids), 0, 0, 0))
    args = [g, qk, v, ga, cos, sin, state]
    in_specs = [pl.BlockSpec(memory_space=pltpu.SMEM), _const_spec(qk, 2), _const_spec(v, 2),
                _const_spec(ga, 2), tab, tab, st]
    out_specs = [pl.BlockSpec((nq, 1, RET_V), lambda *ids: (step_index(*ids), 0, 0)), st]
    out_shape = [jax.ShapeDtypeStruct((nb, 1, RET_V), BF16), jax.ShapeDtypeStruct(state.shape, F32)]
    return args, in_specs, out_specs, out_shape


def _head_sums(xs, ones_bd):
    rows = xs[0].shape[0]
    stack = jnp.concatenate(
        [x[:, g * MXU_DIM:(g + 1) * MXU_DIM] for x in xs for g in range(N_GROUPS)], axis=0)
    s = jnp.dot(stack.astype(BF16), ones_bd, preferred_element_type=F32)
    return [jnp.concatenate([s[(i * N_GROUPS + g) * rows:(i * N_GROUPS + g + 1) * rows]
                             for g in range(N_GROUPS)], axis=-1) for i in range(len(xs))]


def _rwkv_prep(z, zwa, w0, w2p, a0, a2p, k_k, k_a, ones_bd):
    r = z[:, :RW_C]
    k = z[:, RW_C:2 * RW_C]
    v = z[:, 2 * RW_C:]
    wpre = w0 + _dot(jnp.tanh(zwa), w2p)
    log_decay = -math.exp(-0.5) * _sigmoid(wpre)
    a = _sigmoid(a0 + _dot(zwa, a2p))
    kk = k * k_k
    kk = kk * lax.rsqrt(jnp.maximum(_head_sums([kk * kk], ones_bd)[0], 1e-24))
    k = k * (1.0 + (a - 1.0) * k_a)
    return r, log_decay, k, v, kk, a


def _rwkv_post(o, r, k, v, gate, r_k, ln_w, ln_b, ones_bd):
    o_sum, rk_sum = _head_sums([o, r * k * r_k], ones_bd)
    d = o - o_sum * (1.0 / RW_HEAD)
    var = _head_sums([d * d], ones_bd)[0] * (1.0 / RW_HEAD)
    on = d * lax.rsqrt(var + RW_GN_EPS) * ln_w + ln_b
    return (on + rk_sum * v) * gate


def _block_diag(x, head_of_lane):
    xb = x.astype(BF16)
    zero = jnp.zeros_like(xb)
    return jnp.concatenate(
        [jnp.where(head_of_lane == j, xb, zero) for j in range(HEADS_PER_GROUP)], axis=0)


def _rwkv_chunk_kernel(sh_ref, gb_ref, vec_ref, w2_ref, a2_ref, ones_ref,
                       rt_dec_ref, rt_qk_ref, rt_v_ref, rt_ga_ref, rt_cos_ref, rt_sin_ref, rt_s_ref,
                       y_ref, so_ref, rt_y_ref, rt_so_ref, state):
    c = pl.program_id(1)
    nc = pl.num_programs(1)
    step = pl.program_id(0) * nc + c
    _ret_step(rt_dec_ref, rt_qk_ref, rt_v_ref, rt_ga_ref, rt_cos_ref, rt_sin_ref, rt_s_ref,
              rt_y_ref, rt_so_ref, step * rt_s_ref.shape[0])
    C = RW_CHUNK
    n_seq = sh_ref.shape[0]

    @pl.when(c == 0)
    def _():
        state[...] = jnp.zeros_like(state)

    ones_bd = ones_ref[...]
    vec = {n: vec_ref[i:i + 1, :] for i, n in enumerate(RW_VEC_ROWS)}
    ti = lax.broadcasted_iota(jnp.int32, (C, C), 0)
    tj = lax.broadcasted_iota(jnp.int32, (C, C), 1)
    tri = (ti >= tj).astype(BF16)

    tok = []
    for q in range(n_seq):
        z = sh_ref[q].astype(F32)
        r, lw, k, v, kk, a = _rwkv_prep(z[:, :3 * RW_C], z[:, 3 * RW_C:], vec["w0"], w2_ref[...],
                                        vec["a0"], a2_ref[...], vec["k_k"], vec["k_a"], ones_bd)
        lw_hi = lw.astype(BF16)
        lw_lo = (lw - lw_hi.astype(F32)).astype(BF16)
        cl = (jnp.dot(tri, lw_hi, preferred_element_type=F32)
              + jnp.dot(tri, lw_lo, preferred_element_type=F32))
        cl_last = cl[C - 1:C, :]
        e_neg = jnp.exp(-cl)
        g_chunk = jnp.exp(cl_last)
        e_rem = e_neg * g_chunk
        beta = a * kk
        b16 = lambda x: x.astype(BF16)
        tok.append(dict(
            r=r, k=k, v=v,
            ag=b16(-kk * jnp.exp(cl - lw)), rg=b16(r * jnp.exp(cl)), bg=b16(beta * e_neg),
            kg=b16(k * e_neg), bg_c=b16(beta * e_rem), kg_c=b16(k * e_rem), g_chunk=g_chunk))

    lane = lax.broadcasted_iota(jnp.int32, (1, MXU_DIM), 1)
    head_of_lane = lane >> HEAD_SHIFT
    t_col = lax.broadcasted_iota(jnp.int32, (C, MXU_DIM), 0)
    i_lane = lax.broadcasted_iota(jnp.int32, (C, MXU_DIM), 1) & (RW_HEAD - 1)
    strict = t_col > i_lane
    incl = t_col >= i_lane
    eye = (t_col == i_lane).astype(F32)
    vrow_head = lax.broadcasted_iota(jnp.int32, (MXU_DIM, MXU_DIM), 0) >> HEAD_SHIFT
    klane_head = lax.broadcasted_iota(jnp.int32, (MXU_DIM, MXU_DIM), 1) >> HEAD_SHIFT
    same_head = vrow_head == klane_head

    units = [(q, g) for q in range(n_seq) for g in range(N_GROUPS)]
    U = range(len(units))
    bd = lambda x: _block_diag(x, head_of_lane)
    grp = lambda name, u: tok[units[u][0]][name][:, units[u][1] * MXU_DIM:(units[u][1] + 1) * MXU_DIM]
    s_bd = [state[q, g] for q, g in units]
    lhs = [jnp.concatenate([grp("ag", u), grp("rg", u)], axis=0) for u in U]
    ab = [_dot_nt(lhs[u], bd(grp("bg", u))) for u in U]
    ak = [_dot_nt(lhs[u], bd(grp("kg", u))) for u in U]
    n_pow = [jnp.where(strict, ab[u][:C], 0.0) for u in U]
    a_ak = [jnp.where(strict, ak[u][:C], 0.0) for u in U]
    a_rb = [jnp.where(incl, ab[u][C:], 0.0) for u in U]
    a_rk = [jnp.where(incl, ak[u][C:], 0.0) for u in U]
    v_bd = [bd(grp("v", u)) for u in U]
    sv = [_dot_nt(lhs[u], s_bd[u]) for u in U]
    av = [_dot(jnp.concatenate([a_ak[u], a_rk[u]], axis=0), v_bd[u]) for u in U]
    t_inv = [eye + n_pow[u] for u in U]
    n_pow = [_dot(n_pow[u], bd(n_pow[u])) for u in U]
    for _ in range(int(math.log2(C)) - 2):
        prod = [_dot(jnp.concatenate([n_pow[u], t_inv[u]], axis=0), bd(n_pow[u])) for u in U]
        n_pow = [prod[u][:C] for u in U]
        t_inv = [t_inv[u] + prod[u][C:] for u in U]
    t_inv = [t_inv[u] + _dot(t_inv[u], bd(n_pow[u])) for u in U]
    p = [_dot(t_inv[u], bd(sv[u][:C] + av[u][:C])) for u in U]
    o = [sv[u][C:] + _dot(a_rb[u], bd(p[u])) + av[u][C:] for u in U]
    for u, (q, g) in enumerate(units):
        upd = _dot_tn(jnp.concatenate([p[u], grp("v", u)], axis=0),
                      jnp.concatenate([grp("bg_c", u), grp("kg_c", u)], axis=0))
        state[q, g] = s_bd[u] * grp("g_chunk", u) + jnp.where(same_head, upd, 0.0)
    for q in range(n_seq):
        o_q = jnp.concatenate(o[q * N_GROUPS:(q + 1) * N_GROUPS], axis=-1)
        t = tok[q]
        y = _rwkv_post(o_q, t["r"], t["k"], t["v"], gb_ref[q].astype(F32), vec["r_k"], vec["ln_w"],
                       vec["ln_b"], ones_bd)
        y_ref[q] = y.astype(y_ref.dtype)

    @pl.when(c == nc - 1)
    def _():
        for q in range(n_seq):
            for g in range(N_GROUPS):
                for j in range(HEADS_PER_GROUP):
                    blk = slice(j * RW_HEAD, (j + 1) * RW_HEAD)
                    so_ref[q, g * HEADS_PER_GROUP + j] = state[q, g, blk, blk]


RW_VEC_ROWS = ("mu_r", "mu_k", "mu_v", "w0", "a0", "k_k", "k_a", "r_k", "ln_w", "ln_b")


def _rwkv_consts(p):
    mu = p["rw_mu"]
    vec = jnp.stack([mu[:RW_C], mu[RW_C:2 * RW_C], mu[2 * RW_C:3 * RW_C], p["rw_w0"], p["rw_a0"],
                     p["rw_k_k"], p["rw_k_a"], p["rw_r_k"].reshape(-1), p["rw_ln_w"],
                     p["rw_ln_b"]]).astype(F32)
    zeros = jnp.zeros((RW_LORA, RW_C), F32)
    w2p = jnp.concatenate([p["rw_w2"], zeros], axis=0).astype(BF16)
    a2p = jnp.concatenate([zeros, p["rw_a2"]], axis=0).astype(BF16)
    hl = jnp.arange(MXU_DIM) // RW_HEAD
    ones_bd = (hl[:, None] == hl[None, :]).astype(BF16)
    return dict(vec=vec, mu_wa=mu[3 * RW_C:].reshape(1, -1).astype(F32), w2p=w2p, a2p=a2p,
                ones_bd=ones_bd)


def _const_spec(arr, ngrid):
    zeros = (0,) * arr.ndim
    if ngrid == 1:
        return pl.BlockSpec(arr.shape, lambda i: zeros, pipeline_mode=pl.Buffered(1))
    return pl.BlockSpec(arr.shape, lambda i, j: zeros, pipeline_mode=pl.Buffered(1))


def _rwkv_prompt(sh, gb, cs, batch, seq, ret_sample):
    C = RW_CHUNK
    nq = RW_SEQS_PER_STEP
    nc = seq // C
    n_sh = 3 * RW_C + 2 * RW_LORA
    blk = lambda n: pl.BlockSpec((nq, C, n), lambda b, c: (b, c, 0))
    consts = [cs[n] for n in ("vec", "w2p", "a2p", "ones_bd")]
    rt_args, rt_in, rt_out, rt_shape = _ret_step_operands(
        *ret_sample, n_steps=(batch // nq) * nc, step_index=lambda b, c: b * nc + c)
    y, s, rt_y, rt_s = pl.pallas_call(
        _rwkv_chunk_kernel,
        grid=(batch // nq, nc),
        in_specs=[blk(n_sh), blk(RW_C)] + [_const_spec(a, 2) for a in consts] + rt_in,
        out_specs=[blk(RW_C),
                   pl.BlockSpec((nq, RW_HEADS, RW_HEAD, RW_HEAD), lambda b, c: (b, 0, 0, 0))] + rt_out,
        out_shape=[jax.ShapeDtypeStruct((batch, seq, RW_C), BF16),
                   jax.ShapeDtypeStruct((batch, RW_HEADS, RW_HEAD, RW_HEAD), F32)] + rt_shape,
        scratch_shapes=[pltpu.VMEM((nq, N_GROUPS, MXU_DIM, MXU_DIM), F32)],
        compiler_params=_params("arbitrary", "arbitrary"),
        name="rwkv_chunk",
    )(sh.reshape(batch, seq, n_sh), gb.reshape(batch, seq, RW_C), *consts, *rt_args)
    return y.reshape(batch * seq, RW_C), s, rt_y.reshape(rt_y.shape[0], RET_V), rt_s


def _rwkv_step_kernel(r_ref, k_ref, v_ref, wa_ref, gb_ref, s_ref, col_ref, mu_wa_ref, w2t_ref, a2t_ref,
                      y_ref, so_ref, o_scr):
    nb = y_ref.shape[-1]
    col = {n: col_ref[:, i:i + 1] for i, n in enumerate(RW_VEC_ROWS)}
    lerp = lambda ref, mu: ref[:, :nb] + (ref[:, nb:2 * nb] - ref[:, :nb]) * mu
    r = lerp(r_ref, col["mu_r"])
    k = lerp(k_ref, col["mu_k"])
    v = lerp(v_ref, col["mu_v"])
    zwa = lerp(wa_ref, mu_wa_ref[...])
    wpre = col["w0"] + _dot(w2t_ref[...], jnp.tanh(zwa[:RW_LORA]))
    decay = jnp.exp(-math.exp(-0.5) * _sigmoid(wpre))
    a = _sigmoid(col["a0"] + _dot(a2t_ref[...], zwa[RW_LORA:]))
    kk = k * col["k_k"]
    kk = kk * lax.rsqrt(jnp.maximum(jnp.sum(kk * kk, axis=0, keepdims=True), 1e-24))
    k = k * (1.0 + (a - 1.0) * col["k_a"])
    beta = a * kk
    for i in range(RW_HEAD):
        s = s_ref[i]
        sk = jnp.sum(s * kk, axis=0, keepdims=True)
        s_new = s * decay - sk * beta + v[i:i + 1, :] * k
        so_ref[i] = s_new
        o_scr[i:i + 1, :] = jnp.sum(s_new * r, axis=0, keepdims=True)
    o = o_scr[...]
    d = o - jnp.mean(o, axis=0, keepdims=True)
    var = jnp.mean(d * d, axis=0, keepdims=True)
    on = d * lax.rsqrt(var + RW_GN_EPS) * col["ln_w"] + col["ln_b"]
    bonus = jnp.sum(r * k * col["r_k"], axis=0, keepdims=True) * v
    y_ref[...] = ((on + bonus) * _silu(gb_ref[:, :nb])).astype(y_ref.dtype)


def _rwkv_sample(sh2, gb2, p, cs, state):
    nb = state.shape[1]
    sht = sh2.T
    gbt = gb2.T
    st = jnp.transpose(state, (0, 2, 3, 4, 1))
    n_head_blocks = RW_C // RW_HEAD
    rows = lambda off: pl.BlockSpec((RW_HEAD, 2 * nb), lambda h: (h + off, 0))
    lora_blk = pl.BlockSpec((2 * RW_LORA, 2 * nb), lambda h: (3 * RW_C // (2 * RW_LORA), 0))
    st_blk = pl.BlockSpec((None, None, RW_HEAD, RW_HEAD, nb), lambda h: (0, h, 0, 0, 0))
    wt_blk = pl.BlockSpec((RW_HEAD, RW_LORA), lambda h: (h, 0))
    yt, so = pl.pallas_call(
        _rwkv_step_kernel,
        grid=(RW_HEADS,),
        in_specs=[rows(0), rows(n_head_blocks), rows(2 * n_head_blocks), lora_blk, rows(0), st_blk,
                  pl.BlockSpec((RW_HEAD, len(RW_VEC_ROWS)), lambda h: (h, 0)),
                  pl.BlockSpec((2 * RW_LORA, 1), lambda h: (0, 0)), wt_blk, wt_blk],
        out_specs=[pl.BlockSpec((RW_HEAD, nb), lambda h: (h, 0)), st_blk],
        out_shape=[jax.ShapeDtypeStruct((RW_C, nb), BF16), jax.ShapeDtypeStruct(st.shape, F32)],
        scratch_shapes=[pltpu.VMEM((RW_HEAD, nb), F32)],
        compiler_params=_params("arbitrary"),
        name="rwkv_step",
    )(sht, sht, sht, sht, gbt, st, cs["vec"].T, cs["mu_wa"].reshape(-1, 1),
      p["rw_w2"].T.astype(BF16), p["rw_a2"].T.astype(BF16))
    return yt.T, jnp.transpose(so, (0, 4, 1, 2, 3))


def _tail_kernel(ya_ref, yb_ref, m_ref, x_ref, p_ref, wda_ref, wdb_ref, wout_ref, wple_ref, wgate_ref,
                 pg_ref, fg_ref, y_ref):
    m = m_ref[...].astype(F32)
    merged = (_sigmoid(m[:, :D_MODEL]) * jnp.dot(ya_ref[...], wda_ref[...], preferred_element_type=F32)
              + _sigmoid(m[:, D_MODEL:]) * jnp.dot(yb_ref[...], wdb_ref[...], preferred_element_type=F32))
    x = x_ref[...] + _dot(merged, wout_ref[...])
    gate = _sigmoid(_dot(_rms(x, pg_ref[...]), wgate_ref[...]))
    x = x + _dot(p_ref[...], wple_ref[...]) * gate
    y_ref[...] = _rms(x, fg_ref[...])


def _tail(ya, yb, m, x, p, w, tm):
    rows = x.shape[0]
    tile = lambda n: pl.BlockSpec((tm, n), lambda i: (i, 0))
    consts = [w["wda"], w["wdb"], w["wout"], w["wple"], w["wgate"], w["ple_g"], w["final_g"]]
    return pl.pallas_call(
        _tail_kernel,
        grid=(rows // tm,),
        in_specs=[tile(RET_V), tile(RW_C), tile(2 * D_MODEL), tile(D_MODEL), tile(PLE_DIM)]
        + [_const_spec(a, 1) for a in consts],
        out_specs=tile(D_MODEL),
        out_shape=jax.ShapeDtypeStruct((rows, D_MODEL), F32),
        compiler_params=_params("arbitrary"),
        name="tail",
    )(ya, yb, m, x, p, *consts)


def _layer_weights(p):
    return dict(
        wda=p["w_down_a"].astype(BF16), wdb=p["w_down_b"].astype(BF16), wout=p["w_out"].astype(BF16),
        wple=p["w_ple"].astype(BF16), wgate=p["w_ple_gate"].astype(BF16),
        ple_g=p["ple_norm_g"].reshape(1, -1), final_g=p["final_norm_g"].reshape(1, -1),
    )


def _layer_paths(x_p, pe_p, x_s, h_prev, s_ret, s_rw, pe_s, p, w, cs):
    batch, seq, d = x_p.shape
    rows = batch * seq
    nb = x_s.shape[0]
    xp2 = x_p.reshape(rows, d)
    xs2 = x_s.reshape(nb, d)
    shift_p = _rmsnorm(x_p[:, -1, :], p["norm_g"], F32, batch)
    h_s = _rmsnorm(xs2, p["norm_g"], F32, nb)
    hcat = jnp.concatenate([h_s, h_prev], axis=0)
    w_in, (qk_s, v_s, ga_s, sh_s, gb_s, m_s) = _sample_proj(hcat, p["w_in"])
    ya_p, sh, gb, m, ret_p = _in_proj_retention(xp2, p["norm_g"], w_in, p["rw_mu"].reshape(1, -1),
                                                batch, seq, PROMPT_PROJ_ROWS)
    yb_p, rw_p, ya_s, ret_s = _rwkv_prompt(sh, gb, cs, batch, seq, (qk_s, v_s, ga_s, s_ret))
    y_p = _tail(ya_p, yb_p, m, xp2, pe_p.reshape(rows, PLE_DIM), w, PROMPT_TAIL_ROWS)
    yb_s, rw_s = _rwkv_sample(sh_s, gb_s, p, cs, s_rw)
    y_s = _tail(ya_s, yb_s, m_s, xs2, pe_s.reshape(nb, PLE_DIM), w, nb)
    return (y_p.reshape(batch, seq, d), shift_p, ret_p, rw_p,
            y_s.reshape(nb, 1, d), h_s, ret_s, rw_s)


def kernel(x_prompt, x_sample, state_ret, state_rwkv, state_shift, p_prompt, p_sample, norm_g, w_in, rw_mu, rw_w0, rw_w2, rw_a0, rw_a2, rw_k_k, rw_k_a, rw_r_k, rw_ln_w, rw_ln_b, w_down_a, w_down_b, w_out, w_ple, ple_norm_g, w_ple_gate, final_norm_g):
    assert norm_g.shape[0] == 1, "single-layer step"
    p = dict(norm_g=norm_g[0], w_in=w_in[0], rw_mu=rw_mu[0], rw_w0=rw_w0[0], rw_w2=rw_w2[0],
             rw_a0=rw_a0[0], rw_a2=rw_a2[0], rw_k_k=rw_k_k[0], rw_k_a=rw_k_a[0], rw_r_k=rw_r_k[0],
             rw_ln_w=rw_ln_w[0], rw_ln_b=rw_ln_b[0], w_down_a=w_down_a[0], w_down_b=w_down_b[0],
             w_out=w_out[0], w_ple=w_ple[0], ple_norm_g=ple_norm_g[0], w_ple_gate=w_ple_gate[0],
             final_norm_g=final_norm_g)
    w = _layer_weights(p)
    cs = _rwkv_consts(p)
    y_p, sh_p, ret_p, rw_p, y_s, sh_s, ret_s, rw_s = _layer_paths(
        x_prompt, p_prompt[0], x_sample, state_shift[0], state_ret[0], state_rwkv, p_sample[0], p, w, cs)
    return (y_p, y_s, ret_p[None], rw_p[None], sh_p[None], ret_s[None], rw_s, sh_s[None])
```

```python
import math

import jax
import jax.numpy as jnp
from jax import lax
from jax.experimental import pallas as pl
from jax.experimental.pallas import tpu as pltpu

F32 = jnp.float32
BF16 = jnp.bfloat16

D_MODEL = 1024
RET_HEADS = 4
RET_DK = 256
RET_DV = 512
RET_QK = RET_HEADS * RET_DK
RET_V = RET_HEADS * RET_DV
ROPE_BASE = 10000.0
RW_HEAD = 64
RW_HEADS = D_MODEL // RW_HEAD
RW_C = RW_HEADS * RW_HEAD
RW_LORA = 64
RW_GN_EPS = 1e-5 * RW_HEAD
RW_CHUNK = 64
PLE_DIM = 256
NORM_EPS = 1e-6
PAST_LEN = 16384

LANES = 128
MXU_DIM = 256
HEADS_PER_GROUP = MXU_DIM // RW_HEAD
N_GROUPS = RW_C // MXU_DIM
HEAD_SHIFT = RW_HEAD.bit_length() - 1
VMEM_LIMIT_BYTES = 56 * 1024 * 1024
PROMPT_PROJ_ROWS = 256
PROMPT_TAIL_ROWS = 512
RW_SEQS_PER_STEP = 4


def _params(*sem):
    return pltpu.CompilerParams(dimension_semantics=sem, vmem_limit_bytes=VMEM_LIMIT_BYTES)


def _dot(a, b):
    return jnp.dot(a.astype(BF16), b.astype(BF16), preferred_element_type=F32)


def _dot_nt(a, b):
    return lax.dot_general(a.astype(BF16), b.astype(BF16), (((1,), (1,)), ((), ())),
                           preferred_element_type=F32)


def _dot_tn(a, b):
    return lax.dot_general(a.astype(BF16), b.astype(BF16), (((0,), (0,)), ((), ())),
                           preferred_element_type=F32)


def _sigmoid(x):
    return 0.5 * jnp.tanh(0.5 * x) + 0.5


def _silu(x):
    return x * _sigmoid(x)


def _rms(x, g):
    return x * lax.rsqrt(jnp.mean(x * x, axis=-1, keepdims=True) + NORM_EPS) * g


def _rmsnorm_kernel(x_ref, g_ref, o_ref):
    o_ref[...] = _rms(x_ref[...], g_ref[...]).astype(o_ref.dtype)


def _rmsnorm(x, g, out_dtype, tm):
    m, d = x.shape
    return pl.pallas_call(
        _rmsnorm_kernel,
        grid=(m // tm,),
        in_specs=[pl.BlockSpec((tm, d), lambda i: (i, 0)), pl.BlockSpec((1, d), lambda i: (0, 0))],
        out_specs=pl.BlockSpec((tm, d), lambda i: (i, 0)),
        out_shape=jax.ShapeDtypeStruct((m, d), out_dtype),
        compiler_params=_params("arbitrary"),
        name="rmsnorm",
    )(x, g.reshape(1, d))


PROJ_WIDTHS = (2 * RET_QK, RET_V, RET_V, 3 * RW_C + 2 * RW_LORA, RW_C, 2 * D_MODEL)


SAMPLE_PROJ_K_ROWS = 128


def _sample_proj_kernel(h_ref, w_ref, wb_ref, *out_refs):
    @pl.when(pl.program_id(0) == 0)
    def _():
        for o_ref in out_refs:
            o_ref[...] = jnp.zeros_like(o_ref)

    wb = w_ref[...].astype(BF16)
    wb_ref[...] = wb
    h = h_ref[...].astype(BF16)
    off = 0
    for o_ref, n in zip(out_refs, PROJ_WIDTHS):
        o_ref[...] += jnp.dot(h, wb[:, off:off + n], preferred_element_type=F32)
        off += n


def _sample_proj(h, w_in):
    r, d = h.shape
    n_all = w_in.shape[1]
    tk = SAMPLE_PROJ_K_ROWS
    outs = pl.pallas_call(
        _sample_proj_kernel,
        grid=(d // tk,),
        in_specs=[pl.BlockSpec((r, tk), lambda k: (0, k)), pl.BlockSpec((tk, n_all), lambda k: (k, 0))],
        out_specs=[pl.BlockSpec((tk, n_all), lambda k: (k, 0))]
        + [pl.BlockSpec((r, n), lambda k: (0, 0)) for n in PROJ_WIDTHS],
        out_shape=[jax.ShapeDtypeStruct((d, n_all), BF16)]
        + [jax.ShapeDtypeStruct((r, n), F32) for n in PROJ_WIDTHS],
        compiler_params=_params("arbitrary"),
        name="sample_proj",
    )(h, w_in)
    return outs[0], outs[1:]


def _in_proj_ret_kernel(dec_ref, x_ref, g_ref, w_ref, cos_ref, sin_ref, mu_ref,
                        ya_ref, sh_ref, gb_ref, m_ref, s_ref, carry):
    @pl.when(pl.program_id(1) == 0)
    def _():
        s_ref[...] = jnp.zeros_like(s_ref)
        carry[...] = jnp.zeros_like(carry)

    tm = x_ref.shape[0]
    h = _rms(x_ref[...], g_ref[...]).astype(BF16)
    o_qk, o_v, o_ga, o_sh, o_gb, o_m = [sum(PROJ_WIDTHS[:i]) for i in range(len(PROJ_WIDTHS))]
    proj = lambda off, n: jnp.dot(h, w_ref[:, off:off + n], preferred_element_type=F32)
    qk = proj(o_qk, 2 * RET_QK)
    v = proj(o_v, RET_V)
    ga = proj(o_ga, RET_V)
    sh = proj(o_sh, PROJ_WIDTHS[3])
    row = lax.broadcasted_iota(jnp.int32, (tm, 1), 0)
    sh_prev = jnp.where(row == 0, carry[...], pltpu.roll(sh, 1, 0))
    carry[...] = sh[tm - 1:tm, :]
    sh_ref[...] = (sh + (sh_prev - sh) * mu_ref[...]).astype(sh_ref.dtype)
    gb_ref[...] = _silu(proj(o_gb, PROJ_WIDTHS[4])).astype(gb_ref.dtype)
    m_ref[...] = proj(o_m, PROJ_WIDTHS[5]).astype(m_ref.dtype)
    _ret_chunk(qk[:, :RET_QK], qk[:, RET_QK:], v, ga, cos_ref[...], sin_ref[...], dec_ref, s_ref, ya_ref)


def _in_proj_retention(x, g, w_in, mu, batch, seq, tm):
    m, d = x.shape
    nt = seq // tm
    n_all = w_in.shape[1]
    log_g = _ret_decay_table()
    dec = jnp.stack([log_g, jnp.exp(tm * log_g)], axis=1).reshape(-1)
    half = RET_DK // 2
    cos, sin = _rope_tables(jnp.arange(seq, dtype=F32))
    row = lambda b, t: (b * nt + t, 0)
    widths = (RET_V,) + PROJ_WIDTHS[3:]
    return pl.pallas_call(
        _in_proj_ret_kernel,
        grid=(batch, nt),
        in_specs=[pl.BlockSpec(memory_space=pltpu.SMEM),
                  pl.BlockSpec((tm, d), row), pl.BlockSpec((1, d), lambda b, t: (0, 0)),
                  pl.BlockSpec((d, n_all), lambda b, t: (0, 0), pipeline_mode=pl.Buffered(1)),
                  pl.BlockSpec((tm, half), lambda b, t: (t, 0)),
                  pl.BlockSpec((tm, half), lambda b, t: (t, 0)), _const_spec(mu, 2)],
        out_specs=[pl.BlockSpec((tm, n), row) for n in widths]
        + [pl.BlockSpec((None, RET_HEADS, RET_DK, RET_DV), lambda b, t: (b, 0, 0, 0))],
        out_shape=[jax.ShapeDtypeStruct((m, n), BF16) for n in widths]
        + [jax.ShapeDtypeStruct((batch, RET_HEADS, RET_DK, RET_DV), F32)],
        scratch_shapes=[pltpu.VMEM((1, PROJ_WIDTHS[3]), F32)],
        compiler_params=_params("arbitrary", "arbitrary"),
        name="in_proj_retention",
    )(dec, x, g.reshape(1, d), w_in, cos, sin, mu)


def _rope(x, cos, sin):
    half = x.shape[-1] // 2
    x1, x2 = x[:, :half], x[:, half:]
    return jnp.concatenate([x1 * cos - x2 * sin, x2 * cos + x1 * sin], axis=-1)


def _rope_tables(pos):
    half = RET_DK // 2
    inv = ROPE_BASE ** (-jnp.arange(half, dtype=F32) / half)
    ang = pos[:, None] * inv[None, :]
    return jnp.cos(ang), jnp.sin(ang)


def _ret_chunk(q, k, v, ga, cos, sin, dec_ref, s_ref, y_ref):
    C = q.shape[0]
    H = range(RET_HEADS)
    ti = lax.broadcasted_iota(jnp.int32, (C, C), 0)
    tj = lax.broadcasted_iota(jnp.int32, (C, C), 1)
    rel = (ti - tj).astype(F32)
    idx = lax.broadcasted_iota(jnp.int32, (C, 1), 0).astype(F32)
    lg = [dec_ref[2 * h] for h in H]
    qh = [_rope(q[:, h * RET_DK:(h + 1) * RET_DK], cos, sin) for h in H]
    kh = [_rope(k[:, h * RET_DK:(h + 1) * RET_DK], cos, sin) * (RET_DK ** -0.5) for h in H]
    vh = [v[:, h * RET_DV:(h + 1) * RET_DV].astype(BF16) for h in H]
    s = [s_ref[h] for h in H]
    inner = [_dot_nt(qh[h], kh[h]) * jnp.where(rel >= 0, jnp.exp(jnp.maximum(rel, 0.0) * lg[h]), 0.0)
             for h in H]
    o = [_dot(inner[h], vh[h]) + _dot(qh[h] * jnp.exp((idx + 1.0) * lg[h]), s[h]) for h in H]
    for h in H:
        s_ref[h] = dec_ref[2 * h + 1] * s[h] + _dot_tn(kh[h] * jnp.exp((C - 1.0 - idx) * lg[h]), vh[h])
    for h in H:
        on = o[h] * lax.rsqrt(jnp.mean(o[h] * o[h], axis=-1, keepdims=True) + NORM_EPS)
        y_ref[:, h * RET_DV:(h + 1) * RET_DV] = (
            on * _silu(ga[:, h * RET_DV:(h + 1) * RET_DV])).astype(y_ref.dtype)


def _ret_decay_table():
    log_g = jnp.log(1.0 - jnp.exp2(-5.0 - jnp.arange(RET_HEADS, dtype=F32)))
    return log_g


def _ret_step(dec_ref, qk_ref, v_ref, ga_ref, cos_ref, sin_ref, s_ref, y_ref, so_ref, first_row):
    row0 = lax.broadcasted_iota(jnp.int32, (8, 1), 0)
    cos = cos_ref[...]
    sin = sin_ref[...]
    for b, h in [(b, h) for b in range(s_ref.shape[0]) for h in range(RET_HEADS)]:
        g = dec_ref[h]
        row = pl.ds(first_row + b, 1)
        q = _rope(qk_ref[row, h * RET_DK:(h + 1) * RET_DK], cos, sin)
        k = _rope(qk_ref[row, RET_QK + h * RET_DK:RET_QK + (h + 1) * RET_DK], cos, sin) * (RET_DK ** -0.5)
        v = v_ref[row, h * RET_DV:(h + 1) * RET_DV]
        s = s_ref[b, h]
        qk = jnp.sum(q * k, axis=-1, keepdims=True)
        q8 = jnp.broadcast_to(q, (8, RET_DK))
        o = qk * v + g * _dot(q8, s)[0:1, :]
        k_hi = k.astype(BF16).astype(F32)
        k_lo = k - k_hi
        v_hi = v.astype(BF16).astype(F32)
        v_lo = v - v_hi
        k8 = jnp.where(row0 < 2, k_hi, jnp.where(row0 == 2, k_lo, 0.0))
        v8 = jnp.where((row0 == 0) | (row0 == 2), v_hi, jnp.where(row0 == 1, v_lo, 0.0))
        so_ref[b, h] = g * s + _dot_tn(k8, v8)
        o = o * lax.rsqrt(jnp.mean(o * o, axis=-1, keepdims=True) + NORM_EPS)
        y_ref[b, :, h * RET_DV:(h + 1) * RET_DV] = (
            o * _silu(ga_ref[row, h * RET_DV:(h + 1) * RET_DV])).astype(y_ref.dtype)


def _ret_step_operands(qk, v, ga, state, n_steps, step_index):
    nb = state.shape[0]
    nq = nb // n_steps
    assert nq * n_steps == nb
    g = jnp.exp(_ret_decay_table())
    cos, sin = _rope_tables(PAST_LEN + jnp.arange(1, dtype=F32))
    tab = pl.BlockSpec((1, RET_DK // 2), lambda *ids: (0, 0))
    st = pl.BlockSpec((nq, RET_HEADS, RET_DK, RET_DV), lambda *ids: (step_index(*ids), 0, 0, 0))
    args = [g, qk, v, ga, cos, sin, state]
    in_specs = [pl.BlockSpec(memory_space=pltpu.SMEM), _const_spec(qk, 2), _const_spec(v, 2),
                _const_spec(ga, 2), tab, tab, st]
    out_specs = [pl.BlockSpec((nq, 1, RET_V), lambda *ids: (step_index(*ids), 0, 0)), st]
    out_shape = [jax.ShapeDtypeStruct((nb, 1, RET_V), BF16), jax.ShapeDtypeStruct(state.shape, F32)]
    return args, in_specs, out_specs, out_shape


def _head_sums(xs, ones_bd):
    rows = xs[0].shape[0]
    stack = jnp.concatenate(
        [x[:, g * MXU_DIM:(g + 1) * MXU_DIM] for x in xs for g in range(N_GROUPS)], axis=0)
    s = jnp.dot(stack.astype(BF16), ones_bd, preferred_element_type=F32)
    return [jnp.concatenate([s[(i * N_GROUPS + g) * rows:(i * N_GROUPS + g + 1) * rows]
                             for g in range(N_GROUPS)], axis=-1) for i in range(len(xs))]


def _rwkv_prep(z, zwa, w0, w_lora, a0, k_k, k_a, ones_bd):
    r = z[:, :RW_C]
    k = z[:, RW_C:2 * RW_C]
    v = z[:, 2 * RW_C:]
    is_decay = lax.broadcasted_iota(jnp.int32, (1, 2 * RW_LORA), 1) < RW_LORA
    la = _dot(jnp.where(is_decay, jnp.tanh(zwa), zwa), w_lora)
    log_decay = -math.exp(-0.5) * _sigmoid(w0 + la[:, :RW_C])
    a = _sigmoid(a0 + la[:, RW_C:])
    kk = k * k_k
    kk = kk * lax.rsqrt(jnp.maximum(_head_sums([kk * kk], ones_bd)[0], 1e-24))
    k = k * (1.0 + (a - 1.0) * k_a)
    return r, log_decay, k, v, kk, a


def _rwkv_post(o, r, k, v, gate, r_k, ln_w, ln_b, ones_bd):
    o_sum, rk_sum = _head_sums([o, r * k * r_k], ones_bd)
    d = o - o_sum * (1.0 / RW_HEAD)
    var = _head_sums([d * d], ones_bd)[0] * (1.0 / RW_HEAD)
    on = d * lax.rsqrt(var + RW_GN_EPS) * ln_w + ln_b
    return (on + rk_sum * v) * gate


def _block_diag(x, head_of_lane):
    xb = x.astype(BF16)
    zero = jnp.zeros_like(xb)
    return jnp.concatenate(
        [jnp.where(head_of_lane == j, xb, zero) for j in range(HEADS_PER_GROUP)], axis=0)


def _rwkv_chunk_kernel(sh_ref, gb_ref, vec_ref, wl_ref, ones_ref,
                       rt_dec_ref, rt_qk_ref, rt_v_ref, rt_ga_ref, rt_cos_ref, rt_sin_ref, rt_s_ref,
                       y_ref, so_ref, rt_y_ref, rt_so_ref, state):
    c = pl.program_id(1)
    nc = pl.num_programs(1)
    step = pl.program_id(0) * nc + c
    _ret_step(rt_dec_ref, rt_qk_ref, rt_v_ref, rt_ga_ref, rt_cos_ref, rt_sin_ref, rt_s_ref,
              rt_y_ref, rt_so_ref, step * rt_s_ref.shape[0])
    C = RW_CHUNK
    n_seq = sh_ref.shape[0]

    @pl.when(c == 0)
    def _():
        state[...] = jnp.zeros_like(state)

    ones_bd = ones_ref[...]
    vec = {n: vec_ref[i:i + 1, :] for i, n in enumerate(RW_VEC_ROWS)}
    R = n_seq * C
    z = sh_ref[...].reshape(R, sh_ref.shape[-1]).astype(F32)
    r, lw, k, v, kk, a = _rwkv_prep(z[:, :3 * RW_C], z[:, 3 * RW_C:], vec["w0"], wl_ref[...],
                                    vec["a0"], vec["k_k"], vec["k_a"], ones_bd)
    ti = lax.broadcasted_iota(jnp.int32, (R, R), 0)
    tj = lax.broadcasted_iota(jnp.int32, (R, R), 1)
    chunk_shift = C.bit_length() - 1
    tri = ((ti >= tj) & ((ti >> chunk_shift) == (tj >> chunk_shift))).astype(BF16)
    lw_hi = lw.astype(BF16)
    lw_lo = (lw - lw_hi.astype(F32)).astype(BF16)
    cl = jnp.dot(jnp.concatenate([tri, tri], axis=1), jnp.concatenate([lw_hi, lw_lo], axis=0),
                 preferred_element_type=F32)
    g_chunk = [jnp.exp(cl[(q + 1) * C - 1:(q + 1) * C, :]) for q in range(n_seq)]
    g_rows = jnp.concatenate([jnp.broadcast_to(g, (C, RW_C)) for g in g_chunk], axis=0)
    e_neg = jnp.exp(-cl)
    e_rem = e_neg * g_rows
    beta = a * kk
    b16 = lambda x: x.astype(BF16)
    tok = dict(r=r, k=k, v=v, ag=b16(-kk * jnp.exp(cl - lw)), rg=b16(r * jnp.exp(cl)),
               bg=b16(beta * e_neg), kg=b16(k * e_neg), bg_c=b16(beta * e_rem), kg_c=b16(k * e_rem))

    lane = lax.broadcasted_iota(jnp.int32, (1, MXU_DIM), 1)
    head_of_lane = lane >> HEAD_SHIFT
    t_col = lax.broadcasted_iota(jnp.int32, (C, MXU_DIM), 0)
    i_lane = lax.broadcasted_iota(jnp.int32, (C, MXU_DIM), 1) & (RW_HEAD - 1)
    strict = t_col > i_lane
    incl = t_col >= i_lane
    eye = (t_col == i_lane).astype(F32)
    vrow_head = lax.broadcasted_iota(jnp.int32, (MXU_DIM, MXU_DIM), 0) >> HEAD_SHIFT
    klane_head = lax.broadcasted_iota(jnp.int32, (MXU_DIM, MXU_DIM), 1) >> HEAD_SHIFT
    same_head = vrow_head == klane_head

    units = [(q, g) for q in range(n_seq) for g in range(N_GROUPS)]
    U = range(len(units))
    bd = lambda x: _block_diag(x, head_of_lane)
    grp = lambda name, u: tok[name][units[u][0] * C:(units[u][0] + 1) * C,
                                    units[u][1] * MXU_DIM:(units[u][1] + 1) * MXU_DIM]
    s_bd = [state[q, g] for q, g in units]
    lhs = [jnp.concatenate([grp("ag", u), grp("rg", u)], axis=0) for u in U]
    rhs = [jnp.concatenate([bd(grp("bg", u)), bd(grp("kg", u)), s_bd[u].astype(BF16)], axis=0) for u in U]
    abs_ = [_dot_nt(lhs[u], rhs[u]) for u in U]
    ab = [abs_[u][:, :MXU_DIM] for u in U]
    ak = [abs_[u][:, MXU_DIM:2 * MXU_DIM] for u in U]
    sv = [abs_[u][:, 2 * MXU_DIM:] for u in U]
    n_pow = [jnp.where(strict, ab[u][:C], 0.0) for u in U]
    a_ak = [jnp.where(strict, ak[u][:C], 0.0) for u in U]
    a_rb = [jnp.where(incl, ab[u][C:], 0.0) for u in U]
    a_rk = [jnp.where(incl, ak[u][C:], 0.0) for u in U]
    v_bd = [bd(grp("v", u)) for u in U]
    av = [_dot(jnp.concatenate([a_ak[u], a_rk[u]], axis=0), v_bd[u]) for u in U]
    t_inv = [eye + n_pow[u] for u in U]
    n_pow = [_dot(n_pow[u], bd(n_pow[u])) for u in U]
    for _ in range(int(math.log2(C)) - 2):
        prod = [_dot(jnp.concatenate([n_pow[u], t_inv[u]], axis=0), bd(n_pow[u])) for u in U]
        n_pow = [prod[u][:C] for u in U]
        t_inv = [t_inv[u] + prod[u][C:] for u in U]
    t_inv = [t_inv[u] + _dot(t_inv[u], bd(n_pow[u])) for u in U]
    p = [_dot(t_inv[u], bd(sv[u][:C] + av[u][:C])) for u in U]
    o = [sv[u][C:] + _dot(a_rb[u], bd(p[u])) + av[u][C:] for u in U]
    for u, (q, g) in enumerate(units):
        upd = _dot_tn(jnp.concatenate([p[u], grp("v", u)], axis=0),
                      jnp.concatenate([grp("bg_c", u), grp("kg_c", u)], axis=0))
        state[q, g] = (s_bd[u] * g_chunk[q][:, g * MXU_DIM:(g + 1) * MXU_DIM]
                       + jnp.where(same_head, upd, 0.0))
    o_all = jnp.concatenate([jnp.concatenate(o[q * N_GROUPS:(q + 1) * N_GROUPS], axis=-1)
                             for q in range(n_seq)], axis=0)
    gate = gb_ref[...].reshape(R, RW_C).astype(F32)
    y = _rwkv_post(o_all, r, k, v, gate, vec["r_k"], vec["ln_w"], vec["ln_b"], ones_bd)
    y_ref[...] = y.reshape(n_seq, C, RW_C).astype(y_ref.dtype)

    @pl.when(c == nc - 1)
    def _():
        for q in range(n_seq):
            for g in range(N_GROUPS):
                for j in range(HEADS_PER_GROUP):
                    blk = slice(j * RW_HEAD, (j + 1) * RW_HEAD)
                    so_ref[q, g * HEADS_PER_GROUP + j] = state[q, g, blk, blk]


RW_VEC_ROWS = ("mu_r", "mu_k", "mu_v", "w0", "a0", "k_k", "k_a", "r_k", "ln_w", "ln_b")


def _rwkv_consts(p):
    mu = p["rw_mu"]
    vec = jnp.stack([mu[:RW_C], mu[RW_C:2 * RW_C], mu[2 * RW_C:3 * RW_C], p["rw_w0"], p["rw_a0"],
                     p["rw_k_k"], p["rw_k_a"], p["rw_r_k"].reshape(-1), p["rw_ln_w"],
                     p["rw_ln_b"]]).astype(F32)
    zeros = jnp.zeros((RW_LORA, RW_C), F32)
    w_lora = jnp.concatenate([jnp.concatenate([p["rw_w2"], zeros], axis=1),
                              jnp.concatenate([zeros, p["rw_a2"]], axis=1)], axis=0).astype(BF16)
    hl = jnp.arange(MXU_DIM) // RW_HEAD
    ones_bd = (hl[:, None] == hl[None, :]).astype(BF16)
    return dict(vec=vec, mu_wa=mu[3 * RW_C:].reshape(1, -1).astype(F32), w_lora=w_lora,
                ones_bd=ones_bd)


def _const_spec(arr, ngrid):
    zeros = (0,) * arr.ndim
    if ngrid == 1:
        return pl.BlockSpec(arr.shape, lambda i: zeros, pipeline_mode=pl.Buffered(1))
    return pl.BlockSpec(arr.shape, lambda i, j: zeros, pipeline_mode=pl.Buffered(1))


def _rwkv_prompt(sh, gb, cs, batch, seq, ret_sample):
    C = RW_CHUNK
    nq = RW_SEQS_PER_STEP
    nc = seq // C
    n_sh = 3 * RW_C + 2 * RW_LORA
    blk = lambda n: pl.BlockSpec((nq, C, n), lambda b, c: (b, c, 0))
    consts = [cs[n] for n in ("vec", "w_lora", "ones_bd")]
    rt_args, rt_in, rt_out, rt_shape = _ret_step_operands(
        *ret_sample, n_steps=(batch // nq) * nc, step_index=lambda b, c: b * nc + c)
    y, s, rt_y, rt_s = pl.pallas_call(
        _rwkv_chunk_kernel,
        grid=(batch // nq, nc),
        in_specs=[blk(n_sh), blk(RW_C)] + [_const_spec(a, 2) for a in consts] + rt_in,
        out_specs=[blk(RW_C),
                   pl.BlockSpec((nq, RW_HEADS, RW_HEAD, RW_HEAD), lambda b, c: (b, 0, 0, 0))] + rt_out,
        out_shape=[jax.ShapeDtypeStruct((batch, seq, RW_C), BF16),
                   jax.ShapeDtypeStruct((batch, RW_HEADS, RW_HEAD, RW_HEAD), F32)] + rt_shape,
        scratch_shapes=[pltpu.VMEM((nq, N_GROUPS, MXU_DIM, MXU_DIM), F32)],
        compiler_params=_params("arbitrary", "arbitrary"),
        name="rwkv_chunk",
    )(sh.reshape(batch, seq, n_sh), gb.reshape(batch, seq, RW_C), *consts, *rt_args)
    return y.reshape(batch * seq, RW_C), s, rt_y.reshape(rt_y.shape[0], RET_V), rt_s


def _rwkv_step_kernel(r_ref, k_ref, v_ref, wa_ref, gb_ref, s_ref, col_ref, mu_wa_ref, w2t_ref, a2t_ref,
                      y_ref, so_ref, o_scr):
    nb = y_ref.shape[-1]
    col = {n: col_ref[:, i:i + 1] for i, n in enumerate(RW_VEC_ROWS)}
    lerp = lambda ref, mu: ref[:, :nb] + (ref[:, nb:2 * nb] - ref[:, :nb]) * mu
    r = lerp(r_ref, col["mu_r"])
    k = lerp(k_ref, col["mu_k"])
    v = lerp(v_ref, col["mu_v"])
    zwa = lerp(wa_ref, mu_wa_ref[...])
    wpre = col["w0"] + _dot(w2t_ref[...], jnp.tanh(zwa[:RW_LORA]))
    decay = jnp.exp(-math.exp(-0.5) * _sigmoid(wpre))
    a = _sigmoid(col["a0"] + _dot(a2t_ref[...], zwa[RW_LORA:]))
    kk = k * col["k_k"]
    kk = kk * lax.rsqrt(jnp.maximum(jnp.sum(kk * kk, axis=0, keepdims=True), 1e-24))
    k = k * (1.0 + (a - 1.0) * col["k_a"])
    beta = a * kk
    for i in range(RW_HEAD):
        s = s_ref[i]
        sk = jnp.sum(s * kk, axis=0, keepdims=True)
        s_new = s * decay - sk * beta + v[i:i + 1, :] * k
        so_ref[i] = s_new
        o_scr[i:i + 1, :] = jnp.sum(s_new * r, axis=0, keepdims=True)
    o = o_scr[...]
    d = o - jnp.mean(o, axis=0, keepdims=True)
    var = jnp.mean(d * d, axis=0, keepdims=True)
    on = d * lax.rsqrt(var + RW_GN_EPS) * col["ln_w"] + col["ln_b"]
    bonus = jnp.sum(r * k * col["r_k"], axis=0, keepdims=True) * v
    y_ref[...] = ((on + bonus) * _silu(gb_ref[:, :nb])).astype(y_ref.dtype)


def _rwkv_sample(sh2, gb2, p, cs, state):
    nb = state.shape[1]
    sht = sh2.T
    gbt = gb2.T
    st = jnp.transpose(state, (0, 2, 3, 4, 1))
    n_head_blocks = RW_C // RW_HEAD
    rows = lambda off: pl.BlockSpec((RW_HEAD, 2 * nb), lambda h: (h + off, 0))
    lora_blk = pl.BlockSpec((2 * RW_LORA, 2 * nb), lambda h: (3 * RW_C // (2 * RW_LORA), 0))
    st_blk = pl.BlockSpec((None, None, RW_HEAD, RW_HEAD, nb), lambda h: (0, h, 0, 0, 0))
    wt_blk = pl.BlockSpec((RW_HEAD, RW_LORA), lambda h: (h, 0))
    yt, so = pl.pallas_call(
        _rwkv_step_kernel,
        grid=(RW_HEADS,),
        in_specs=[rows(0), rows(n_head_blocks), rows(2 * n_head_blocks), lora_blk, rows(0), st_blk,
                  pl.BlockSpec((RW_HEAD, len(RW_VEC_ROWS)), lambda h: (h, 0)),
                  pl.BlockSpec((2 * RW_LORA, 1), lambda h: (0, 0)), wt_blk, wt_blk],
        out_specs=[pl.BlockSpec((RW_HEAD, nb), lambda h: (h, 0)), st_blk],
        out_shape=[jax.ShapeDtypeStruct((RW_C, nb), BF16), jax.ShapeDtypeStruct(st.shape, F32)],
        scratch_shapes=[pltpu.VMEM((RW_HEAD, nb), F32)],
        compiler_params=_params("arbitrary"),
        name="rwkv_step",
    )(sht, sht, sht, sht, gbt, st, cs["vec"].T, cs["mu_wa"].reshape(-1, 1),
      p["rw_w2"].T.astype(BF16), p["rw_a2"].T.astype(BF16))
    return yt.T, jnp.transpose(so, (0, 4, 1, 2, 3))


def _tail_kernel(ya_ref, yb_ref, m_ref, x_ref, p_ref, wda_ref, wdb_ref, wout_ref, wple_ref, wgate_ref,
                 pg_ref, fg_ref, y_ref):
    m = m_ref[...].astype(F32)
    merged = (_sigmoid(m[:, :D_MODEL]) * jnp.dot(ya_ref[...], wda_ref[...], preferred_element_type=F32)
              + _sigmoid(m[:, D_MODEL:]) * jnp.dot(yb_ref[...], wdb_ref[...], preferred_element_type=F32))
    x = x_ref[...] + _dot(merged, wout_ref[...])
    gate = _sigmoid(_dot(_rms(x, pg_ref[...]), wgate_ref[...]))
    x = x + _dot(p_ref[...], wple_ref[...]) * gate
    y_ref[...] = _rms(x, fg_ref[...])


def _tail(ya, yb, m, x, p, w, tm):
    rows = x.shape[0]
    tile = lambda n: pl.BlockSpec((tm, n), lambda i: (i, 0))
    consts = [w["wda"], w["wdb"], w["wout"], w["wple"], w["wgate"], w["ple_g"], w["final_g"]]
    return pl.pallas_call(
        _tail_kernel,
        grid=(rows // tm,),
        in_specs=[tile(RET_V), tile(RW_C), tile(2 * D_MODEL), tile(D_MODEL), tile(PLE_DIM)]
        + [_const_spec(a, 1) for a in consts],
        out_specs=tile(D_MODEL),
        out_shape=jax.ShapeDtypeStruct((rows, D_MODEL), F32),
        compiler_params=_params("arbitrary"),
        name="tail",
    )(ya, yb, m, x, p, *consts)


def _layer_weights(p):
    return dict(
        wda=p["w_down_a"].astype(BF16), wdb=p["w_down_b"].astype(BF16), wout=p["w_out"].astype(BF16),
        wple=p["w_ple"].astype(BF16), wgate=p["w_ple_gate"].astype(BF16),
        ple_g=p["ple_norm_g"].reshape(1, -1), final_g=p["final_norm_g"].reshape(1, -1),
    )


def _layer_paths(x_p, pe_p, x_s, h_prev, s_ret, s_rw, pe_s, p, w, cs):
    batch, seq, d = x_p.shape
    rows = batch * seq
    nb = x_s.shape[0]
    xp2 = x_p.reshape(rows, d)
    xs2 = x_s.reshape(nb, d)
    shift_p = _rmsnorm(x_p[:, -1, :], p["norm_g"], F32, batch)
    h_s = _rmsnorm(xs2, p["norm_g"], F32, nb)
    hcat = jnp.concatenate([h_s, h_prev], axis=0)
    w_in, (qk_s, v_s, ga_s, sh_s, gb_s, m_s) = _sample_proj(hcat, p["w_in"])
    ya_p, sh, gb, m, ret_p = _in_proj_retention(xp2, p["norm_g"], w_in, p["rw_mu"].reshape(1, -1),
                                                batch, seq, PROMPT_PROJ_ROWS)
    yb_p, rw_p, ya_s, ret_s = _rwkv_prompt(sh, gb, cs, batch, seq, (qk_s, v_s, ga_s, s_ret))
    y_p = _tail(ya_p, yb_p, m, xp2, pe_p.reshape(rows, PLE_DIM), w, PROMPT_TAIL_ROWS)
    yb_s, rw_s = _rwkv_sample(sh_s, gb_s, p, cs, s_rw)
    y_s = _tail(ya_s, yb_s, m_s, xs2, pe_s.reshape(nb, PLE_DIM), w, nb)
    return (y_p.reshape(batch, seq, d), shift_p, ret_p, rw_p,
            y_s.reshape(nb, 1, d), h_s, ret_s, rw_s)


def kernel(x_prompt, x_sample, state_ret, state_rwkv, state_shift, p_prompt, p_sample, norm_g, w_in, rw_mu, rw_w0, rw_w2, rw_a0, rw_a2, rw_k_k, rw_k_a, rw_r_k, rw_ln_w, rw_ln_b, w_down_a, w_down_b, w_out, w_ple, ple_norm_g, w_ple_gate, final_norm_g):
    assert norm_g.shape[0] == 1, "single-layer step"
    p = dict(norm_g=norm_g[0], w_in=w_in[0], rw_mu=rw_mu[0], rw_w0=rw_w0[0], rw_w2=rw_w2[0],
             rw_a0=rw_a0[0], rw_a2=rw_a2[0], rw_k_k=rw_k_k[0], rw_k_a=rw_k_a[0], rw_r_k=rw_r_k[0],
             rw_ln_w=rw_ln_w[0], rw_ln_b=rw_ln_b[0], w_down_a=w_down_a[0], w_down_b=w_down_b[0],
             w_out=w_out[0], w_ple=w_ple[0], ple_norm_g=ple_norm_g[0], w_ple_gate=w_ple_gate[0],
             final_norm_g=final_norm_g)
    w = _layer_weights(p)
    cs = _rwkv_consts(p)
    y_p, sh_p, ret_p, rw_p, y_s, sh_s, ret_s, rw_s = _layer_paths(
        x_prompt, p_prompt[0], x_sample, state_shift[0], state_ret[0], state_rwkv, p_sample[0], p, w, cs)
    return (y_p, y_s, ret_p[None], rw_p[None], sh_p[None], ret_s[None], rw_s, sh_s[None])
```

```python
import math

import jax
import jax.numpy as jnp
from jax import lax
from jax.experimental import pallas as pl
from jax.experimental.pallas import tpu as pltpu

F32 = jnp.float32
BF16 = jnp.bfloat16

D_MODEL = 1024
RET_HEADS = 4
RET_DK = 256
RET_DV = 512
RET_QK = RET_HEADS * RET_DK
RET_V = RET_HEADS * RET_DV
ROPE_BASE = 10000.0
RW_HEAD = 64
RW_HEADS = D_MODEL // RW_HEAD
RW_C = RW_HEADS * RW_HEAD
RW_LORA = 64
RW_GN_EPS = 1e-5 * RW_HEAD
RW_CHUNK = 64
PLE_DIM = 256
NORM_EPS = 1e-6
PAST_LEN = 16384

MXU_DIM = 256
HEADS_PER_GROUP = MXU_DIM // RW_HEAD
N_GROUPS = RW_C // MXU_DIM
HEAD_SHIFT = RW_HEAD.bit_length() - 1
VMEM_LIMIT_BYTES = 56 * 1024 * 1024
PROMPT_PROJ_ROWS = 256
PROMPT_TAIL_ROWS = 512
RW_SEQS_PER_STEP = 4


def _params(*sem):
    return pltpu.CompilerParams(dimension_semantics=sem, vmem_limit_bytes=VMEM_LIMIT_BYTES)


def _dot(a, b):
    return jnp.dot(a.astype(BF16), b.astype(BF16), preferred_element_type=F32)


def _dot_nt(a, b):
    return lax.dot_general(a.astype(BF16), b.astype(BF16), (((1,), (1,)), ((), ())),
                           preferred_element_type=F32)


def _dot_tn(a, b):
    return lax.dot_general(a.astype(BF16), b.astype(BF16), (((0,), (0,)), ((), ())),
                           preferred_element_type=F32)


def _sigmoid(x):
    return 0.5 * jnp.tanh(0.5 * x) + 0.5


def _silu(x):
    return x * _sigmoid(x)


def _rms(x, g):
    return x * lax.rsqrt(jnp.mean(x * x, axis=-1, keepdims=True) + NORM_EPS) * g


PROJ_WIDTHS = (2 * RET_QK, RET_V, RET_V, 3 * RW_C + 2 * RW_LORA, RW_C, 2 * D_MODEL)


SAMPLE_PROJ_K_ROWS = 128


def _sample_proj_kernel(x_ref, xk_ref, hp_ref, g_ref, w_ref, wb_ref, h_ref, *rest):
    out_refs, scale = rest[:-1], rest[-1]

    @pl.when(pl.program_id(0) == 0)
    def _():
        x = x_ref[...]
        scale[...] = lax.rsqrt(jnp.mean(x * x, axis=-1, keepdims=True) + NORM_EPS)
        for o_ref in out_refs:
            o_ref[...] = jnp.zeros_like(o_ref)

    wb = w_ref[...].astype(BF16)
    wb_ref[...] = wb
    h = xk_ref[...] * scale[...] * g_ref[...]
    h_ref[...] = h
    rows = jnp.concatenate([h, hp_ref[...]], axis=0).astype(BF16)
    off = 0
    for o_ref, n in zip(out_refs, PROJ_WIDTHS):
        o_ref[...] += jnp.dot(rows, wb[:, off:off + n], preferred_element_type=F32)
        off += n


def _sample_proj(x, h_prev, g, w_in):
    nb, d = x.shape
    n_all = w_in.shape[1]
    tk = SAMPLE_PROJ_K_ROWS
    slab = pl.BlockSpec((nb, tk), lambda k: (0, k))
    outs = pl.pallas_call(
        _sample_proj_kernel,
        grid=(d // tk,),
        in_specs=[_const_spec(x, 1), slab, slab, pl.BlockSpec((1, tk), lambda k: (0, k)),
                  pl.BlockSpec((tk, n_all), lambda k: (k, 0))],
        out_specs=[pl.BlockSpec((tk, n_all), lambda k: (k, 0)), slab]
        + [pl.BlockSpec((2 * nb, n), lambda k: (0, 0)) for n in PROJ_WIDTHS],
        out_shape=[jax.ShapeDtypeStruct((d, n_all), BF16), jax.ShapeDtypeStruct((nb, d), F32)]
        + [jax.ShapeDtypeStruct((2 * nb, n), F32) for n in PROJ_WIDTHS],
        scratch_shapes=[pltpu.VMEM((nb, 1), F32)],
        compiler_params=_params("arbitrary"),
        name="sample_proj",
    )(x, x, h_prev, g.reshape(1, d), w_in)
    return outs[0], outs[1], outs[2:]


def _in_proj_ret_kernel(dec_ref, x_ref, g_ref, w_ref, cos_ref, sin_ref, mu_ref,
                        ya_ref, sh_ref, gb_ref, m_ref, s_ref, hl_ref, carry):
    @pl.when(pl.program_id(1) == 0)
    def _():
        s_ref[...] = jnp.zeros_like(s_ref)
        carry[...] = jnp.zeros_like(carry)

    tm = x_ref.shape[0]
    h32 = _rms(x_ref[...], g_ref[...])
    hl_ref[...] = h32[tm - 1:tm, :]
    h = h32.astype(BF16)
    o_qk, o_v, o_ga, o_sh, o_gb, o_m = [sum(PROJ_WIDTHS[:i]) for i in range(len(PROJ_WIDTHS))]
    proj = lambda off, n: jnp.dot(h, w_ref[:, off:off + n], preferred_element_type=F32)
    qk = proj(o_qk, 2 * RET_QK)
    v = proj(o_v, RET_V)
    ga = proj(o_ga, RET_V)
    sh = proj(o_sh, PROJ_WIDTHS[3])
    row = lax.broadcasted_iota(jnp.int32, (tm, 1), 0)
    sh_prev = jnp.where(row == 0, carry[...], pltpu.roll(sh, 1, 0))
    carry[...] = sh[tm - 1:tm, :]
    sh_ref[...] = (sh + (sh_prev - sh) * mu_ref[...]).astype(sh_ref.dtype)
    gb_ref[...] = _silu(proj(o_gb, PROJ_WIDTHS[4])).astype(gb_ref.dtype)
    m_ref[...] = proj(o_m, PROJ_WIDTHS[5]).astype(m_ref.dtype)
    _ret_chunk(qk[:, :RET_QK], qk[:, RET_QK:], v, ga, cos_ref[...], sin_ref[...], dec_ref, s_ref, ya_ref)


def _in_proj_retention(x, g, w_in, mu, batch, seq, tm):
    m, d = x.shape
    nt = seq // tm
    n_all = w_in.shape[1]
    log_g = _ret_decay_table()
    dec = jnp.stack([log_g, jnp.exp(tm * log_g)], axis=1).reshape(-1)
    half = RET_DK // 2
    cos, sin = _rope_tables(jnp.arange(seq, dtype=F32))
    row = lambda b, t: (b * nt + t, 0)
    widths = (RET_V,) + PROJ_WIDTHS[3:]
    return pl.pallas_call(
        _in_proj_ret_kernel,
        grid=(batch, nt),
        in_specs=[pl.BlockSpec(memory_space=pltpu.SMEM),
                  pl.BlockSpec((tm, d), row), pl.BlockSpec((1, d), lambda b, t: (0, 0)),
                  pl.BlockSpec((d, n_all), lambda b, t: (0, 0), pipeline_mode=pl.Buffered(1)),
                  pl.BlockSpec((tm, half), lambda b, t: (t, 0)),
                  pl.BlockSpec((tm, half), lambda b, t: (t, 0)), _const_spec(mu, 2)],
        out_specs=[pl.BlockSpec((tm, n), row) for n in widths]
        + [pl.BlockSpec((None, RET_HEADS, RET_DK, RET_DV), lambda b, t: (b, 0, 0, 0)),
           pl.BlockSpec((None, 1, d), lambda b, t: (b, 0, 0))],
        out_shape=[jax.ShapeDtypeStruct((m, n), BF16) for n in widths]
        + [jax.ShapeDtypeStruct((batch, RET_HEADS, RET_DK, RET_DV), F32),
           jax.ShapeDtypeStruct((batch, 1, d), F32)],
        scratch_shapes=[pltpu.VMEM((1, PROJ_WIDTHS[3]), F32)],
        compiler_params=_params("arbitrary", "arbitrary"),
        name="in_proj_retention",
    )(dec, x, g.reshape(1, d), w_in, cos, sin, mu)


def _rope(x, cos, sin):
    half = x.shape[-1] // 2
    x1, x2 = x[:, :half], x[:, half:]
    return jnp.concatenate([x1 * cos - x2 * sin, x2 * cos + x1 * sin], axis=-1)


def _rope_tables(pos):
    half = RET_DK // 2
    inv = ROPE_BASE ** (-jnp.arange(half, dtype=F32) / half)
    ang = pos[:, None] * inv[None, :]
    return jnp.cos(ang), jnp.sin(ang)


def _ret_chunk(q, k, v, ga, cos, sin, dec_ref, s_ref, y_ref):
    C = q.shape[0]
    H = range(RET_HEADS)
    ti = lax.broadcasted_iota(jnp.int32, (C, C), 0)
    tj = lax.broadcasted_iota(jnp.int32, (C, C), 1)
    rel = (ti - tj).astype(F32)
    idx = lax.broadcasted_iota(jnp.int32, (C, 1), 0).astype(F32)
    lg = [dec_ref[2 * h] for h in H]
    qh = [_rope(q[:, h * RET_DK:(h + 1) * RET_DK], cos, sin) for h in H]
    kh = [_rope(k[:, h * RET_DK:(h + 1) * RET_DK], cos, sin) * (RET_DK ** -0.5) for h in H]
    vh = [v[:, h * RET_DV:(h + 1) * RET_DV].astype(BF16) for h in H]
    s = [s_ref[h] for h in H]
    inner = [_dot_nt(qh[h], kh[h]) * jnp.where(rel >= 0, jnp.exp(jnp.maximum(rel, 0.0) * lg[h]), 0.0)
             for h in H]
    o = [_dot(inner[h], vh[h]) + _dot(qh[h] * jnp.exp((idx + 1.0) * lg[h]), s[h]) for h in H]
    for h in H:
        s_ref[h] = dec_ref[2 * h + 1] * s[h] + _dot_tn(kh[h] * jnp.exp((C - 1.0 - idx) * lg[h]), vh[h])
    for h in H:
        on = o[h] * lax.rsqrt(jnp.mean(o[h] * o[h], axis=-1, keepdims=True) + NORM_EPS)
        y_ref[:, h * RET_DV:(h + 1) * RET_DV] = (
            on * _silu(ga[:, h * RET_DV:(h + 1) * RET_DV])).astype(y_ref.dtype)


def _ret_decay_table():
    log_g = jnp.log(1.0 - jnp.exp2(-5.0 - jnp.arange(RET_HEADS, dtype=F32)))
    return log_g


def _ret_step(dec_ref, qk_ref, v_ref, ga_ref, cos_ref, sin_ref, s_ref, y_ref, so_ref, first_row):
    row0 = lax.broadcasted_iota(jnp.int32, (8, 1), 0)
    cos = cos_ref[...]
    sin = sin_ref[...]
    for b, h in [(b, h) for b in range(s_ref.shape[0]) for h in range(RET_HEADS)]:
        g = dec_ref[h]
        row = pl.ds(first_row + b, 1)
        q = _rope(qk_ref[row, h * RET_DK:(h + 1) * RET_DK], cos, sin)
        k = _rope(qk_ref[row, RET_QK + h * RET_DK:RET_QK + (h + 1) * RET_DK], cos, sin) * (RET_DK ** -0.5)
        v = v_ref[row, h * RET_DV:(h + 1) * RET_DV]
        s = s_ref[b, h]
        qk = jnp.sum(q * k, axis=-1, keepdims=True)
        q8 = jnp.broadcast_to(q, (8, RET_DK))
        o = qk * v + g * _dot(q8, s)[0:1, :]
        k_hi = k.astype(BF16).astype(F32)
        k_lo = k - k_hi
        v_hi = v.astype(BF16).astype(F32)
        v_lo = v - v_hi
        k8 = jnp.where(row0 < 2, k_hi, jnp.where(row0 == 2, k_lo, 0.0))
        v8 = jnp.where((row0 == 0) | (row0 == 2), v_hi, jnp.where(row0 == 1, v_lo, 0.0))
        so_ref[b, h] = g * s + _dot_tn(k8, v8)
        o = o * lax.rsqrt(jnp.mean(o * o, axis=-1, keepdims=True) + NORM_EPS)
        y_ref[b, :, h * RET_DV:(h + 1) * RET_DV] = (
            o * _silu(ga_ref[row, h * RET_DV:(h + 1) * RET_DV])).astype(y_ref.dtype)


def _ret_step_operands(qk, v, ga, state, n_steps, step_index):
    nb = state.shape[0]
    nq = nb // n_steps
    assert nq * n_steps == nb
    g = jnp.exp(_ret_decay_table())
    cos, sin = _rope_tables(PAST_LEN + jnp.arange(1, dtype=F32))
    tab = pl.BlockSpec((1, RET_DK // 2), lambda *ids: (0, 0))
    st = pl.BlockSpec((nq, RET_HEADS, RET_DK, RET_DV), lambda *ids: (step_index(*ids), 0, 0, 0))
    args = [g, qk, v, ga, cos, sin, state]
    in_specs = [pl.BlockSpec(memory_space=pltpu.SMEM), _const_spec(qk, 2), _const_spec(v, 2),
                _const_spec(ga, 2), tab, tab, st]
    out_specs = [pl.BlockSpec((nq, 1, RET_V), lambda *ids: (step_index(*ids), 0, 0)), st]
    out_shape = [jax.ShapeDtypeStruct((nb, 1, RET_V), BF16), jax.ShapeDtypeStruct(state.shape, F32)]
    return args, in_specs, out_specs, out_shape


def _head_sums(xs, ones_bd):
    rows = xs[0].shape[0]
    stack = jnp.concatenate(
        [x[:, g * MXU_DIM:(g + 1) * MXU_DIM] for x in xs for g in range(N_GROUPS)], axis=0)
    s = jnp.dot(stack.astype(BF16), ones_bd, preferred_element_type=F32)
    return [jnp.concatenate([s[(i * N_GROUPS + g) * rows:(i * N_GROUPS + g + 1) * rows]
                             for g in range(N_GROUPS)], axis=-1) for i in range(len(xs))]


def _rwkv_prep(z, zwa, w0, w_lora, a0, k_k, k_a, ones_bd):
    r = z[:, :RW_C]
    k = z[:, RW_C:2 * RW_C]
    v = z[:, 2 * RW_C:]
    is_decay = lax.broadcasted_iota(jnp.int32, (1, 2 * RW_LORA), 1) < RW_LORA
    la = _dot(jnp.where(is_decay, jnp.tanh(zwa), zwa), w_lora)
    log_decay = -math.exp(-0.5) * _sigmoid(w0 + la[:, :RW_C])
    a = _sigmoid(a0 + la[:, RW_C:])
    kk = k * k_k
    kk = kk * lax.rsqrt(jnp.maximum(_head_sums([kk * kk], ones_bd)[0], 1e-24))
    k = k * (1.0 + (a - 1.0) * k_a)
    return r, log_decay, k, v, kk, a


def _rwkv_post(o, r, k, v, gate, r_k, ln_w, ln_b, ones_bd):
    o_sum, rk_sum = _head_sums([o, r * k * r_k], ones_bd)
    d = o - o_sum * (1.0 / RW_HEAD)
    var = _head_sums([d * d], ones_bd)[0] * (1.0 / RW_HEAD)
    on = d * lax.rsqrt(var + RW_GN_EPS) * ln_w + ln_b
    return (on + rk_sum * v) * gate


def _block_diag(x, head_of_lane):
    xb = x.astype(BF16)
    zero = jnp.zeros_like(xb)
    return jnp.concatenate(
        [jnp.where(head_of_lane == j, xb, zero) for j in range(HEADS_PER_GROUP)], axis=0)


def _rwkv_chunk_kernel(sh_ref, gb_ref, vec_ref, wl_ref, ones_ref,
                       rt_dec_ref, rt_qk_ref, rt_v_ref, rt_ga_ref, rt_cos_ref, rt_sin_ref, rt_s_ref,
                       y_ref, so_ref, rt_y_ref, rt_so_ref, state):
    c = pl.program_id(1)
    nc = pl.num_programs(1)
    step = pl.program_id(0) * nc + c
    _ret_step(rt_dec_ref, rt_qk_ref, rt_v_ref, rt_ga_ref, rt_cos_ref, rt_sin_ref, rt_s_ref,
              rt_y_ref, rt_so_ref, step * rt_s_ref.shape[0])
    C = RW_CHUNK
    n_seq = sh_ref.shape[0]

    @pl.when(c == 0)
    def _():
        state[...] = jnp.zeros_like(state)

    ones_bd = ones_ref[...]
    vec = {n: vec_ref[i:i + 1, :] for i, n in enumerate(RW_VEC_ROWS)}
    R = n_seq * C
    z = sh_ref[...].reshape(R, sh_ref.shape[-1]).astype(F32)
    r, lw, k, v, kk, a = _rwkv_prep(z[:, :3 * RW_C], z[:, 3 * RW_C:], vec["w0"], wl_ref[...],
                                    vec["a0"], vec["k_k"], vec["k_a"], ones_bd)
    ti = lax.broadcasted_iota(jnp.int32, (R, R), 0)
    tj = lax.broadcasted_iota(jnp.int32, (R, R), 1)
    chunk_shift = C.bit_length() - 1
    tri = ((ti >= tj) & ((ti >> chunk_shift) == (tj >> chunk_shift))).astype(BF16)
    lw_hi = lw.astype(BF16)
    lw_lo = (lw - lw_hi.astype(F32)).astype(BF16)
    cl = jnp.dot(jnp.concatenate([tri, tri], axis=1), jnp.concatenate([lw_hi, lw_lo], axis=0),
                 preferred_element_type=F32)
    g_chunk = [jnp.exp(cl[(q + 1) * C - 1:(q + 1) * C, :]) for q in range(n_seq)]
    g_rows = jnp.concatenate([jnp.broadcast_to(g, (C, RW_C)) for g in g_chunk], axis=0)
    e_neg = jnp.exp(-cl)
    e_rem = e_neg * g_rows
    beta = a * kk
    b16 = lambda x: x.astype(BF16)
    tok = dict(r=r, k=k, v=v, ag=b16(-kk * jnp.exp(cl - lw)), rg=b16(r * jnp.exp(cl)),
               bg=b16(beta * e_neg), kg=b16(k * e_neg), bg_c=b16(beta * e_rem), kg_c=b16(k * e_rem))

    lane = lax.broadcasted_iota(jnp.int32, (1, MXU_DIM), 1)
    head_of_lane = lane >> HEAD_SHIFT
    t_col = lax.broadcasted_iota(jnp.int32, (C, MXU_DIM), 0)
    i_lane = lax.broadcasted_iota(jnp.int32, (C, MXU_DIM), 1) & (RW_HEAD - 1)
    strict = t_col > i_lane
    incl = t_col >= i_lane
    eye = (t_col == i_lane).astype(F32)
    vrow_head = lax.broadcasted_iota(jnp.int32, (MXU_DIM, MXU_DIM), 0) >> HEAD_SHIFT
    klane_head = lax.broadcasted_iota(jnp.int32, (MXU_DIM, MXU_DIM), 1) >> HEAD_SHIFT
    same_head = vrow_head == klane_head

    units = [(q, g) for q in range(n_seq) for g in range(N_GROUPS)]
    U = range(len(units))
    bd = lambda x: _block_diag(x, head_of_lane)
    grp = lambda name, u: tok[name][units[u][0] * C:(units[u][0] + 1) * C,
                                    units[u][1] * MXU_DIM:(units[u][1] + 1) * MXU_DIM]
    s_bd = [state[q, g] for q, g in units]
    lhs = [jnp.concatenate([grp("ag", u), grp("rg", u)], axis=0) for u in U]
    rhs = [jnp.concatenate([bd(grp("bg", u)), bd(grp("kg", u)), s_bd[u].astype(BF16)], axis=0) for u in U]
    abs_ = [_dot_nt(lhs[u], rhs[u]) for u in U]
    ab = [abs_[u][:, :MXU_DIM] for u in U]
    ak = [abs_[u][:, MXU_DIM:2 * MXU_DIM] for u in U]
    sv = [abs_[u][:, 2 * MXU_DIM:] for u in U]
    n_pow = [jnp.where(strict, ab[u][:C], 0.0) for u in U]
    a_ak = [jnp.where(strict, ak[u][:C], 0.0) for u in U]
    a_rb = [jnp.where(incl, ab[u][C:], 0.0) for u in U]
    a_rk = [jnp.where(incl, ak[u][C:], 0.0) for u in U]
    v_bd = [bd(grp("v", u)) for u in U]
    av = [_dot(jnp.concatenate([a_ak[u], a_rk[u]], axis=0), v_bd[u]) for u in U]
    t_inv = [eye + n_pow[u] for u in U]
    n_pow = [_dot(n_pow[u], bd(n_pow[u])) for u in U]
    for _ in range(int(math.log2(C)) - 2):
        prod = [_dot(jnp.concatenate([n_pow[u], t_inv[u]], axis=0), bd(n_pow[u])) for u in U]
        n_pow = [prod[u][:C] for u in U]
        t_inv = [t_inv[u] + prod[u][C:] for u in U]
    t_inv = [t_inv[u] + _dot(t_inv[u], bd(n_pow[u])) for u in U]
    p = [_dot(t_inv[u], bd(sv[u][:C] + av[u][:C])) for u in U]
    o = [sv[u][C:] + _dot(a_rb[u], bd(p[u])) + av[u][C:] for u in U]
    for u, (q, g) in enumerate(units):
        upd = _dot_tn(jnp.concatenate([p[u], grp("v", u)], axis=0),
                      jnp.concatenate([grp("bg_c", u), grp("kg_c", u)], axis=0))
        state[q, g] = (s_bd[u] * g_chunk[q][:, g * MXU_DIM:(g + 1) * MXU_DIM]
                       + jnp.where(same_head, upd, 0.0))
    o_all = jnp.concatenate([jnp.concatenate(o[q * N_GROUPS:(q + 1) * N_GROUPS], axis=-1)
                             for q in range(n_seq)], axis=0)
    gate = gb_ref[...].reshape(R, RW_C).astype(F32)
    y = _rwkv_post(o_all, r, k, v, gate, vec["r_k"], vec["ln_w"], vec["ln_b"], ones_bd)
    y_ref[...] = y.reshape(n_seq, C, RW_C).astype(y_ref.dtype)

    @pl.when(c == nc - 1)
    def _():
        for q in range(n_seq):
            for g in range(N_GROUPS):
                for j in range(HEADS_PER_GROUP):
                    blk = slice(j * RW_HEAD, (j + 1) * RW_HEAD)
                    so_ref[q, g * HEADS_PER_GROUP + j] = state[q, g, blk, blk]


RW_VEC_ROWS = ("mu_r", "mu_k", "mu_v", "w0", "a0", "k_k", "k_a", "r_k", "ln_w", "ln_b")


def _rwkv_consts(p):
    mu = p["rw_mu"]
    vec = jnp.stack([mu[:RW_C], mu[RW_C:2 * RW_C], mu[2 * RW_C:3 * RW_C], p["rw_w0"], p["rw_a0"],
                     p["rw_k_k"], p["rw_k_a"], p["rw_r_k"].reshape(-1), p["rw_ln_w"],
                     p["rw_ln_b"]]).astype(F32)
    zeros = jnp.zeros((RW_LORA, RW_C), F32)
    w_lora = jnp.concatenate([jnp.concatenate([p["rw_w2"], zeros], axis=1),
                              jnp.concatenate([zeros, p["rw_a2"]], axis=1)], axis=0).astype(BF16)
    hl = jnp.arange(MXU_DIM) // RW_HEAD
    ones_bd = (hl[:, None] == hl[None, :]).astype(BF16)
    return dict(vec=vec, mu_wa=mu[3 * RW_C:].reshape(1, -1).astype(F32), w_lora=w_lora,
                ones_bd=ones_bd)


def _const_spec(arr, ngrid):
    zeros = (0,) * arr.ndim
    if ngrid == 1:
        return pl.BlockSpec(arr.shape, lambda i: zeros, pipeline_mode=pl.Buffered(1))
    return pl.BlockSpec(arr.shape, lambda i, j: zeros, pipeline_mode=pl.Buffered(1))


def _rwkv_prompt(sh, gb, cs, batch, seq, ret_sample):
    C = RW_CHUNK
    nq = RW_SEQS_PER_STEP
    nc = seq // C
    n_sh = 3 * RW_C + 2 * RW_LORA
    blk = lambda n: pl.BlockSpec((nq, C, n), lambda b, c: (b, c, 0))
    consts = [cs[n] for n in ("vec", "w_lora", "ones_bd")]
    rt_args, rt_in, rt_out, rt_shape = _ret_step_operands(
        *ret_sample, n_steps=(batch // nq) * nc, step_index=lambda b, c: b * nc + c)
    y, s, rt_y, rt_s = pl.pallas_call(
        _rwkv_chunk_kernel,
        grid=(batch // nq, nc),
        in_specs=[blk(n_sh), blk(RW_C)] + [_const_spec(a, 2) for a in consts] + rt_in,
        out_specs=[blk(RW_C),
                   pl.BlockSpec((nq, RW_HEADS, RW_HEAD, RW_HEAD), lambda b, c: (b, 0, 0, 0))] + rt_out,
        out_shape=[jax.ShapeDtypeStruct((batch, seq, RW_C), BF16),
                   jax.ShapeDtypeStruct((batch, RW_HEADS, RW_HEAD, RW_HEAD), F32)] + rt_shape,
        scratch_shapes=[pltpu.VMEM((nq, N_GROUPS, MXU_DIM, MXU_DIM), F32)],
        compiler_params=_params("arbitrary", "arbitrary"),
        name="rwkv_chunk",
    )(sh.reshape(batch, seq, n_sh), gb.reshape(batch, seq, RW_C), *consts, *rt_args)
    return y.reshape(batch * seq, RW_C), s, rt_y.reshape(rt_y.shape[0], RET_V), rt_s


def _rwkv_step_kernel(r_ref, k_ref, v_ref, wa_ref, gb_ref, s_ref, col_ref, mu_wa_ref, w2t_ref, a2t_ref,
                      y_ref, so_ref, o_scr):
    nb = y_ref.shape[-1]
    col = {n: col_ref[:, i:i + 1] for i, n in enumerate(RW_VEC_ROWS)}
    lerp = lambda ref, mu: ref[:, :nb] + (ref[:, nb:2 * nb] - ref[:, :nb]) * mu
    r = lerp(r_ref, col["mu_r"])
    k = lerp(k_ref, col["mu_k"])
    v = lerp(v_ref, col["mu_v"])
    zwa = lerp(wa_ref, mu_wa_ref[...])
    wpre = col["w0"] + _dot(w2t_ref[...], jnp.tanh(zwa[:RW_LORA]))
    decay = jnp.exp(-math.exp(-0.5) * _sigmoid(wpre))
    a = _sigmoid(col["a0"] + _dot(a2t_ref[...], zwa[RW_LORA:]))
    kk = k * col["k_k"]
    kk = kk * lax.rsqrt(jnp.maximum(jnp.sum(kk * kk, axis=0, keepdims=True), 1e-24))
    k = k * (1.0 + (a - 1.0) * col["k_a"])
    beta = a * kk
    for i in range(RW_HEAD):
        s = s_ref[i]
        sk = jnp.sum(s * kk, axis=0, keepdims=True)
        s_new = s * decay - sk * beta + v[i:i + 1, :] * k
        so_ref[i] = s_new
        o_scr[i:i + 1, :] = jnp.sum(s_new * r, axis=0, keepdims=True)
    o = o_scr[...]
    d = o - jnp.mean(o, axis=0, keepdims=True)
    var = jnp.mean(d * d, axis=0, keepdims=True)
    on = d * lax.rsqrt(var + RW_GN_EPS) * col["ln_w"] + col["ln_b"]
    bonus = jnp.sum(r * k * col["r_k"], axis=0, keepdims=True) * v
    y_ref[...] = ((on + bonus) * _silu(gb_ref[:, :nb])).astype(y_ref.dtype)


def _rwkv_sample(sh2, gb2, p, cs, state):
    nb = state.shape[1]
    sht = sh2.T
    gbt = gb2.T
    st = jnp.transpose(state, (0, 2, 3, 4, 1))
    n_head_blocks = RW_C // RW_HEAD
    rows = lambda off: pl.BlockSpec((RW_HEAD, 2 * nb), lambda h: (h + off, 0))
    lora_blk = pl.BlockSpec((2 * RW_LORA, 2 * nb), lambda h: (3 * RW_C // (2 * RW_LORA), 0))
    st_blk = pl.BlockSpec((None, None, RW_HEAD, RW_HEAD, nb), lambda h: (0, h, 0, 0, 0))
    wt_blk = pl.BlockSpec((RW_HEAD, RW_LORA), lambda h: (h, 0))
    yt, so = pl.pallas_call(
        _rwkv_step_kernel,
        grid=(RW_HEADS,),
        in_specs=[rows(0), rows(n_head_blocks), rows(2 * n_head_blocks), lora_blk, rows(0), st_blk,
                  pl.BlockSpec((RW_HEAD, len(RW_VEC_ROWS)), lambda h: (h, 0)),
                  pl.BlockSpec((2 * RW_LORA, 1), lambda h: (0, 0)), wt_blk, wt_blk],
        out_specs=[pl.BlockSpec((RW_HEAD, nb), lambda h: (h, 0)), st_blk],
        out_shape=[jax.ShapeDtypeStruct((RW_C, nb), BF16), jax.ShapeDtypeStruct(st.shape, F32)],
        scratch_shapes=[pltpu.VMEM((RW_HEAD, nb), F32)],
        compiler_params=_params("arbitrary"),
        name="rwkv_step",
    )(sht, sht, sht, sht, gbt, st, cs["vec"].T, cs["mu_wa"].reshape(-1, 1),
      p["rw_w2"].T.astype(BF16), p["rw_a2"].T.astype(BF16))
    return yt.T, jnp.transpose(so, (0, 4, 1, 2, 3))


def _tail_kernel(ya_ref, yb_ref, m_ref, x_ref, p_ref, wda_ref, wdb_ref, wout_ref, wple_ref, wgate_ref,
                 pg_ref, fg_ref, y_ref):
    m = m_ref[...].astype(F32)
    merged = (_sigmoid(m[:, :D_MODEL]) * jnp.dot(ya_ref[...], wda_ref[...], preferred_element_type=F32)
              + _sigmoid(m[:, D_MODEL:]) * jnp.dot(yb_ref[...], wdb_ref[...], preferred_element_type=F32))
    x = x_ref[...] + _dot(merged, wout_ref[...])
    gate = _sigmoid(_dot(_rms(x, pg_ref[...]), wgate_ref[...]))
    x = x + _dot(p_ref[...], wple_ref[...]) * gate
    y_ref[...] = _rms(x, fg_ref[...])


def _tail(ya, yb, m, x, p, w, tm):
    rows = x.shape[0]
    tile = lambda n: pl.BlockSpec((tm, n), lambda i: (i, 0))
    consts = [w["wda"], w["wdb"], w["wout"], w["wple"], w["wgate"], w["ple_g"], w["final_g"]]
    return pl.pallas_call(
        _tail_kernel,
        grid=(rows // tm,),
        in_specs=[tile(RET_V), tile(RW_C), tile(2 * D_MODEL), tile(D_MODEL), tile(PLE_DIM)]
        + [_const_spec(a, 1) for a in consts],
        out_specs=tile(D_MODEL),
        out_shape=jax.ShapeDtypeStruct((rows, D_MODEL), F32),
        compiler_params=_params("arbitrary"),
        name="tail",
    )(ya, yb, m, x, p, *consts)


def _layer_weights(p):
    return dict(
        wda=p["w_down_a"].astype(BF16), wdb=p["w_down_b"].astype(BF16), wout=p["w_out"].astype(BF16),
        wple=p["w_ple"].astype(BF16), wgate=p["w_ple_gate"].astype(BF16),
        ple_g=p["ple_norm_g"].reshape(1, -1), final_g=p["final_norm_g"].reshape(1, -1),
    )


def _layer_paths(x_p, pe_p, x_s, h_prev, s_ret, s_rw, pe_s, p, w, cs):
    batch, seq, d = x_p.shape
    rows = batch * seq
    nb = x_s.shape[0]
    xp2 = x_p.reshape(rows, d)
    xs2 = x_s.reshape(nb, d)
    w_in, h_s, (qk_s, v_s, ga_s, sh_s, gb_s, m_s) = _sample_proj(xs2, h_prev, p["norm_g"], p["w_in"])
    ya_p, sh, gb, m, ret_p, shift_p = _in_proj_retention(
        xp2, p["norm_g"], w_in, p["rw_mu"].reshape(1, -1), batch, seq, PROMPT_PROJ_ROWS)
    yb_p, rw_p, ya_s, ret_s = _rwkv_prompt(sh, gb, cs, batch, seq, (qk_s, v_s, ga_s, s_ret))
    y_p = _tail(ya_p, yb_p, m, xp2, pe_p.reshape(rows, PLE_DIM), w, PROMPT_TAIL_ROWS)
    yb_s, rw_s = _rwkv_sample(sh_s, gb_s, p, cs, s_rw)
    y_s = _tail(ya_s, yb_s, m_s, xs2, pe_s.reshape(nb, PLE_DIM), w, nb)
    return (y_p.reshape(batch, seq, d), shift_p.reshape(batch, d), ret_p, rw_p,
            y_s.reshape(nb, 1, d), h_s, ret_s, rw_s)


def kernel(x_prompt, x_sample, state_ret, state_rwkv, state_shift, p_prompt, p_sample, norm_g, w_in, rw_mu, rw_w0, rw_w2, rw_a0, rw_a2, rw_k_k, rw_k_a, rw_r_k, rw_ln_w, rw_ln_b, w_down_a, w_down_b, w_out, w_ple, ple_norm_g, w_ple_gate, final_norm_g):
    assert norm_g.shape[0] == 1, "single-layer step"
    p = dict(norm_g=norm_g[0], w_in=w_in[0], rw_mu=rw_mu[0], rw_w0=rw_w0[0], rw_w2=rw_w2[0],
             rw_a0=rw_a0[0], rw_a2=rw_a2[0], rw_k_k=rw_k_k[0], rw_k_a=rw_k_a[0], rw_r_k=rw_r_k[0],
             rw_ln_w=rw_ln_w[0], rw_ln_b=rw_ln_b[0], w_down_a=w_down_a[0], w_down_b=w_down_b[0],
             w_out=w_out[0], w_ple=w_ple[0], ple_norm_g=ple_norm_g[0], w_ple_gate=w_ple_gate[0],
             final_norm_g=final_norm_g)
    w = _layer_weights(p)
    cs = _rwkv_consts(p)
    y_p, sh_p, ret_p, rw_p, y_s, sh_s, ret_s, rw_s = _layer_paths(
        x_prompt, p_prompt[0], x_sample, state_shift[0], state_ret[0], state_rwkv, p_sample[0], p, w, cs)
    return (y_p, y_s, ret_p[None], rw_p[None], sh_p[None], ret_s[None], rw_s, sh_s[None])
```

```python
import functools
import math

import jax
import jax.numpy as jnp
from jax import lax
from jax.experimental import pallas as pl
from jax.experimental.pallas import tpu as pltpu

F32 = jnp.float32
BF16 = jnp.bfloat16

D_MODEL = 1024
RET_HEADS = 4
RET_DK = 256
RET_DV = 512
RET_QK = RET_HEADS * RET_DK
RET_V = RET_HEADS * RET_DV
ROPE_BASE = 10000.0
RW_HEAD = 64
RW_HEADS = D_MODEL // RW_HEAD
RW_C = RW_HEADS * RW_HEAD
RW_LORA = 64
RW_GN_EPS = 1e-5 * RW_HEAD
RW_CHUNK = 64
PLE_DIM = 256
NORM_EPS = 1e-6
PAST_LEN = 16384

MXU_DIM = 256
HEADS_PER_GROUP = MXU_DIM // RW_HEAD
N_GROUPS = RW_C // MXU_DIM
HEAD_SHIFT = RW_HEAD.bit_length() - 1
VMEM_LIMIT_BYTES = 56 * 1024 * 1024
PROMPT_PROJ_ROWS = 256
PROMPT_TAIL_ROWS = 512
RW_SEQS_PER_STEP = 4


def _params(*sem):
    return pltpu.CompilerParams(dimension_semantics=sem, vmem_limit_bytes=VMEM_LIMIT_BYTES)


def _dot(a, b):
    return jnp.dot(a.astype(BF16), b.astype(BF16), preferred_element_type=F32)


def _dot_nt(a, b):
    return lax.dot_general(a.astype(BF16), b.astype(BF16), (((1,), (1,)), ((), ())),
                           preferred_element_type=F32)


def _dot_tn(a, b):
    return lax.dot_general(a.astype(BF16), b.astype(BF16), (((0,), (0,)), ((), ())),
                           preferred_element_type=F32)


def _sigmoid(x):
    return 0.5 * jnp.tanh(0.5 * x) + 0.5


def _silu(x):
    return x * _sigmoid(x)


def _rms(x, g):
    return x * lax.rsqrt(jnp.mean(x * x, axis=-1, keepdims=True) + NORM_EPS) * g


PROJ_WIDTHS = (2 * RET_QK, RET_V, RET_V, 3 * RW_C + 2 * RW_LORA, RW_C, 2 * D_MODEL)


SAMPLE_PROJ_K_ROWS = 128


def _sample_proj_kernel(x_ref, xk_ref, hp_ref, g_ref, w_ref, wb_ref, h_ref, *rest):
    out_refs, scale = rest[:-1], rest[-1]

    @pl.when(pl.program_id(0) == 0)
    def _():
        x = x_ref[...]
        scale[...] = lax.rsqrt(jnp.mean(x * x, axis=-1, keepdims=True) + NORM_EPS)
        for o_ref in out_refs:
            o_ref[...] = jnp.zeros_like(o_ref)

    wb = w_ref[...].astype(BF16)
    wb_ref[...] = wb
    h = xk_ref[...] * scale[...] * g_ref[...]
    h_ref[...] = h
    rows = jnp.concatenate([h, hp_ref[...]], axis=0).astype(BF16)
    off = 0
    for o_ref, n in zip(out_refs, PROJ_WIDTHS):
        o_ref[...] += jnp.dot(rows, wb[:, off:off + n], preferred_element_type=F32)
        off += n


def _sample_proj(x, h_prev, g, w_in):
    nb, d = x.shape
    n_all = w_in.shape[1]
    tk = SAMPLE_PROJ_K_ROWS
    slab = pl.BlockSpec((nb, tk), lambda k: (0, k))
    outs = pl.pallas_call(
        _sample_proj_kernel,
        grid=(d // tk,),
        in_specs=[_const_spec(x, 1), slab, slab, pl.BlockSpec((1, tk), lambda k: (0, k)),
                  pl.BlockSpec((tk, n_all), lambda k: (k, 0))],
        out_specs=[pl.BlockSpec((tk, n_all), lambda k: (k, 0)), slab]
        + [pl.BlockSpec((2 * nb, n), lambda k: (0, 0)) for n in PROJ_WIDTHS],
        out_shape=[jax.ShapeDtypeStruct((d, n_all), BF16), jax.ShapeDtypeStruct((nb, d), F32)]
        + [jax.ShapeDtypeStruct((2 * nb, n), F32) for n in PROJ_WIDTHS],
        scratch_shapes=[pltpu.VMEM((nb, 1), F32)],
        compiler_params=_params("arbitrary"),
        name="sample_proj",
    )(x, x, h_prev, g.reshape(1, d), w_in)
    return outs[0], outs[1], outs[2:]


def _in_proj_ret_kernel(dec_ref, x_ref, g_ref, w_ref, cos_ref, sin_ref, mu_ref,
                        ya_ref, sh_ref, gb_ref, m_ref, s_ref, hl_ref, carry):
    @pl.when(pl.program_id(1) == 0)
    def _():
        s_ref[...] = jnp.zeros_like(s_ref)
        carry[...] = jnp.zeros_like(carry)

    tm = x_ref.shape[0]
    h32 = _rms(x_ref[...], g_ref[...])
    hl_ref[...] = h32[tm - 1:tm, :]
    h = h32.astype(BF16)
    o_qk, o_v, o_ga, o_sh, o_gb, o_m = [sum(PROJ_WIDTHS[:i]) for i in range(len(PROJ_WIDTHS))]
    proj = lambda off, n: jnp.dot(h, w_ref[:, off:off + n], preferred_element_type=F32)
    qk = proj(o_qk, 2 * RET_QK)
    v = proj(o_v, RET_V)
    ga = proj(o_ga, RET_V)
    sh = proj(o_sh, PROJ_WIDTHS[3])
    row = lax.broadcasted_iota(jnp.int32, (tm, 1), 0)
    sh_prev = jnp.where(row == 0, carry[...], pltpu.roll(sh, 1, 0))
    carry[...] = sh[tm - 1:tm, :]
    sh_ref[...] = (sh + (sh_prev - sh) * mu_ref[...]).astype(sh_ref.dtype)
    gb_ref[...] = _silu(proj(o_gb, PROJ_WIDTHS[4])).astype(gb_ref.dtype)
    m_ref[...] = proj(o_m, PROJ_WIDTHS[5]).astype(m_ref.dtype)
    _ret_chunk(qk[:, :RET_QK], qk[:, RET_QK:], v, ga, cos_ref[...], sin_ref[...], dec_ref, s_ref, ya_ref)


def _in_proj_retention(x, g, w_in, mu, batch, seq, tm):
    m, d = x.shape
    nt = seq // tm
    n_all = w_in.shape[1]
    log_g = _ret_decay_table()
    dec = jnp.stack([log_g, jnp.exp(tm * log_g)], axis=1).reshape(-1)
    half = RET_DK // 2
    cos, sin = _rope_tables(jnp.arange(seq, dtype=F32))
    row = lambda b, t: (b * nt + t, 0)
    widths = (RET_V,) + PROJ_WIDTHS[3:]
    return pl.pallas_call(
        _in_proj_ret_kernel,
        grid=(batch, nt),
        in_specs=[pl.BlockSpec(memory_space=pltpu.SMEM),
                  pl.BlockSpec((tm, d), row), pl.BlockSpec((1, d), lambda b, t: (0, 0)),
                  pl.BlockSpec((d, n_all), lambda b, t: (0, 0), pipeline_mode=pl.Buffered(1)),
                  pl.BlockSpec((tm, half), lambda b, t: (t, 0)),
                  pl.BlockSpec((tm, half), lambda b, t: (t, 0)), _const_spec(mu, 2)],
        out_specs=[pl.BlockSpec((tm, n), row) for n in widths]
        + [pl.BlockSpec((None, RET_HEADS, RET_DK, RET_DV), lambda b, t: (b, 0, 0, 0)),
           pl.BlockSpec((None, 1, d), lambda b, t: (b, 0, 0))],
        out_shape=[jax.ShapeDtypeStruct((m, n), BF16) for n in widths]
        + [jax.ShapeDtypeStruct((batch, RET_HEADS, RET_DK, RET_DV), F32),
           jax.ShapeDtypeStruct((batch, 1, d), F32)],
        scratch_shapes=[pltpu.VMEM((1, PROJ_WIDTHS[3]), F32)],
        compiler_params=_params("arbitrary", "arbitrary"),
        name="in_proj_retention",
    )(dec, x, g.reshape(1, d), w_in, cos, sin, mu)


def _rope(x, cos, sin):
    half = x.shape[-1] // 2
    x1, x2 = x[:, :half], x[:, half:]
    return jnp.concatenate([x1 * cos - x2 * sin, x2 * cos + x1 * sin], axis=-1)


def _rope_tables(pos):
    half = RET_DK // 2
    inv = ROPE_BASE ** (-jnp.arange(half, dtype=F32) / half)
    ang = pos[:, None] * inv[None, :]
    return jnp.cos(ang), jnp.sin(ang)


def _ret_chunk(q, k, v, ga, cos, sin, dec_ref, s_ref, y_ref):
    C = q.shape[0]
    H = range(RET_HEADS)
    ti = lax.broadcasted_iota(jnp.int32, (C, C), 0)
    tj = lax.broadcasted_iota(jnp.int32, (C, C), 1)
    rel = (ti - tj).astype(F32)
    idx = lax.broadcasted_iota(jnp.int32, (C, 1), 0).astype(F32)
    lg = [dec_ref[2 * h] for h in H]
    qh = [_rope(q[:, h * RET_DK:(h + 1) * RET_DK], cos, sin) for h in H]
    kh = [_rope(k[:, h * RET_DK:(h + 1) * RET_DK], cos, sin) * (RET_DK ** -0.5) for h in H]
    vh = [v[:, h * RET_DV:(h + 1) * RET_DV].astype(BF16) for h in H]
    s = [s_ref[h] for h in H]
    inner = [_dot_nt(qh[h], kh[h]) * jnp.where(rel >= 0, jnp.exp(jnp.maximum(rel, 0.0) * lg[h]), 0.0)
             for h in H]
    o = [_dot(inner[h], vh[h]) + _dot(qh[h] * jnp.exp((idx + 1.0) * lg[h]), s[h]) for h in H]
    for h in H:
        s_ref[h] = dec_ref[2 * h + 1] * s[h] + _dot_tn(kh[h] * jnp.exp((C - 1.0 - idx) * lg[h]), vh[h])
    for h in H:
        on = o[h] * lax.rsqrt(jnp.mean(o[h] * o[h], axis=-1, keepdims=True) + NORM_EPS)
        y_ref[:, h * RET_DV:(h + 1) * RET_DV] = (
            on * _silu(ga[:, h * RET_DV:(h + 1) * RET_DV])).astype(y_ref.dtype)


def _ret_decay_table():
    log_g = jnp.log(1.0 - jnp.exp2(-5.0 - jnp.arange(RET_HEADS, dtype=F32)))
    return log_g


def _ret_step_pieces(dec_ref, qk_ref, v_ref, ga_ref, cos_ref, sin_ref, s_ref, y_ref, so_ref, first_row):
    cos = cos_ref[...]
    sin = sin_ref[...]
    units = [(b, h) for b in range(s_ref.shape[0]) for h in range(RET_HEADS)]
    rows = [pl.ds(first_row + b, 1) for b, _ in units]
    ks = [_rope(qk_ref[rows[u], RET_QK + h * RET_DK:RET_QK + (h + 1) * RET_DK], cos, sin) * (RET_DK ** -0.5)
          for u, (_, h) in enumerate(units)]
    row0 = lax.broadcasted_iota(jnp.int32, (8, 1), 0)

    def piece(u):
        b, h = units[u]
        g = dec_ref[h]
        q = _rope(qk_ref[rows[u], h * RET_DK:(h + 1) * RET_DK], cos, sin)
        v = v_ref[rows[u], h * RET_DV:(h + 1) * RET_DV]
        s = s_ref[b, h]
        qk = jnp.sum(q * ks[u], axis=-1, keepdims=True)
        q8 = jnp.broadcast_to(q, (8, RET_DK))
        o = qk * v + g * _dot(q8, s)[0:1, :]
        k = ks[u]
        k_hi = k.astype(BF16).astype(F32)
        k_lo = k - k_hi
        v_hi = v.astype(BF16).astype(F32)
        v_lo = v - v_hi
        k8 = jnp.where(row0 < 2, k_hi, jnp.where(row0 == 2, k_lo, 0.0))
        v8 = jnp.where((row0 == 0) | (row0 == 2), v_hi, jnp.where(row0 == 1, v_lo, 0.0))
        so_ref[b, h] = g * s + _dot_tn(k8, v8)
        o = o * lax.rsqrt(jnp.mean(o * o, axis=-1, keepdims=True) + NORM_EPS)
        y_ref[b, :, h * RET_DV:(h + 1) * RET_DV] = (
            o * _silu(ga_ref[rows[u], h * RET_DV:(h + 1) * RET_DV])).astype(y_ref.dtype)

    return [functools.partial(piece, u) for u in range(len(units))]


def _ret_step_operands(qk, v, ga, state, n_steps, step_index):
    nb = state.shape[0]
    nq = nb // n_steps
    assert nq * n_steps == nb
    g = jnp.exp(_ret_decay_table())
    cos, sin = _rope_tables(PAST_LEN + jnp.arange(1, dtype=F32))
    tab = pl.BlockSpec((1, RET_DK // 2), lambda *ids: (0, 0))
    st = pl.BlockSpec((nq, RET_HEADS, RET_DK, RET_DV), lambda *ids: (step_index(*ids), 0, 0, 0))
    args = [g, qk, v, ga, cos, sin, state]
    in_specs = [pl.BlockSpec(memory_space=pltpu.SMEM), _const_spec(qk, 2), _const_spec(v, 2),
                _const_spec(ga, 2), tab, tab, st]
    out_specs = [pl.BlockSpec((nq, 1, RET_V), lambda *ids: (step_index(*ids), 0, 0)), st]
    out_shape = [jax.ShapeDtypeStruct((nb, 1, RET_V), BF16), jax.ShapeDtypeStruct(state.shape, F32)]
    return args, in_specs, out_specs, out_shape


def _head_sums(xs, ones_bd):
    rows = xs[0].shape[0]
    stack = jnp.concatenate(
        [x[:, g * MXU_DIM:(g + 1) * MXU_DIM] for x in xs for g in range(N_GROUPS)], axis=0)
    s = jnp.dot(stack.astype(BF16), ones_bd, preferred_element_type=F32)
    return [jnp.concatenate([s[(i * N_GROUPS + g) * rows:(i * N_GROUPS + g + 1) * rows]
                             for g in range(N_GROUPS)], axis=-1) for i in range(len(xs))]


def _rwkv_prep(z, zwa, w0, w_lora, a0, k_k, k_a, ones_bd):
    r = z[:, :RW_C]
    k = z[:, RW_C:2 * RW_C]
    v = z[:, 2 * RW_C:]
    is_decay = lax.broadcasted_iota(jnp.int32, (1, 2 * RW_LORA), 1) < RW_LORA
    la = _dot(jnp.where(is_decay, jnp.tanh(zwa), zwa), w_lora)
    log_decay = -math.exp(-0.5) * _sigmoid(w0 + la[:, :RW_C])
    a = _sigmoid(a0 + la[:, RW_C:])
    kk = k * k_k
    kk = kk * lax.rsqrt(jnp.maximum(_head_sums([kk * kk], ones_bd)[0], 1e-24))
    k = k * (1.0 + (a - 1.0) * k_a)
    return r, log_decay, k, v, kk, a


def _rwkv_post(o, r, k, v, gate, r_k, ln_w, ln_b, ones_bd):
    o_sum, rk_sum = _head_sums([o, r * k * r_k], ones_bd)
    d = o - o_sum * (1.0 / RW_HEAD)
    var = _head_sums([d * d], ones_bd)[0] * (1.0 / RW_HEAD)
    on = d * lax.rsqrt(var + RW_GN_EPS) * ln_w + ln_b
    return (on + rk_sum * v) * gate


def _block_diag(x, head_of_lane):
    xb = x.astype(BF16)
    zero = jnp.zeros_like(xb)
    return jnp.concatenate(
        [jnp.where(head_of_lane == j, xb, zero) for j in range(HEADS_PER_GROUP)], axis=0)


def _rwkv_chunk_kernel(sh_ref, gb_ref, vec_ref, wl_ref, ones_ref,
                       rt_dec_ref, rt_qk_ref, rt_v_ref, rt_ga_ref, rt_cos_ref, rt_sin_ref, rt_s_ref,
                       y_ref, so_ref, rt_y_ref, rt_so_ref, state):
    c = pl.program_id(1)
    nc = pl.num_programs(1)
    step = pl.program_id(0) * nc + c
    rt_pieces = _ret_step_pieces(rt_dec_ref, rt_qk_ref, rt_v_ref, rt_ga_ref, rt_cos_ref, rt_sin_ref,
                                 rt_s_ref, rt_y_ref, rt_so_ref, step * rt_s_ref.shape[0])
    C = RW_CHUNK
    n_seq = sh_ref.shape[0]

    @pl.when(c == 0)
    def _():
        state[...] = jnp.zeros_like(state)

    ones_bd = ones_ref[...]
    vec = {n: vec_ref[i:i + 1, :] for i, n in enumerate(RW_VEC_ROWS)}
    R = n_seq * C
    z = sh_ref[...].reshape(R, sh_ref.shape[-1]).astype(F32)
    r, lw, k, v, kk, a = _rwkv_prep(z[:, :3 * RW_C], z[:, 3 * RW_C:], vec["w0"], wl_ref[...],
                                    vec["a0"], vec["k_k"], vec["k_a"], ones_bd)
    ti = lax.broadcasted_iota(jnp.int32, (R, R), 0)
    tj = lax.broadcasted_iota(jnp.int32, (R, R), 1)
    chunk_shift = C.bit_length() - 1
    tri = ((ti >= tj) & ((ti >> chunk_shift) == (tj >> chunk_shift))).astype(BF16)
    lw_hi = lw.astype(BF16)
    lw_lo = (lw - lw_hi.astype(F32)).astype(BF16)
    cl = jnp.dot(jnp.concatenate([tri, tri], axis=1), jnp.concatenate([lw_hi, lw_lo], axis=0),
                 preferred_element_type=F32)
    g_chunk = [jnp.exp(cl[(q + 1) * C - 1:(q + 1) * C, :]) for q in range(n_seq)]
    g_rows = jnp.concatenate([jnp.broadcast_to(g, (C, RW_C)) for g in g_chunk], axis=0)
    e_neg = jnp.exp(-cl)
    e_rem = e_neg * g_rows
    beta = a * kk
    b16 = lambda x: x.astype(BF16)
    tok = dict(r=r, k=k, v=v, ag=b16(-kk * jnp.exp(cl - lw)), rg=b16(r * jnp.exp(cl)),
               bg=b16(beta * e_neg), kg=b16(k * e_neg), bg_c=b16(beta * e_rem), kg_c=b16(k * e_rem))

    lane = lax.broadcasted_iota(jnp.int32, (1, MXU_DIM), 1)
    head_of_lane = lane >> HEAD_SHIFT
    t_col = lax.broadcasted_iota(jnp.int32, (C, MXU_DIM), 0)
    i_lane = lax.broadcasted_iota(jnp.int32, (C, MXU_DIM), 1) & (RW_HEAD - 1)
    strict = t_col > i_lane
    incl = t_col >= i_lane
    eye = (t_col == i_lane).astype(F32)
    vrow_head = lax.broadcasted_iota(jnp.int32, (MXU_DIM, MXU_DIM), 0) >> HEAD_SHIFT
    klane_head = lax.broadcasted_iota(jnp.int32, (MXU_DIM, MXU_DIM), 1) >> HEAD_SHIFT
    same_head = vrow_head == klane_head

    units = [(q, g) for q in range(n_seq) for g in range(N_GROUPS)]
    U = range(len(units))
    bd = lambda x: _block_diag(x, head_of_lane)
    grp = lambda name, u: tok[name][units[u][0] * C:(units[u][0] + 1) * C,
                                    units[u][1] * MXU_DIM:(units[u][1] + 1) * MXU_DIM]
    s_bd = [state[q, g] for q, g in units]
    lhs = [jnp.concatenate([grp("ag", u), grp("rg", u)], axis=0) for u in U]
    rhs = [jnp.concatenate([bd(grp("bg", u)), bd(grp("kg", u)), s_bd[u].astype(BF16)], axis=0) for u in U]
    abs_ = []
    for u in U:
        abs_.append(_dot_nt(lhs[u], rhs[u]))
        for piece in rt_pieces[u * len(rt_pieces) // len(units):(u + 1) * len(rt_pieces) // len(units)]:
            piece()
    ab = [abs_[u][:, :MXU_DIM] for u in U]
    ak = [abs_[u][:, MXU_DIM:2 * MXU_DIM] for u in U]
    sv = [abs_[u][:, 2 * MXU_DIM:] for u in U]
    n_pow = [jnp.where(strict, ab[u][:C], 0.0) for u in U]
    a_ak = [jnp.where(strict, ak[u][:C], 0.0) for u in U]
    a_rb = [jnp.where(incl, ab[u][C:], 0.0) for u in U]
    a_rk = [jnp.where(incl, ak[u][C:], 0.0) for u in U]
    v_bd = [bd(grp("v", u)) for u in U]
    av = [_dot(jnp.concatenate([a_ak[u], a_rk[u]], axis=0), v_bd[u]) for u in U]
    t_inv = [eye + n_pow[u] for u in U]
    n_pow = [_dot(n_pow[u], bd(n_pow[u])) for u in U]
    for _ in range(int(math.log2(C)) - 2):
        prod = [_dot(jnp.concatenate([n_pow[u], t_inv[u]], axis=0), bd(n_pow[u])) for u in U]
        n_pow = [prod[u][:C] for u in U]
        t_inv = [t_inv[u] + prod[u][C:] for u in U]
    t_inv = [t_inv[u] + _dot(t_inv[u], bd(n_pow[u])) for u in U]
    p = [_dot(t_inv[u], bd(sv[u][:C] + av[u][:C])) for u in U]
    o = [sv[u][C:] + _dot(a_rb[u], bd(p[u])) + av[u][C:] for u in U]
    for u, (q, g) in enumerate(units):
        upd = _dot_tn(jnp.concatenate([p[u], grp("v", u)], axis=0),
                      jnp.concatenate([grp("bg_c", u), grp("kg_c", u)], axis=0))
        state[q, g] = (s_bd[u] * g_chunk[q][:, g * MXU_DIM:(g + 1) * MXU_DIM]
                       + jnp.where(same_head, upd, 0.0))
    o_all = jnp.concatenate([jnp.concatenate(o[q * N_GROUPS:(q + 1) * N_GROUPS], axis=-1)
                             for q in range(n_seq)], axis=0)
    gate = gb_ref[...].reshape(R, RW_C).astype(F32)
    y = _rwkv_post(o_all, r, k, v, gate, vec["r_k"], vec["ln_w"], vec["ln_b"], ones_bd)
    y_ref[...] = y.reshape(n_seq, C, RW_C).astype(y_ref.dtype)

    @pl.when(c == nc - 1)
    def _():
        for q in range(n_seq):
            for g in range(N_GROUPS):
                for j in range(HEADS_PER_GROUP):
                    blk = slice(j * RW_HEAD, (j + 1) * RW_HEAD)
                    so_ref[q, g * HEADS_PER_GROUP + j] = state[q, g, blk, blk]


RW_VEC_ROWS = ("mu_r", "mu_k", "mu_v", "w0", "a0", "k_k", "k_a", "r_k", "ln_w", "ln_b")


def _rwkv_consts(p):
    mu = p["rw_mu"]
    vec = jnp.stack([mu[:RW_C], mu[RW_C:2 * RW_C], mu[2 * RW_C:3 * RW_C], p["rw_w0"], p["rw_a0"],
                     p["rw_k_k"], p["rw_k_a"], p["rw_r_k"].reshape(-1), p["rw_ln_w"],
                     p["rw_ln_b"]]).astype(F32)
    zeros = jnp.zeros((RW_LORA, RW_C), F32)
    w_lora = jnp.concatenate([jnp.concatenate([p["rw_w2"], zeros], axis=1),
                              jnp.concatenate([zeros, p["rw_a2"]], axis=1)], axis=0).astype(BF16)
    hl = jnp.arange(MXU_DIM) // RW_HEAD
    ones_bd = (hl[:, None] == hl[None, :]).astype(BF16)
    return dict(vec=vec, mu_wa=mu[3 * RW_C:].reshape(1, -1).astype(F32), w_lora=w_lora,
                ones_bd=ones_bd)


def _const_spec(arr, ngrid):
    zeros = (0,) * arr.ndim
    if ngrid == 1:
        return pl.BlockSpec(arr.shape, lambda i: zeros, pipeline_mode=pl.Buffered(1))
    return pl.BlockSpec(arr.shape, lambda i, j: zeros, pipeline_mode=pl.Buffered(1))


def _rwkv_prompt(sh, gb, cs, batch, seq, ret_sample):
    C = RW_CHUNK
    nq = RW_SEQS_PER_STEP
    nc = seq // C
    n_sh = 3 * RW_C + 2 * RW_LORA
    blk = lambda n: pl.BlockSpec((nq, C, n), lambda b, c: (b, c, 0))
    consts = [cs[n] for n in ("vec", "w_lora", "ones_bd")]
    rt_args, rt_in, rt_out, rt_shape = _ret_step_operands(
        *ret_sample, n_steps=(batch // nq) * nc, step_index=lambda b, c: b * nc + c)
    y, s, rt_y, rt_s = pl.pallas_call(
        _rwkv_chunk_kernel,
        grid=(batch // nq, nc),
        in_specs=[blk(n_sh), blk(RW_C)] + [_const_spec(a, 2) for a in consts] + rt_in,
        out_specs=[blk(RW_C),
                   pl.BlockSpec((nq, RW_HEADS, RW_HEAD, RW_HEAD), lambda b, c: (b, 0, 0, 0))] + rt_out,
        out_shape=[jax.ShapeDtypeStruct((batch, seq, RW_C), BF16),
                   jax.ShapeDtypeStruct((batch, RW_HEADS, RW_HEAD, RW_HEAD), F32)] + rt_shape,
        scratch_shapes=[pltpu.VMEM((nq, N_GROUPS, MXU_DIM, MXU_DIM), F32)],
        compiler_params=_params("arbitrary", "arbitrary"),
        name="rwkv_chunk",
    )(sh.reshape(batch, seq, n_sh), gb.reshape(batch, seq, RW_C), *consts, *rt_args)
    return y.reshape(batch * seq, RW_C), s, rt_y.reshape(rt_y.shape[0], RET_V), rt_s


def _rwkv_step_kernel(r_ref, k_ref, v_ref, wa_ref, gb_ref, s_ref, col_ref, mu_wa_ref, w2t_ref, a2t_ref,
                      y_ref, so_ref, o_scr):
    nb = y_ref.shape[-1]
    col = {n: col_ref[:, i:i + 1] for i, n in enumerate(RW_VEC_ROWS)}
    lerp = lambda ref, mu: ref[:, :nb] + (ref[:, nb:2 * nb] - ref[:, :nb]) * mu
    r = lerp(r_ref, col["mu_r"])
    k = lerp(k_ref, col["mu_k"])
    v = lerp(v_ref, col["mu_v"])
    zwa = lerp(wa_ref, mu_wa_ref[...])
    wpre = col["w0"] + _dot(w2t_ref[...], jnp.tanh(zwa[:RW_LORA]))
    decay = jnp.exp(-math.exp(-0.5) * _sigmoid(wpre))
    a = _sigmoid(col["a0"] + _dot(a2t_ref[...], zwa[RW_LORA:]))
    kk = k * col["k_k"]
    kk = kk * lax.rsqrt(jnp.maximum(jnp.sum(kk * kk, axis=0, keepdims=True), 1e-24))
    k = k * (1.0 + (a - 1.0) * col["k_a"])
    beta = a * kk
    for i in range(RW_HEAD):
        s = s_ref[i]
        sk = jnp.sum(s * kk, axis=0, keepdims=True)
        s_new = s * decay - sk * beta + v[i:i + 1, :] * k
        so_ref[i] = s_new
        o_scr[i:i + 1, :] = jnp.sum(s_new * r, axis=0, keepdims=True)
    o = o_scr[...]
    d = o - jnp.mean(o, axis=0, keepdims=True)
    var = jnp.mean(d * d, axis=0, keepdims=True)
    on = d * lax.rsqrt(var + RW_GN_EPS) * col["ln_w"] + col["ln_b"]
    bonus = jnp.sum(r * k * col["r_k"], axis=0, keepdims=True) * v
    y_ref[...] = ((on + bonus) * _silu(gb_ref[:, :nb])).astype(y_ref.dtype)


def _rwkv_sample(sh2, gb2, p, cs, state):
    nb = state.shape[1]
    sht = sh2.T
    gbt = gb2.T
    st = jnp.transpose(state, (0, 2, 3, 4, 1))
    n_head_blocks = RW_C // RW_HEAD
    rows = lambda off: pl.BlockSpec((RW_HEAD, 2 * nb), lambda h: (h + off, 0))
    lora_blk = pl.BlockSpec((2 * RW_LORA, 2 * nb), lambda h: (3 * RW_C // (2 * RW_LORA), 0))
    st_blk = pl.BlockSpec((None, None, RW_HEAD, RW_HEAD, nb), lambda h: (0, h, 0, 0, 0))
    wt_blk = pl.BlockSpec((RW_HEAD, RW_LORA), lambda h: (h, 0))
    yt, so = pl.pallas_call(
        _rwkv_step_kernel,
        grid=(RW_HEADS,),
        in_specs=[rows(0), rows(n_head_blocks), rows(2 * n_head_blocks), lora_blk, rows(0), st_blk,
                  pl.BlockSpec((RW_HEAD, len(RW_VEC_ROWS)), lambda h: (h, 0)),
                  pl.BlockSpec((2 * RW_LORA, 1), lambda h: (0, 0)), wt_blk, wt_blk],
        out_specs=[pl.BlockSpec((RW_HEAD, nb), lambda h: (h, 0)), st_blk],
        out_shape=[jax.ShapeDtypeStruct((RW_C, nb), BF16), jax.ShapeDtypeStruct(st.shape, F32)],
        scratch_shapes=[pltpu.VMEM((RW_HEAD, nb), F32)],
        compiler_params=_params("arbitrary"),
        name="rwkv_step",
    )(sht, sht, sht, sht, gbt, st, cs["vec"].T, cs["mu_wa"].reshape(-1, 1),
      p["rw_w2"].T.astype(BF16), p["rw_a2"].T.astype(BF16))
    return yt.T, jnp.transpose(so, (0, 4, 1, 2, 3))


def _tail_kernel(ya_ref, yb_ref, m_ref, x_ref, p_ref, wda_ref, wdb_ref, wout_ref, wple_ref, wgate_ref,
                 pg_ref, fg_ref, y_ref):
    m = m_ref[...].astype(F32)
    merged = (_sigmoid(m[:, :D_MODEL]) * jnp.dot(ya_ref[...], wda_ref[...], preferred_element_type=F32)
              + _sigmoid(m[:, D_MODEL:]) * jnp.dot(yb_ref[...], wdb_ref[...], preferred_element_type=F32))
    x = x_ref[...] + _dot(merged, wout_ref[...])
    gate = _sigmoid(_dot(_rms(x, pg_ref[...]), wgate_ref[...]))
    x = x + _dot(p_ref[...], wple_ref[...]) * gate
    y_ref[...] = _rms(x, fg_ref[...])


def _tail(ya, yb, m, x, p, w, tm):
    rows = x.shape[0]
    tile = lambda n: pl.BlockSpec((tm, n), lambda i: (i, 0))
    consts = [w["wda"], w["wdb"], w["wout"], w["wple"], w["wgate"], w["ple_g"], w["final_g"]]
    return pl.pallas_call(
        _tail_kernel,
        grid=(rows // tm,),
        in_specs=[tile(RET_V), tile(RW_C), tile(2 * D_MODEL), tile(D_MODEL), tile(PLE_DIM)]
        + [_const_spec(a, 1) for a in consts],
        out_specs=tile(D_MODEL),
        out_shape=jax.ShapeDtypeStruct((rows, D_MODEL), F32),
        compiler_params=_params("arbitrary"),
        name="tail",
    )(ya, yb, m, x, p, *consts)


def _layer_weights(p):
    return dict(
        wda=p["w_down_a"].astype(BF16), wdb=p["w_down_b"].astype(BF16), wout=p["w_out"].astype(BF16),
        wple=p["w_ple"].astype(BF16), wgate=p["w_ple_gate"].astype(BF16),
        ple_g=p["ple_norm_g"].reshape(1, -1), final_g=p["final_norm_g"].reshape(1, -1),
    )


def _layer_paths(x_p, pe_p, x_s, h_prev, s_ret, s_rw, pe_s, p, w, cs):
    batch, seq, d = x_p.shape
    rows = batch * seq
    nb = x_s.shape[0]
    xp2 = x_p.reshape(rows, d)
    xs2 = x_s.reshape(nb, d)
    w_in, h_s, (qk_s, v_s, ga_s, sh_s, gb_s, m_s) = _sample_proj(xs2, h_prev, p["norm_g"], p["w_in"])
    ya_p, sh, gb, m, ret_p, shift_p = _in_proj_retention(
        xp2, p["norm_g"], w_in, p["rw_mu"].reshape(1, -1), batch, seq, PROMPT_PROJ_ROWS)
    yb_p, rw_p, ya_s, ret_s = _rwkv_prompt(sh, gb, cs, batch, seq, (qk_s, v_s, ga_s, s_ret))
    y_p = _tail(ya_p, yb_p, m, xp2, pe_p.reshape(rows, PLE_DIM), w, PROMPT_TAIL_ROWS)
    yb_s, rw_s = _rwkv_sample(sh_s, gb_s, p, cs, s_rw)
    y_s = _tail(ya_s, yb_s, m_s, xs2, pe_s.reshape(nb, PLE_DIM), w, nb)
    return (y_p.reshape(batch, seq, d), shift_p.reshape(batch, d), ret_p, rw_p,
            y_s.reshape(nb, 1, d), h_s, ret_s, rw_s)


def kernel(x_prompt, x_sample, state_ret, state_rwkv, state_shift, p_prompt, p_sample, norm_g, w_in, rw_mu, rw_w0, rw_w2, rw_a0, rw_a2, rw_k_k, rw_k_a, rw_r_k, rw_ln_w, rw_ln_b, w_down_a, w_down_b, w_out, w_ple, ple_norm_g, w_ple_gate, final_norm_g):
    assert norm_g.shape[0] == 1, "single-layer step"
    p = dict(norm_g=norm_g[0], w_in=w_in[0], rw_mu=rw_mu[0], rw_w0=rw_w0[0], rw_w2=rw_w2[0],
             rw_a0=rw_a0[0], rw_a2=rw_a2[0], rw_k_k=rw_k_k[0], rw_k_a=rw_k_a[0], rw_r_k=rw_r_k[0],
             rw_ln_w=rw_ln_w[0], rw_ln_b=rw_ln_b[0], w_down_a=w_down_a[0], w_down_b=w_down_b[0],
             w_out=w_out[0], w_ple=w_ple[0], ple_norm_g=ple_norm_g[0], w_ple_gate=w_ple_gate[0],
             final_norm_g=final_norm_g)
    w = _layer_weights(p)
    cs = _rwkv_consts(p)
    y_p, sh_p, ret_p, rw_p, y_s, sh_s, ret_s, rw_s = _layer_paths(
        x_prompt, p_prompt[0], x_sample, state_shift[0], state_ret[0], state_rwkv, p_sample[0], p, w, cs)
    return (y_p, y_s, ret_p[None], rw_p[None], sh_p[None], ret_s[None], rw_s, sh_s[None])
```
